```python
import jax, jax.numpy as jnp
from jax import lax
import numpy as np

D_MODEL = 1024
BATCH = 4
SEQ = 4096
DEPTH = 1

GRID_W = 64
CTX_LEN = 256
EPS = 1e-6

D_RNN = D_MODEL
RG_BLOCKS = 16
RG_BW = D_RNN // RG_BLOCKS
RG_CONV = 4
RG_C = 8.0

DN_HEADS = 8
DN_DK = 128
DN_DV = 128
DN_QK = DN_HEADS * DN_DK
DN_VW = DN_HEADS * DN_DV
DN_QKV = 2 * DN_QK + DN_VW
DN_CONV = 4
DN_CHUNK = 64

N_EXPERTS = 32
TOP_K = 4
D_EXPERT = D_MODEL
SWIGLU_LIMIT = 7.0
SWIGLU_ALPHA = 1.702

IN_SIZES = (D_RNN, D_RNN, DN_QK, DN_QK, DN_VW, DN_VW, 2 * DN_HEADS, 2 * DN_HEADS, D_MODEL, D_MODEL)
N_IN = sum(IN_SIZES)

kernel_name = 'hybrid_rglru_gdn_moe_dit_block'


def rmsnorm(x, g):
    xf = x.astype(jnp.float32)
    y = xf * lax.rsqrt(jnp.mean(xf * xf, axis=-1, keepdims=True) + EPS)
    return (y * g.astype(jnp.float32)).astype(x.dtype)


def modulate(h, shift, scale):
    return h * (1 + scale) + shift


def flip(t):
    return jnp.flip(t, axis=1)


def split_cols(p):
    offs, o = [], 0
    for s in IN_SIZES[:-1]:
        o += s
        offs.append(o)
    return jnp.split(p, offs, axis=-1)


def causal_dwconv(x, w, b=None):
    K, C = w.shape
    y = lax.conv_general_dilated(x, w[:, None, :].astype(x.dtype), window_strides=(1,),
                                 padding=[(K - 1, 0)], dimension_numbers=('NWC', 'WIO', 'NWC'),
                                 feature_group_count=C)
    return y if b is None else y + b


def linear_scan(a, b, h0):
    if h0 is not None:
        b = b.at[:, 0].add(a[:, 0] * h0)

    def combine(l, r):
        return (l[0] * r[0], r[0] * l[1] + r[1])

    _, h = lax.associative_scan(combine, (a, b), axis=1)
    return h


def rglru_coeffs(x, wa, ba, wi, bi, lam, reset_first):
    B, T, _ = x.shape
    xf = x.astype(jnp.float32)
    xb = xf.reshape(B, T, RG_BLOCKS, RG_BW)
    r = jax.nn.sigmoid(jnp.einsum('btnd,nde->btne', xb, wa.astype(jnp.float32)).reshape(B, T, D_RNN) + ba)
    i = jax.nn.sigmoid(jnp.einsum('btnd,nde->btne', xb, wi.astype(jnp.float32)).reshape(B, T, D_RNN) + bi)
    log_a = -RG_C * r * jax.nn.softplus(-lam.astype(jnp.float32))
    a = jnp.exp(log_a)
    mult = jnp.sqrt(-jnp.expm1(2.0 * log_a))
    if reset_first:
        mult = mult.at[:, 0].set(1.0)
    return a, mult * (i * xf)


def rglru_direction(xc, xl, conv_w, conv_b, wa, ba, wi, bi, lam):
    xc = causal_dwconv(xc, conv_w, conv_b)
    xl = causal_dwconv(xl, conv_w, conv_b)
    a_c, b_c = rglru_coeffs(xc, wa, ba, wi, bi, lam, True)
    h_c = linear_scan(a_c, b_c, None)
    a_l, b_l = rglru_coeffs(xl, wa, ba, wi, bi, lam, False)
    h_l = linear_scan(a_l, b_l, h_c[:, -1])
    return h_c, h_l


def rglru_bidir(xc, xl, conv_w, conv_b, wa, ba, wi, bi, lam):
    fc, fl = rglru_direction(xc, xl, conv_w[0], conv_b[0], wa[0], ba[0], wi[0], bi[0], lam[0])
    bc, bl = rglru_direction(flip(xc), flip(xl), conv_w[1], conv_b[1], wa[1], ba[1], wi[1], bi[1], lam[1])
    return fc + flip(bc), fl + flip(bl)


def l2norm(t):
    return t * lax.rsqrt(jnp.sum(t * t, axis=-1, keepdims=True) + EPS)


def dn_prepare(qkv, alpha, beta_logit, conv_w, a_log, dt_bias):
    B, T, _ = qkv.shape
    qkv = jax.nn.silu(causal_dwconv(qkv, conv_w)).astype(jnp.float32)
    q, k, v = jnp.split(qkv, [DN_QK, 2 * DN_QK], axis=-1)
    q = l2norm(q.reshape(B, T, DN_HEADS, DN_DK)) * (DN_DK ** -0.5)
    k = l2norm(k.reshape(B, T, DN_HEADS, DN_DK))
    v = v.reshape(B, T, DN_HEADS, DN_DV)
    g = -jnp.exp(a_log.astype(jnp.float32)) * jax.nn.softplus(alpha.astype(jnp.float32) + dt_bias.astype(jnp.float32))
    beta = jax.nn.sigmoid(beta_logit.astype(jnp.float32))
    return q, k, v, g, beta


def chunk_gated_delta(q, k, v, g, beta, h0):
    B, T, H, DK = q.shape
    DV = v.shape[-1]
    N = T // DN_CHUNK

    def chunks(t):
        return t.reshape(B, N, DN_CHUNK, H, -1).transpose(1, 0, 3, 2, 4)

    qc, kc, vc = chunks(q), chunks(k), chunks(v)
    gc = chunks(g[..., None])[..., 0]
    bc = chunks(beta[..., None])[..., 0]
    gcum = jnp.cumsum(gc, axis=-1)
    idx = jnp.arange(DN_CHUNK)
    causal = idx[:, None] >= idx[None, :]
    strict = idx[:, None] > idx[None, :]
    decay = jnp.where(causal, jnp.exp(jnp.where(causal, gcum[..., :, None] - gcum[..., None, :], 0.0)), 0.0)
    kb = kc * bc[..., None]
    lower = jnp.where(strict, jnp.einsum('nbhid,nbhjd->nbhij', kb, kc) * decay, 0.0)
    tri = lower + jnp.eye(DN_CHUNK, dtype=jnp.float32)
    rhs = jnp.concatenate([vc * bc[..., None], kb * jnp.exp(gcum)[..., None]], axis=-1)
    sol = lax.linalg.triangular_solve(tri, rhs, left_side=True, lower=True)
    u, w = sol[..., :DV], sol[..., DV:]
    qk = jnp.einsum('nbhid,nbhjd->nbhij', qc, kc) * decay

    def step(h, inp):
        qi, ki, ui, wi, gi, qki = inp
        v_new = ui - jnp.einsum('bhck,bhkv->bhcv', wi, h)
        o = jnp.einsum('bhck,bhkv->bhcv', qi * jnp.exp(gi)[..., None], h) + jnp.einsum('bhij,bhjv->bhiv', qki, v_new)
        g_last = gi[..., -1:]
        h = h * jnp.exp(g_last)[..., None] + jnp.einsum('bhck,bhcv->bhkv', ki * jnp.exp(g_last - gi)[..., None], v_new)
        return h, o

    h, o = lax.scan(step, h0, (qc, kc, u, w, gcum, qk))
    return o.transpose(1, 0, 3, 2, 4).reshape(B, T, H, DV), h


def delta_direction(qkv_c, qkv_l, al_c, al_l, be_c, be_l, conv_w, a_log, dt_bias):
    qc, kc, vc, gc, bc = dn_prepare(qkv_c, al_c, be_c, conv_w, a_log, dt_bias)
    h0 = jnp.zeros((qkv_c.shape[0], DN_HEADS, DN_DK, DN_DV), jnp.float32)
    oc, hc = chunk_gated_delta(qc, kc, vc, gc, bc, h0)
    ql, kl, vl, gl, bl = dn_prepare(qkv_l, al_l, be_l, conv_w, a_log, dt_bias)
    ol, _ = chunk_gated_delta(ql, kl, vl, gl, bl, hc)
    return oc, ol


def delta_bidir(qkv_c, qkv_l, al_c, al_l, be_c, be_l, conv_w, a_log, dt_bias):
    H = DN_HEADS
    fc, fl = delta_direction(qkv_c, qkv_l, al_c[..., :H], al_l[..., :H], be_c[..., :H], be_l[..., :H],
                             conv_w[0], a_log[0], dt_bias[0])
    bc, bl = delta_direction(flip(qkv_c), flip(qkv_l), flip(al_c[..., H:]), flip(al_l[..., H:]),
                             flip(be_c[..., H:]), flip(be_l[..., H:]), conv_w[1], a_log[1], dt_bias[1])
    oc = fc + flip(bc)
    ol = fl + flip(bl)
    return oc.reshape(oc.shape[0], oc.shape[1], -1), ol.reshape(ol.shape[0], ol.shape[1], -1)


def merge(parts, rg_h, dn_o, dn_norm_g, rg_w_o, dn_w_o, w_out):
    _, rg_gate, _, _, _, z, _, _, gate_rg, gate_dn = parts
    dt = rg_gate.dtype
    B, T, _ = rg_gate.shape
    y_rg = (rg_h * jax.nn.gelu(rg_gate.astype(jnp.float32))).astype(dt) @ rg_w_o
    o = rmsnorm(dn_o.reshape(B, T, DN_HEADS, DN_DV), dn_norm_g) * jax.nn.silu(z.astype(jnp.float32).reshape(B, T, DN_HEADS, DN_DV))
    y_dn = o.reshape(B, T, DN_VW).astype(dt) @ dn_w_o
    merged = jax.nn.sigmoid(gate_rg) * y_rg + jax.nn.sigmoid(gate_dn) * y_dn
    return merged @ w_out


def token_mixer(h_lat, h_ctx, rows, need_ctx, w_in, rg_conv_w, rg_conv_b, rg_wa, rg_ba, rg_wi, rg_bi, rg_lam,
                rg_w_o, dn_conv_w, dn_a_log, dn_dt_bias, dn_norm_g, dn_w_o, w_out):
    B, S, _ = h_lat.shape
    pl = split_cols(h_lat @ w_in)
    pc = split_cols(h_ctx @ w_in)
    rg_c, rg_l = rglru_bidir(pc[0], pl[0], rg_conv_w, rg_conv_b, rg_wa, rg_ba, rg_wi, rg_bi, rg_lam)

    def to_col(t):
        return t.reshape(B, rows, GRID_W, -1).transpose(0, 2, 1, 3).reshape(B, S, -1)

    def from_col(t):
        return t.reshape(B, GRID_W, rows, -1).transpose(0, 2, 1, 3).reshape(B, S, -1)

    qkv_c = jnp.concatenate(pc[2:5], axis=-1)
    qkv_l = to_col(jnp.concatenate(pl[2:5], axis=-1))
    dn_c, dn_l = delta_bidir(qkv_c, qkv_l, pc[6], to_col(pl[6]), pc[7], to_col(pl[7]),
                             dn_conv_w, dn_a_log, dn_dt_bias)
    dn_l = from_col(dn_l)
    y_lat = merge(pl, rg_l, dn_l, dn_norm_g, rg_w_o, dn_w_o, w_out)
    y_ctx = merge(pc, rg_c, dn_c, dn_norm_g, rg_w_o, dn_w_o, w_out) if need_ctx else None
    return y_lat, y_ctx


def moe(tok, router_w, router_b, e_w1, e_b1, e_w2, e_b2):
    logits = (tok @ router_w).astype(jnp.float32) + router_b
    top_v, top_i = lax.top_k(logits, TOP_K)
    top_w = jax.nn.softmax(top_v, axis=-1)
    combine = jnp.sum(jax.nn.one_hot(top_i, N_EXPERTS, dtype=jnp.float32) * top_w[..., None], axis=1)

    def expert(acc, p):
        w1, b1, w2, b2, cw = p
        hh = (tok @ w1 + b1).astype(jnp.float32)
        glu, lin = jnp.split(hh, 2, axis=-1)
        glu = jnp.minimum(glu, SWIGLU_LIMIT)
        lin = jnp.clip(lin, -SWIGLU_LIMIT, SWIGLU_LIMIT)
        act = glu * jax.nn.sigmoid(SWIGLU_ALPHA * glu) * (lin + 1.0)
        out = act.astype(tok.dtype) @ w2 + b2
        return acc + cw[:, None] * out.astype(jnp.float32), None

    acc, _ = lax.scan(expert, jnp.zeros(tok.shape, jnp.float32), (e_w1, e_b1, e_w2, e_b2, combine.T))
    return acc.astype(tok.dtype)


def setup_inputs(seed: int = 0) -> dict:
    key = jax.random.key(seed)
    ks = jax.random.split(key, 31)
    f32 = jnp.float32
    D, L = D_MODEL, DEPTH

    def nrm(k, shape, scale):
        return jax.random.normal(k, shape, f32) * scale

    def gain(k, shape):
        return 1.0 + 0.1 * jax.random.normal(k, shape, f32)

    u = jax.random.uniform(ks[15], (L, 2, D_RNN), f32, 0.9, 0.999)
    s = u ** (1.0 / RG_C)
    rg_lam = jnp.log(s) - jnp.log1p(-s)
    dn_a_log = jnp.log(jax.random.uniform(ks[18], (L, 2, DN_HEADS), f32, 1.0, 16.0))
    return {
        'x': nrm(ks[0], (BATCH, SEQ, D), 1.0),
        'c': nrm(ks[1], (BATCH, D), 1.0),
        'ctx': nrm(ks[2], (BATCH, CTX_LEN, D), 1.0),
        'c_ctx': nrm(ks[3], (D,), 1.0),
        'ada_w': nrm(ks[4], (L, D, 6 * D), 0.5 * D ** -0.5),
        'ada_b': nrm(ks[5], (L, 6 * D), 0.02),
        'mix_pre_g': gain(ks[6], (L, D)),
        'mix_post_g': gain(ks[7], (L, D)),
        'w_in': nrm(ks[8], (L, D, N_IN), D ** -0.5),
        'rg_conv_w': nrm(ks[9], (L, 2, RG_CONV, D_RNN), RG_CONV ** -0.5),
        'rg_conv_b': nrm(ks[10], (L, 2, D_RNN), 0.02),
        'rg_wa': nrm(ks[11], (L, 2, RG_BLOCKS, RG_BW, RG_BW), RG_BW ** -0.5),
        'rg_ba': nrm(ks[12], (L, 2, D_RNN), 0.02),
        'rg_wi': nrm(ks[13], (L, 2, RG_BLOCKS, RG_BW, RG_BW), RG_BW ** -0.5),
        'rg_bi': nrm(ks[14], (L, 2, D_RNN), 0.02),
        'rg_lam': rg_lam,
        'rg_w_o': nrm(ks[16], (L, D_RNN, D), D_RNN ** -0.5),
        'dn_conv_w': nrm(ks[17], (L, 2, DN_CONV, DN_QKV), DN_CONV ** -0.5),
        'dn_a_log': dn_a_log,
        'dn_dt_bias': gain(ks[19], (L, 2, DN_HEADS)),
        'dn_norm_g': gain(ks[20], (L, DN_DV)),
        'dn_w_o': nrm(ks[21], (L, DN_VW, D), DN_VW ** -0.5),
        'w_out': nrm(ks[22], (L, D, D), D ** -0.5),
        'ffn_pre_g': gain(ks[23], (L, D)),
        'ffn_post_g': gain(ks[24], (L, D)),
        'router_w': nrm(ks[25], (L, D, N_EXPERTS), D ** -0.5),
        'router_b': nrm(ks[26], (L, N_EXPERTS), 0.01),
        'e_w1': nrm(ks[27], (L, N_EXPERTS, D, 2 * D_EXPERT), D ** -0.5),
        'e_b1': nrm(ks[28], (L, N_EXPERTS, 2 * D_EXPERT), 0.01),
        'e_w2': nrm(ks[29], (L, N_EXPERTS, D_EXPERT, D), D_EXPERT ** -0.5),
        'e_b2': nrm(ks[30], (L, N_EXPERTS, D), 0.01),
    }


def reference(x, c, ctx, c_ctx, ada_w, ada_b, mix_pre_g, mix_post_g, w_in, rg_conv_w, rg_conv_b, rg_wa, rg_ba,
              rg_wi, rg_bi, rg_lam, rg_w_o, dn_conv_w, dn_a_log, dn_dt_bias, dn_norm_g, dn_w_o, w_out,
              ffn_pre_g, ffn_post_g, router_w, router_b, e_w1, e_b1, e_w2, e_b2):
    B, S, D = x.shape
    Lc = ctx.shape[1]
    rows = S // GRID_W
    for l in range(DEPTH):
        last = l == DEPTH - 1
        mod_lat = jax.nn.silu(c) @ ada_w[l] + ada_b[l]
        mod_ctx = jax.nn.silu(c_ctx) @ ada_w[l] + ada_b[l]
        sh1, sc1, g1, sh2, sc2, g2 = jnp.split(mod_lat[:, None, :], 6, axis=-1)
        sh1c, sc1c, g1c, sh2c, sc2c, g2c = jnp.split(mod_ctx, 6)

        h_lat = modulate(rmsnorm(x, mix_pre_g[l]), sh1, sc1)
        h_ctx = modulate(rmsnorm(ctx, mix_pre_g[l]), sh1c, sc1c)
        y_lat, y_ctx = token_mixer(h_lat, h_ctx, rows, not last, w_in[l], rg_conv_w[l], rg_conv_b[l], rg_wa[l],
                                   rg_ba[l], rg_wi[l], rg_bi[l], rg_lam[l], rg_w_o[l], dn_conv_w[l], dn_a_log[l],
                                   dn_dt_bias[l], dn_norm_g[l], dn_w_o[l], w_out[l])
        x = x + g1 * rmsnorm(y_lat, mix_post_g[l])

        h_f = modulate(rmsnorm(x, ffn_pre_g[l]), sh2, sc2).reshape(B * S, D)
        if not last:
            ctx = ctx + g1c * rmsnorm(y_ctx, mix_post_g[l])
            h_fc = modulate(rmsnorm(ctx, ffn_pre_g[l]), sh2c, sc2c).reshape(B * Lc, D)
            f = moe(jnp.concatenate([h_f, h_fc], axis=0), router_w[l], router_b[l], e_w1[l], e_b1[l], e_w2[l], e_b2[l])
            f_lat = f[:B * S]
            ctx = ctx + g2c * rmsnorm(f[B * S:].reshape(B, Lc, D), ffn_post_g[l])
        else:
            f_lat = moe(h_f, router_w[l], router_b[l], e_w1[l], e_b1[l], e_w2[l], e_b2[l])
        x = x + g2 * rmsnorm(f_lat.reshape(B, S, D), ffn_post_g[l])
    return x
```

```python
import functools

import jax
import jax.numpy as jnp
from jax import lax
from jax.experimental import pallas as pl
from jax.experimental.pallas import tpu as pltpu

F32 = jnp.float32
BF16 = jnp.bfloat16
HIGHEST = lax.Precision.HIGHEST

D = 1024
EPS = 1e-6
GRID_W = 64
CHUNK = 64
HEADS = 8
DK = 128
RG_C = 8.0
RG_BW = 64
N_EXPERTS = 32
TOP_K = 4
SWIGLU_LIMIT = 7.0
SWIGLU_ALPHA = 1.702
LANES = 128
MOE_TM = 256
VMEM_LIMIT = 56 * 1024 * 1024


def _cparams(sem):
    return pltpu.CompilerParams(dimension_semantics=sem, vmem_limit_bytes=VMEM_LIMIT)


def _sigmoid(x):
    return 1.0 / (1.0 + jnp.exp(-x))


def _softplus(y):
    return jnp.maximum(y, 0.0) + jnp.log1p(jnp.exp(-jnp.abs(y)))


def _nt_dot(a, b):
    return lax.dot_general(a, b, (((1,), (1,)), ((), ())), preferred_element_type=F32)


def _ada_kernel(c_ref, w_ref, b_ref, o_ref):
    c = c_ref[...]
    a = c * _sigmoid(c)
    o_ref[...] = jnp.dot(a, w_ref[...], preferred_element_type=F32, precision=HIGHEST) + b_ref[...]


def _ada_mod(c8, ada_w, ada_b):
    n = ada_w.shape[1]
    tn = 1024
    return pl.pallas_call(
        _ada_kernel,
        grid=(n // tn,),
        in_specs=[pl.BlockSpec((8, D), lambda j: (0, 0)),
                  pl.BlockSpec((D, tn), lambda j: (0, j)),
                  pl.BlockSpec((1, tn), lambda j: (0, j))],
        out_specs=pl.BlockSpec((8, tn), lambda j: (0, j)),
        out_shape=jax.ShapeDtypeStruct((8, n), F32),
        compiler_params=_cparams(("arbitrary",)),
        name="ada_mod",
    )(c8, ada_w, ada_b)


def _inproj_kernel(x_ref, g_ref, mod_ref, w_ref, wab_ref, o_ref, ab_ref, h_scr, *, tiles_per_batch, row_offset):
    i = pl.program_id(0)
    j = pl.program_id(1)

    @pl.when(j == 0)
    def _():
        x = x_ref[...]
        y = x * lax.rsqrt(jnp.mean(x * x, axis=-1, keepdims=True) + EPS) * g_ref[...]
        row = row_offset + i // tiles_per_batch
        sh = mod_ref[pl.ds(row, 1), 0:D]
        sc = mod_ref[pl.ds(row, 1), D:2 * D]
        h = (y * (1.0 + sc) + sh).astype(BF16)
        h_scr[...] = h
        ab_ref[...] = jnp.dot(h, wab_ref[...], preferred_element_type=F32)

    o_ref[0] = jnp.dot(h_scr[...], w_ref[0], preferred_element_type=F32).astype(BF16)


def _inproj(x2d, g, mod, w_main, w_ab, *, tm, tiles_per_batch, row_offset):
    n = x2d.shape[0]
    nj = w_main.shape[0]
    kern = functools.partial(_inproj_kernel, tiles_per_batch=tiles_per_batch, row_offset=row_offset)
    return pl.pallas_call(
        kern,
        grid=(n // tm, nj),
        in_specs=[pl.BlockSpec((tm, D), lambda i, j: (i, 0)),
                  pl.BlockSpec((1, D), lambda i, j: (0, 0)),
                  pl.BlockSpec((8, 6 * D), lambda i, j: (0, 0)),
                  pl.BlockSpec((1, D, D), lambda i, j: (j, 0, 0)),
                  pl.BlockSpec((D, LANES), lambda i, j: (0, 0))],
        out_specs=[pl.BlockSpec((1, tm, D), lambda i, j: (j, i, 0)),
                   pl.BlockSpec((tm, LANES), lambda i, j: (i, 0))],
        out_shape=[jax.ShapeDtypeStruct((nj, n, D), BF16),
                   jax.ShapeDtypeStruct((n, LANES), F32)],
        scratch_shapes=[pltpu.VMEM((tm, D), BF16)],
        compiler_params=_cparams(("arbitrary", "arbitrary")),
        name="inproj",
    )(x2d, g, mod, w_main, w_ab)


RG_HALO = 16


def _rg_coef_kernel(cur_ref, prev_ref, next_ref, cw_ref, cb_ref, wblk_ref, ba_ref, bi_ref, lam_ref,
                    laf_ref, bbf_ref, lab_ref, bbb_ref, *, tt, nt, reset_first):
    i = pl.program_id(1)
    x = cur_ref[0].astype(F32)
    xp = jnp.where(i > 0, prev_ref[0].astype(F32), 0.0)
    xn = jnp.where(i < nt - 1, next_ref[0].astype(F32), 0.0)
    xe_f = jnp.concatenate([xp, x], axis=0)
    xe_b = jnp.concatenate([x, xn], axis=0)
    n_ext = tt + RG_HALO
    row = lax.broadcasted_iota(jnp.int32, (tt, 1), 0)
    outs = ((laf_ref, bbf_ref), (lab_ref, bbb_ref))
    for d in (0, 1):
        w = cw_ref[d]
        acc = x * w[3:4]
        for sft in (1, 2, 3):
            if d == 0:
                sh = pltpu.roll(xe_f, sft, axis=0)[RG_HALO:RG_HALO + tt]
            else:
                sh = pltpu.roll(xe_b, n_ext - sft, axis=0)[0:tt]
            acc = acc + sh * w[3 - sft:4 - sft]
        xc = acc + cb_ref[d:d + 1]
        xcb = xc.astype(BF16)
        pre_a = jnp.concatenate(
            [jnp.dot(xcb[:, g * 256:(g + 1) * 256], wblk_ref[d, 0, g], preferred_element_type=F32)
             for g in range(4)], axis=1) + ba_ref[d:d + 1]
        pre_i = jnp.concatenate(
            [jnp.dot(xcb[:, g * 256:(g + 1) * 256], wblk_ref[d, 1, g], preferred_element_type=F32)
             for g in range(4)], axis=1) + bi_ref[d:d + 1]
        r = _sigmoid(pre_a)
        ig = _sigmoid(pre_i)
        la = r * (-RG_C * _softplus(-lam_ref[d:d + 1]))
        a = jnp.exp(la)
        mult = jnp.sqrt(1.0 - a * a)
        if reset_first:
            if d == 0:
                first = (row == 0) & (i == 0)
            else:
                first = (row == tt - 1) & (i == nt - 1)
            mult = jnp.where(first, 1.0, mult)
        la_ref, bb_ref = outs[d]
        la_ref[...] = la.astype(BF16)
        bb_ref[...] = (mult * ig * xc).astype(BF16)


def _rg_coef(xr, conv_w, conv_b, wblk, ba, bi, lam, *, tt, reset_first):
    b, t, _ = xr.shape
    nt = t // tt
    hb = tt // RG_HALO
    kern = functools.partial(_rg_coef_kernel, tt=tt, nt=nt, reset_first=reset_first)
    full = lambda shape: pl.BlockSpec(shape, lambda bb, i: (0,) * len(shape))
    out_sds = jax.ShapeDtypeStruct((t, b * D), BF16)
    out_spec = pl.BlockSpec((tt, D), lambda bb, i: (i, bb))
    return pl.pallas_call(
        kern,
        grid=(b, nt),
        in_specs=[pl.BlockSpec((1, tt, D), lambda bb, i: (bb, i, 0)),
                  pl.BlockSpec((1, RG_HALO, D), lambda bb, i: (bb, jnp.maximum(i * hb - 1, 0), 0)),
                  pl.BlockSpec((1, RG_HALO, D), lambda bb, i: (bb, jnp.minimum((i + 1) * hb, t // RG_HALO - 1), 0)),
                  full((2, 4, D)), full((2, D)), full((2, 2, 4, 256, 256)),
                  full((2, D)), full((2, D)), full((2, D))],
        out_specs=[out_spec] * 4,
        out_shape=[out_sds] * 4,
        compiler_params=_cparams(("arbitrary", "arbitrary")),
        name="rg_coef",
    )(xr, xr, xr, conv_w, conv_b, wblk, ba, bi, lam)


SCAN_ROWS = 8
SCAN_W = 512


def _rg_scan_run(la_f, bb_f, la_b, bb_b, hf, hb, carry, n):
    npair = n // 2

    def body(u, c):
        h_f, h_b = c
        r0 = pl.multiple_of(u * 16, 16)
        a = jnp.exp(la_f[pl.ds(r0, 16), :].astype(F32))
        bv = bb_f[pl.ds(r0, 16), :].astype(F32)
        f1 = a[0:8] * h_f + bv[0:8]
        f2 = a[8:16] * f1 + bv[8:16]
        if hf is not None:
            hf[pl.ds(r0, 16), :] = jnp.concatenate([f1, f2], axis=0).astype(BF16)
        r1 = pl.multiple_of((npair - 1 - u) * 16, 16)
        a2 = jnp.exp(la_b[pl.ds(r1, 16), :].astype(F32))
        bv2 = bb_b[pl.ds(r1, 16), :].astype(F32)
        g2 = a2[8:16] * h_b + bv2[8:16]
        g1 = a2[0:8] * g2 + bv2[0:8]
        if hb is not None:
            hb[pl.ds(r1, 16), :] = jnp.concatenate([g1, g2], axis=0).astype(BF16)
        return f2, g1

    h_f, h_b = lax.fori_loop(0, npair, body, (carry[0], carry[1]), unroll=4)
    carry[0] = h_f
    carry[1] = h_b


def _rg_scan_kernel(laf_c, bbf_c, lab_c, bbb_c, laf_l, bbf_l, lab_l, bbb_l, hf_ref, hb_ref, carry, *, t_ctx, tt):
    s = pl.program_id(0)

    @pl.when(s == 0)
    def _():
        carry[...] = jnp.zeros(carry.shape, F32)
        _rg_scan_run(laf_c, bbf_c, lab_c, bbb_c, None, None, carry, t_ctx)

    @pl.when(s > 0)
    def _():
        _rg_scan_run(laf_l, bbf_l, lab_l, bbb_l, hf_ref, hb_ref, carry, tt)


def _rg_scan(coef_c, coef_l, *, t_ctx, t_lat, tt):
    nt = t_lat // tt
    view = lambda a: a.reshape(a.shape[0] * SCAN_ROWS, SCAN_W)
    cs = [view(a) for a in coef_c]
    ls = [view(a) for a in coef_l]
    ctx_spec = pl.BlockSpec((t_ctx * SCAN_ROWS, SCAN_W), lambda s: (0, 0))
    f_spec = pl.BlockSpec((tt * SCAN_ROWS, SCAN_W), lambda s: (jnp.maximum(s - 1, 0), 0))
    b_spec = pl.BlockSpec((tt * SCAN_ROWS, SCAN_W), lambda s: (nt - jnp.maximum(s, 1), 0))
    out_sds = jax.ShapeDtypeStruct((t_lat * SCAN_ROWS, SCAN_W), BF16)
    kern = functools.partial(_rg_scan_kernel, t_ctx=t_ctx, tt=tt)
    hf, hb = pl.pallas_call(
        kern,
        grid=(nt + 1,),
        in_specs=[ctx_spec] * 4 + [f_spec, f_spec, b_spec, b_spec],
        out_specs=[f_spec, b_spec],
        out_shape=[out_sds, out_sds],
        scratch_shapes=[pltpu.VMEM((2, SCAN_ROWS, SCAN_W), F32)],
        compiler_params=_cparams(("arbitrary",)),
        name="rg_scan",
    )(*cs, *ls)
    return hf, hb


N_CTX_CHUNKS = 4


def _delta_dir(d, is_ctx, refs_l, refs_c, out_ref, write_out, state, prevraw, cw_ref, alog_ref, dtb_ref):
    q_l, k_l, v_l, ab_l = refs_l
    q_c, k_c, v_c, ab_c = refs_c

    def sel(rc, rl):
        return jnp.where(is_ctx, rc[0, 0], rl[0, 0])

    raw = jnp.concatenate([sel(q_c, q_l), sel(k_c, k_l), sel(v_c, v_l)], axis=1).astype(F32)
    abv = jnp.where(is_ctx, ab_c[0], ab_l[0])
    prev = prevraw[d]
    w = cw_ref[d]
    acc = raw * w[3:4]
    if d == 0:
        xe = jnp.concatenate([prev, raw], axis=0)
        for sft in (1, 2, 3):
            acc = acc + pltpu.roll(xe, sft, axis=0)[8:8 + CHUNK] * w[3 - sft:4 - sft]
        prevraw[d] = raw[CHUNK - 8:CHUNK]
    else:
        xe = jnp.concatenate([raw, prev], axis=0)
        for sft in (1, 2, 3):
            acc = acc + pltpu.roll(xe, CHUNK + 8 - sft, axis=0)[0:CHUNK] * w[3 - sft:4 - sft]
        prevraw[d] = raw[0:8]
    y = acc * _sigmoid(acc)

    ri = lax.broadcasted_iota(jnp.int32, (CHUNK, CHUNK), 0)
    ci = lax.broadcasted_iota(jnp.int32, (CHUNK, CHUNK), 1)
    if d == 0:
        incl, strict, last = ri >= ci, ri > ci, CHUNK - 1
    else:
        incl, strict, last = ri <= ci, ri < ci, 0
    eye = (ri == ci).astype(F32)
    rp = lax.broadcasted_iota(jnp.int32, (2 * CHUNK, CHUNK), 0)
    cp = lax.broadcasted_iota(jnp.int32, (2 * CHUNK, CHUNK), 1)
    if d == 0:
        m_pad = ((rp >= cp) & (rp < CHUNK)).astype(F32)
    else:
        m_pad = ((rp <= cp) & (rp < CHUNK)).astype(F32)

    g_all = -jnp.exp(alog_ref[...]) * _softplus(abv + dtb_ref[...])
    beta_all = _sigmoid(abv)
    gcum = jnp.dot(m_pad, g_all, preferred_element_type=F32, precision=HIGHEST)
    gcum_t = gcum.T
    lane2 = lax.broadcasted_iota(jnp.int32, (CHUNK, 2 * CHUNK), 1)
    keep = lane2 >= CHUNK
    zeros_half = jnp.zeros((CHUNK, DK), F32)

    for h in range(HEADS):
        la = d * HEADS + h
        lb = 2 * HEADS + d * HEADS + h
        gcol = gcum[0:CHUNK, la:la + 1]
        grow = gcum_t[la:la + 1, 0:CHUNK]
        bcol = beta_all[:, lb:lb + 1]
        qh = y[:, h * DK:(h + 1) * DK]
        kh = y[:, D + h * DK:D + (h + 1) * DK]
        vh = y[:, 2 * D + h * DK:2 * D + (h + 1) * DK]
        qn = qh * lax.rsqrt(jnp.sum(qh * qh, axis=-1, keepdims=True) + EPS) * (DK ** -0.5)
        kn = kh * lax.rsqrt(jnp.sum(kh * kh, axis=-1, keepdims=True) + EPS)
        kbeta = kn * bcol
        knb = kn.astype(BF16)
        a_mat = _nt_dot(kbeta.astype(BF16), knb)
        qk = _nt_dot(qn.astype(BF16), knb)
        dec = jnp.where(incl, jnp.exp(jnp.where(incl, gcol - grow, 0.0)), 0.0)
        s0 = jnp.where(strict, -a_mat * dec, 0.0)
        qmat = jnp.concatenate([s0, eye], axis=1)
        for _ in range(6):
            qmat = jnp.dot(qmat[:, 0:CHUNK].astype(BF16), qmat.astype(BF16),
                           preferred_element_type=F32) + jnp.where(keep, qmat, 0.0)
        tinv = qmat[:, CHUNK:2 * CHUNK]
        eg = jnp.exp(gcol)
        rhs = jnp.concatenate([vh * bcol, kbeta * eg], axis=1)
        sol = jnp.dot(tinv.astype(BF16), rhs.astype(BF16), preferred_element_type=F32)
        u = sol[:, 0:DK]
        wm = sol[:, DK:2 * DK]
        hst = state[d, h]
        wq = jnp.concatenate([wm, qn * eg], axis=0)
        wqh = jnp.dot(wq.astype(BF16), hst.astype(BF16), preferred_element_type=F32)
        v_new = u - wqh[0:CHUNK]
        o = wqh[CHUNK:2 * CHUNK] + jnp.dot((qk * dec).astype(BF16), v_new.astype(BF16),
                                             preferred_element_type=F32)
        glast = gcum[last:last + 1, la:la + 1]
        kd = kn * jnp.exp(glast - gcol)
        kd_t = jnp.concatenate([kd, zeros_half], axis=0).T
        v_pad = jnp.concatenate([v_new, zeros_half], axis=0)
        state[d, h] = hst * jnp.exp(glast) + jnp.dot(kd_t.astype(BF16), v_pad.astype(BF16),
                                                     preferred_element_type=F32)

        @pl.when(write_out)
        def _():
            out_ref[0, :, h * DK:(h + 1) * DK] = o.astype(BF16)


def _delta_kernel(qf_l, kf_l, vf_l, abf_l, qb_l, kb_l, vb_l, abb_l,
                  qf_c, kf_c, vf_c, abf_c, qb_c, kb_c, vb_c, abb_c,
                  cw_ref, alog_ref, dtb_ref, of_ref, ob_ref, state, prevraw):
    s = pl.program_id(1)
    is_ctx = s < N_CTX_CHUNKS

    @pl.when(s == 0)
    def _():
        state[...] = jnp.zeros(state.shape, F32)

    @pl.when((s == 0) | (s == N_CTX_CHUNKS))
    def _():
        prevraw[...] = jnp.zeros(prevraw.shape, F32)

    write_out = jnp.logical_not(is_ctx)
    _delta_dir(0, is_ctx, (qf_l, kf_l, vf_l, abf_l), (qf_c, kf_c, vf_c, abf_c), of_ref, write_out,
               state, prevraw, cw_ref, alog_ref, dtb_ref)
    _delta_dir(1, is_ctx, (qb_l, kb_l, vb_l, abb_l), (qb_c, kb_c, vb_c, abb_c), ob_ref, write_out,
               state, prevraw, cw_ref, alog_ref, dtb_ref)


def _delta(o_lat, ab_lat, o_ctx, ab_ctx, conv_w, alog_l, dtb_l, *, b, s_lat, t_ctx):
    rows = s_lat // GRID_W
    assert rows == CHUNK and t_ctx == N_CTX_CHUNKS * CHUNK
    ncol = GRID_W
    lat4 = o_lat.reshape(o_lat.shape[0], b, rows, GRID_W * D)
    ab3 = ab_lat.reshape(b, rows, GRID_W * LANES)
    ctx4 = o_ctx.reshape(o_ctx.shape[0], b, t_ctx, D)
    abc3 = ab_ctx.reshape(b, t_ctx, LANES)

    col_f = lambda s: jnp.maximum(s - N_CTX_CHUNKS, 0)
    col_b = lambda s: ncol - 1 - jnp.maximum(s - N_CTX_CHUNKS, 0)
    cch_f = lambda s: jnp.minimum(s, N_CTX_CHUNKS - 1)
    cch_b = lambda s: N_CTX_CHUNKS - 1 - jnp.minimum(s, N_CTX_CHUNKS - 1)

    def lat_spec(j, colfn):
        return pl.BlockSpec((1, 1, rows, D), lambda bb, s: (j, bb, 0, colfn(s)))

    def ctx_spec(j, chfn):
        return pl.BlockSpec((1, 1, CHUNK, D), lambda bb, s: (j, bb, chfn(s), 0))

    ab_l_spec = lambda colfn: pl.BlockSpec((1, rows, LANES), lambda bb, s: (bb, 0, colfn(s)))
    ab_c_spec = lambda chfn: pl.BlockSpec((1, CHUNK, LANES), lambda bb, s: (bb, chfn(s), 0))
    full = lambda shape: pl.BlockSpec(shape, lambda bb, s: (0,) * len(shape))

    in_specs = ([lat_spec(2, col_f), lat_spec(3, col_f), lat_spec(4, col_f), ab_l_spec(col_f),
                 lat_spec(2, col_b), lat_spec(3, col_b), lat_spec(4, col_b), ab_l_spec(col_b),
                 ctx_spec(2, cch_f), ctx_spec(3, cch_f), ctx_spec(4, cch_f), ab_c_spec(cch_f),
                 ctx_spec(2, cch_b), ctx_spec(3, cch_b), ctx_spec(4, cch_b), ab_c_spec(cch_b),
                 full((2, 4, 3 * D)), full((1, LANES)), full((1, LANES))])
    out_sds = jax.ShapeDtypeStruct((b, rows, GRID_W * D), BF16)
    out_specs = [pl.BlockSpec((1, rows, D), lambda bb, s: (bb, 0, col_f(s))),
                 pl.BlockSpec((1, rows, D), lambda bb, s: (bb, 0, col_b(s)))]
    of, ob = pl.pallas_call(
        _delta_kernel,
        grid=(b, N_CTX_CHUNKS + ncol),
        in_specs=in_specs,
        out_specs=out_specs,
        out_shape=[out_sds, out_sds],
        scratch_shapes=[pltpu.VMEM((2, HEADS, DK, DK), F32), pltpu.VMEM((2, 8, 3 * D), F32)],
        compiler_params=_cparams(("arbitrary", "arbitrary")),
        name="delta",
    )(lat4, lat4, lat4, ab3, lat4, lat4, lat4, ab3,
      ctx4, ctx4, ctx4, abc3, ctx4, ctx4, ctx4, abc3,
      conv_w, alog_l, dtb_l)
    return of.reshape(b * s_lat, D), ob.reshape(b * s_lat, D)


def _rms(x, g):
    return x * lax.rsqrt(jnp.mean(x * x, axis=-1, keepdims=True) + EPS) * g


def _post_kernel(x_ref, gate_ref, z_ref, grg_ref, gdn_ref, hf_ref, hb_ref, of_ref, ob_ref, mod_ref,
                 wrg_ref, wdn_ref, wout_ref, dng_ref, postg_ref, preg_ref, rw_ref, rb_ref,
                 x1_ref, hf2_ref, topi_ref, topw_ref, rank_ref, cnt_ref, carry, *, tm, tiles_per_batch):
    i = pl.program_id(0)
    row = i // tiles_per_batch

    @pl.when(i == 0)
    def _():
        carry[...] = jnp.zeros(carry.shape, F32)

    gate = gate_ref[0].astype(F32)
    gelu = 0.5 * gate * (1.0 + jnp.tanh(0.7978845608028654 * (gate + 0.044715 * gate * gate * gate)))
    rg_h = hf_ref[...].astype(F32) + hb_ref[...].astype(F32)
    y_rg = jnp.dot((rg_h * gelu).astype(BF16), wrg_ref[...], preferred_element_type=F32)

    dn = of_ref[...].astype(F32) + ob_ref[...].astype(F32)
    z = z_ref[0].astype(F32)
    parts = []
    for h in range(HEADS):
        seg = dn[:, h * DK:(h + 1) * DK]
        parts.append(seg * lax.rsqrt(jnp.mean(seg * seg, axis=-1, keepdims=True) + EPS))
    o = jnp.concatenate(parts, axis=1) * dng_ref[...] * (z * _sigmoid(z))
    y_dn = jnp.dot(o.astype(BF16), wdn_ref[...], preferred_element_type=F32)

    merged = _sigmoid(grg_ref[0].astype(F32)) * y_rg + _sigmoid(gdn_ref[0].astype(F32)) * y_dn
    y = jnp.dot(merged.astype(BF16), wout_ref[...], preferred_element_type=F32)

    g1 = mod_ref[pl.ds(row, 1), 2 * D:3 * D]
    sh2 = mod_ref[pl.ds(row, 1), 3 * D:4 * D]
    sc2 = mod_ref[pl.ds(row, 1), 4 * D:5 * D]
    x1 = x_ref[...] + g1 * _rms(y, postg_ref[...])
    x1_ref[...] = x1
    hf2 = _rms(x1, preg_ref[...]) * (1.0 + sc2) + sh2
    hf2_ref[...] = hf2

    lane = lax.broadcasted_iota(jnp.int32, (tm, LANES), 1)
    lane_f = lane.astype(F32)
    logits = jnp.dot(hf2, rw_ref[...], preferred_element_type=F32, precision=HIGHEST) + rb_ref[...]
    neg = jnp.float32(-jnp.inf)
    lg = jnp.where(lane < N_EXPERTS, logits, neg)
    vals, sels = [], []
    topi = jnp.zeros((tm, LANES), F32)
    onehot = jnp.zeros((tm, LANES), F32)
    for k in range(TOP_K):
        m = jnp.max(lg, axis=-1, keepdims=True)
        idx = jnp.min(jnp.where(lg == m, lane_f, float(LANES)), axis=-1, keepdims=True)
        sel = lane_f == idx
        vals.append(m)
        sels.append(sel)
        lg = jnp.where(sel, neg, lg)
        onehot = onehot + sel.astype(F32)
        topi = topi + jnp.where(lane == k, idx, 0.0)
    es = [jnp.exp(v - vals[0]) for v in vals]
    den = es[0] + es[1] + es[2] + es[3]
    topw = jnp.zeros((tm, LANES), F32)
    for k in range(TOP_K):
        topw = topw + jnp.where(lane == k, es[k] / den, 0.0)

    ri = lax.broadcasted_iota(jnp.int32, (tm, tm), 0)
    ci = lax.broadcasted_iota(jnp.int32, (tm, tm), 1)
    tri = (ri > ci).astype(BF16)
    cum = jnp.dot(tri, onehot.astype(BF16), preferred_element_type=F32) + carry[...]
    rank = jnp.zeros((tm, LANES), F32)
    for k in range(TOP_K):
        rk = jnp.sum(jnp.where(sels[k], cum, 0.0), axis=-1, keepdims=True)
        rank = rank + jnp.where(lane == k, rk, 0.0)
    new_carry = carry[...] + jnp.sum(onehot, axis=0, keepdims=True)
    carry[...] = new_carry
    topi_ref[...] = topi.astype(jnp.int32)
    topw_ref[...] = topw
    rank_ref[...] = rank.astype(jnp.int32)
    cnt_ref[...] = jnp.broadcast_to(new_carry, (8, LANES)).astype(jnp.int32)


def _post(x2d, o_lat, hf, hb, of, ob, mod, wrg, wdn, wout, dng, postg, preg, rw, rb, *, tm, s_lat):
    n = x2d.shape[0]
    tpb = s_lat // tm
    kern = functools.partial(_post_kernel, tm=tm, tiles_per_batch=tpb)
    tok = pl.BlockSpec((tm, D), lambda i: (i, 0))
    oj = lambda j: pl.BlockSpec((1, tm, D), lambda i: (j, i, 0))
    hspec = pl.BlockSpec((tm, D), lambda i: (i % tpb, i // tpb))
    full = lambda shape: pl.BlockSpec(shape, lambda i: (0,) * len(shape))
    lane_out = pl.BlockSpec((tm, LANES), lambda i: (i, 0))
    return pl.pallas_call(
        kern,
        grid=(n // tm,),
        in_specs=[tok, oj(1), oj(5), oj(6), oj(7), hspec, hspec, tok, tok, full((8, 6 * D)),
                  full((D, D)), full((D, D)), full((D, D)), full((1, D)), full((1, D)), full((1, D)),
                  full((D, LANES)), full((1, LANES))],
        out_specs=[tok, tok, lane_out, lane_out, lane_out, pl.BlockSpec((8, LANES), lambda i: (0, 0))],
        out_shape=[jax.ShapeDtypeStruct((n, D), F32), jax.ShapeDtypeStruct((n, D), F32),
                   jax.ShapeDtypeStruct((n, LANES), jnp.int32), jax.ShapeDtypeStruct((n, LANES), F32),
                   jax.ShapeDtypeStruct((n, LANES), jnp.int32), jax.ShapeDtypeStruct((8, LANES), jnp.int32)],
        scratch_shapes=[pltpu.VMEM((1, LANES), F32)],
        compiler_params=_cparams(("arbitrary",)),
        name="post_mix",
    )(x2d, o_lat, o_lat, o_lat, o_lat, hf, hb, of, ob, mod, wrg, wdn, wout, dng, postg, preg, rw, rb)


def _expert_kernel(te_ref, nu_ref, xs_ref, w1_ref, b1_ref, w2_ref, b2_ref, ys_ref, w1b, w2b):
    i = pl.program_id(0)

    @pl.when(i < nu_ref[0])
    def _():
        changed = (i == 0) | (te_ref[i] != te_ref[jnp.maximum(i - 1, 0)])

        @pl.when(changed)
        def _():
            w1b[...] = w1_ref[0].astype(BF16)
            w2b[...] = w2_ref[0].astype(BF16)

        hh = jnp.dot(xs_ref[...], w1b[...], preferred_element_type=F32) + b1_ref[0]
        glu = jnp.minimum(hh[:, 0:D], SWIGLU_LIMIT)
        lin = jnp.clip(hh[:, D:2 * D], -SWIGLU_LIMIT, SWIGLU_LIMIT)
        act = glu * _sigmoid(SWIGLU_ALPHA * glu) * (lin + 1.0)
        ys_ref[...] = jnp.dot(act.astype(BF16), w2b[...], preferred_element_type=F32) + b2_ref[0]


def _experts(tile_expert, n_used, xs, w1, b1, w2, b2):
    n_rows = xs.shape[0]
    n_tiles = n_rows // MOE_TM
    grid_spec = pltpu.PrefetchScalarGridSpec(
        num_scalar_prefetch=2,
        grid=(n_tiles,),
        in_specs=[pl.BlockSpec((MOE_TM, D), lambda i, te, nu: (i, 0)),
                  pl.BlockSpec((1, D, 2 * D), lambda i, te, nu: (te[i], 0, 0)),
                  pl.BlockSpec((1, 1, 2 * D), lambda i, te, nu: (te[i], 0, 0)),
                  pl.BlockSpec((1, D, D), lambda i, te, nu: (te[i], 0, 0)),
                  pl.BlockSpec((1, 1, D), lambda i, te, nu: (te[i], 0, 0))],
        out_specs=pl.BlockSpec((MOE_TM, D), lambda i, te, nu: (i, 0)),
        scratch_shapes=[pltpu.VMEM((D, 2 * D), BF16), pltpu.VMEM((D, D), BF16)],
    )
    return pl.pallas_call(
        _expert_kernel,
        grid_spec=grid_spec,
        out_shape=jax.ShapeDtypeStruct((n_rows, D), F32),
        compiler_params=_cparams(("arbitrary",)),
        name="experts",
    )(tile_expert, n_used, xs, w1, b1, w2, b2)


def _final_kernel(yg_ref, topw_ref, x1_ref, mod_ref, postg_ref, o_ref, *, tiles_per_batch):
    i = pl.program_id(0)
    row = i // tiles_per_batch
    tw = topw_ref[...]
    f = yg_ref[:, 0:D] * tw[:, 0:1]
    for k in range(1, TOP_K):
        f = f + yg_ref[:, k * D:(k + 1) * D] * tw[:, k:k + 1]
    g2 = mod_ref[pl.ds(row, 1), 5 * D:6 * D]
    o_ref[...] = x1_ref[...] + g2 * _rms(f, postg_ref[...])


def _final(yg, topw, x1, mod, postg, *, tm, s_lat):
    n = x1.shape[0]
    kern = functools.partial(_final_kernel, tiles_per_batch=s_lat // tm)
    return pl.pallas_call(
        kern,
        grid=(n // tm,),
        in_specs=[pl.BlockSpec((tm, TOP_K * D), lambda i: (i, 0)),
                  pl.BlockSpec((tm, LANES), lambda i: (i, 0)),
                  pl.BlockSpec((tm, D), lambda i: (i, 0)),
                  pl.BlockSpec((8, 6 * D), lambda i: (0, 0)),
                  pl.BlockSpec((1, D), lambda i: (0, 0))],
        out_specs=pl.BlockSpec((tm, D), lambda i: (i, 0)),
        out_shape=jax.ShapeDtypeStruct((n, D), F32),
        compiler_params=_cparams(("arbitrary",)),
        name="final",
    )(yg, topw, x1, mod, postg)


def _block_diag(w):
    w = w.reshape(2, 4, 4, RG_BW, RG_BW)
    eye = jnp.eye(4, dtype=w.dtype)
    return jnp.einsum('dgiab,ij->dgiajb', w, eye).reshape(2, 4, 4 * RG_BW, 4 * RG_BW)


def kernel(x, c, ctx, c_ctx, ada_w, ada_b, mix_pre_g, mix_post_g, w_in, rg_conv_w, rg_conv_b, rg_wa, rg_ba,
           rg_wi, rg_bi, rg_lam, rg_w_o, dn_conv_w, dn_a_log, dn_dt_bias, dn_norm_g, dn_w_o, w_out,
           ffn_pre_g, ffn_post_g, router_w, router_b, e_w1, e_b1, e_w2, e_b2):
    b, s_lat, _ = x.shape
    t_ctx = ctx.shape[1]
    depth = ada_w.shape[0]
    assert depth == 1 and b * D == SCAN_ROWS * SCAN_W
    n_lat = b * s_lat
    l = 0

    c8 = jnp.zeros((8, D), F32).at[0:b].set(c).at[b].set(c_ctx)
    mod = _ada_mod(c8, ada_w[l], ada_b[l].reshape(1, -1))

    wi = w_in[l]
    w_main = jnp.concatenate([wi[:, 0:6 * D], wi[:, 6 * D + 32:8 * D + 32]], axis=1)
    w_main = w_main.reshape(D, 8, D).transpose(1, 0, 2).astype(BF16)
    w_ab = jnp.pad(wi[:, 6 * D:6 * D + 32], ((0, 0), (0, LANES - 32))).astype(BF16)
    pre_g = mix_pre_g[l].reshape(1, D)
    x2d = x.reshape(n_lat, D)
    o_lat, ab_lat = _inproj(x2d, pre_g, mod, w_main, w_ab, tm=1024, tiles_per_batch=s_lat // 1024, row_offset=0)
    o_ctx, ab_ctx = _inproj(ctx.reshape(b * t_ctx, D), pre_g, mod, w_main, w_ab, tm=b * t_ctx,
                            tiles_per_batch=1, row_offset=b)

    wblk = jnp.stack([_block_diag(rg_wa[l]), _block_diag(rg_wi[l])], axis=1).astype(BF16)
    rg_args = (rg_conv_w[l], rg_conv_b[l], wblk, rg_ba[l], rg_bi[l], rg_lam[l])
    coef_c = _rg_coef(o_ctx[0].reshape(b, t_ctx, D), *rg_args, tt=t_ctx, reset_first=True)
    coef_l = _rg_coef(o_lat[0].reshape(b, s_lat, D), *rg_args, tt=256, reset_first=False)
    hf, hb = _rg_scan(coef_c, coef_l, t_ctx=t_ctx, t_lat=s_lat, tt=256)
    hf = hf.reshape(s_lat, b * D)
    hb = hb.reshape(s_lat, b * D)

    alog_l = jnp.zeros((1, LANES), F32).at[0, 0:2 * HEADS].set(dn_a_log[l].reshape(-1))
    dtb_l = jnp.zeros((1, LANES), F32).at[0, 0:2 * HEADS].set(dn_dt_bias[l].reshape(-1))
    of, ob = _delta(o_lat, ab_lat, o_ctx, ab_ctx, dn_conv_w[l], alog_l, dtb_l, b=b, s_lat=s_lat, t_ctx=t_ctx)

    rw = jnp.pad(router_w[l], ((0, 0), (0, LANES - N_EXPERTS)))
    rb = jnp.pad(router_b[l].reshape(1, -1), ((0, 0), (0, LANES - N_EXPERTS)))
    x1, hf2, topi, topw, rank, cnt = _post(
        x2d, o_lat, hf, hb, of, ob, mod,
        rg_w_o[l].astype(BF16), dn_w_o[l].astype(BF16), w_out[l].astype(BF16),
        jnp.tile(dn_norm_g[l], HEADS).reshape(1, D), mix_post_g[l].reshape(1, D), ffn_pre_g[l].reshape(1, D),
        rw, rb, tm=512, s_lat=s_lat)

    counts = cnt[0, 0:N_EXPERTS]
    padded = ((counts + MOE_TM - 1) // MOE_TM) * MOE_TM
    ends = jnp.cumsum(padded)
    starts = ends - padded
    top_i = topi[:, 0:TOP_K]
    pos = (starts[top_i] + rank[:, 0:TOP_K]).reshape(-1)
    n_tiles = n_lat * TOP_K // MOE_TM + N_EXPERTS
    tile_start = jnp.arange(n_tiles, dtype=jnp.int32) * MOE_TM
    tile_expert = jnp.minimum(jnp.sum(tile_start[:, None] >= ends[None, :], axis=1), N_EXPERTS - 1).astype(jnp.int32)
    n_used = (ends[-1] // MOE_TM).astype(jnp.int32).reshape(1)

    token_of = jnp.zeros((n_tiles * MOE_TM,), jnp.int32).at[pos].set(
        jnp.arange(n_lat * TOP_K, dtype=jnp.int32) // TOP_K)
    xs = jnp.take(hf2.astype(BF16), token_of, axis=0)
    ys = _experts(tile_expert, n_used, xs, e_w1[l], e_b1[l].reshape(N_EXPERTS, 1, -1),
                  e_w2[l], e_b2[l].reshape(N_EXPERTS, 1, -1))
    yg = jnp.take(ys, pos, axis=0).reshape(n_lat, TOP_K * D)

    out = _final(yg, topw, x1, mod, ffn_post_g[l].reshape(1, D), tm=512, s_lat=s_lat)
    return out.reshape(b, s_lat, D)
```

```python
import functools

import jax
import jax.numpy as jnp
from jax import lax
from jax.experimental import pallas as pl
from jax.experimental.pallas import tpu as pltpu

F32 = jnp.float32
BF16 = jnp.bfloat16
U32 = jnp.uint32
HIGHEST = lax.Precision.HIGHEST

D = 1024
EPS = 1e-6
GRID_W = 64
CHUNK = 64
HEADS = 8
DK = 128
RG_C = 8.0
RG_BW = 64
N_EXPERTS = 32
TOP_K = 4
SWIGLU_LIMIT = 7.0
SWIGLU_ALPHA = 1.702
LANES = 128
MOE_TM = 256
VMEM_LIMIT = 56 * 1024 * 1024


def _cparams(sem):
    return pltpu.CompilerParams(dimension_semantics=sem, vmem_limit_bytes=VMEM_LIMIT)


def _sigmoid(x):
    return 1.0 / (1.0 + jnp.exp(-x))


def _softplus(y):
    return jnp.maximum(y, 0.0) + jnp.log1p(jnp.exp(-jnp.abs(y)))


def _nt_dot(a, b):
    return lax.dot_general(a, b, (((1,), (1,)), ((), ())), preferred_element_type=F32)


def _ada_kernel(c_ref, w_ref, b_ref, o_ref):
    c = c_ref[...]
    a = c * _sigmoid(c)
    o_ref[...] = jnp.dot(a, w_ref[...], preferred_element_type=F32, precision=HIGHEST) + b_ref[...]


def _ada_mod(c8, ada_w, ada_b):
    n = ada_w.shape[1]
    tn = 1024
    return pl.pallas_call(
        _ada_kernel,
        grid=(n // tn,),
        in_specs=[pl.BlockSpec((8, D), lambda j: (0, 0)),
                  pl.BlockSpec((D, tn), lambda j: (0, j)),
                  pl.BlockSpec((1, tn), lambda j: (0, j))],
        out_specs=pl.BlockSpec((8, tn), lambda j: (0, j)),
        out_shape=jax.ShapeDtypeStruct((8, n), F32),
        compiler_params=_cparams(("arbitrary",)),
        name="ada_mod",
    )(c8, ada_w, ada_b)


def _inproj_kernel(x_ref, g_ref, mod_ref, w_ref, wab_ref, o_ref, ab_ref, h_scr, *, tiles_per_batch, row_offset):
    i = pl.program_id(0)
    j = pl.program_id(1)

    @pl.when(j == 0)
    def _():
        x = x_ref[...]
        y = x * lax.rsqrt(jnp.mean(x * x, axis=-1, keepdims=True) + EPS) * g_ref[...]
        row = row_offset + i // tiles_per_batch
        sh = mod_ref[pl.ds(row, 1), 0:D]
        sc = mod_ref[pl.ds(row, 1), D:2 * D]
        h = (y * (1.0 + sc) + sh).astype(BF16)
        h_scr[...] = h
        ab_ref[...] = jnp.dot(h, wab_ref[...], preferred_element_type=F32)

    o_ref[0] = jnp.dot(h_scr[...], w_ref[0], preferred_element_type=F32).astype(BF16)


def _inproj(x2d, g, mod, w_main, w_ab, *, tm, tiles_per_batch, row_offset):
    n = x2d.shape[0]
    nj = w_main.shape[0]
    kern = functools.partial(_inproj_kernel, tiles_per_batch=tiles_per_batch, row_offset=row_offset)
    return pl.pallas_call(
        kern,
        grid=(n // tm, nj),
        in_specs=[pl.BlockSpec((tm, D), lambda i, j: (i, 0)),
                  pl.BlockSpec((1, D), lambda i, j: (0, 0)),
                  pl.BlockSpec((8, 6 * D), lambda i, j: (0, 0)),
                  pl.BlockSpec((1, D, D), lambda i, j: (j, 0, 0)),
                  pl.BlockSpec((D, LANES), lambda i, j: (0, 0))],
        out_specs=[pl.BlockSpec((1, tm, D), lambda i, j: (j, i, 0)),
                   pl.BlockSpec((tm, LANES), lambda i, j: (i, 0))],
        out_shape=[jax.ShapeDtypeStruct((nj, n, D), BF16),
                   jax.ShapeDtypeStruct((n, LANES), F32)],
        scratch_shapes=[pltpu.VMEM((tm, D), BF16)],
        compiler_params=_cparams(("arbitrary", "arbitrary")),
        name="inproj",
    )(x2d, g, mod, w_main, w_ab)


RG_HALO = 16
SCAN_ROWS = 8
SCAN_SLABS = 4


def _pack_bf16_pair(hi, lo):
    hi_bits = lax.bitcast_convert_type(hi.astype(BF16).astype(F32), U32)
    lo_bits = lax.bitcast_convert_type(lo.astype(BF16).astype(F32), U32)
    return hi_bits | (lo_bits >> 16)


def _rg_coef_kernel(cur_ref, prev_ref, next_ref, cw_ref, cb_ref, wblk_ref, ba_ref, bi_ref, lam_ref,
                    wf_ref, wb_ref, *, tt, nt, reset_first):
    i = pl.program_id(0)
    b = pl.program_id(1)
    x = cur_ref[0].astype(F32)
    xp = jnp.where(i > 0, prev_ref[0].astype(F32), 0.0)
    xn = jnp.where(i < nt - 1, next_ref[0].astype(F32), 0.0)
    xe_f = jnp.concatenate([xp, x], axis=0)
    xe_b = jnp.concatenate([x, xn], axis=0)
    n_ext = tt + RG_HALO
    row = lax.broadcasted_iota(jnp.int32, (tt, 1), 0)
    outs = (wf_ref, wb_ref)
    for d in (0, 1):
        w = cw_ref[d]
        acc = x * w[3:4]
        for sft in (1, 2, 3):
            if d == 0:
                sh = pltpu.roll(xe_f, sft, axis=0)[RG_HALO:RG_HALO + tt]
            else:
                sh = pltpu.roll(xe_b, n_ext - sft, axis=0)[0:tt]
            acc = acc + sh * w[3 - sft:4 - sft]
        xc = acc + cb_ref[d:d + 1]
        xcb = xc.astype(BF16)
        pre_a = jnp.concatenate(
            [jnp.dot(xcb[:, g * 256:(g + 1) * 256], wblk_ref[d, 0, g], preferred_element_type=F32)
             for g in range(4)], axis=1) + ba_ref[d:d + 1]
        pre_i = jnp.concatenate(
            [jnp.dot(xcb[:, g * 256:(g + 1) * 256], wblk_ref[d, 1, g], preferred_element_type=F32)
             for g in range(4)], axis=1) + bi_ref[d:d + 1]
        r = _sigmoid(pre_a)
        ig = _sigmoid(pre_i)
        la = r * (-RG_C * _softplus(-lam_ref[d:d + 1]))
        a = jnp.exp(la)
        mult = jnp.sqrt(1.0 - a * a)
        if reset_first:
            if d == 0:
                first = (row == 0) & (i == 0)
            else:
                first = (row == tt - 1) & (i == nt - 1)
            mult = jnp.where(first, 1.0, mult)
        word = _pack_bf16_pair(la, mult * ig * xc)
        for half in (0, 1):
            for q in range(SCAN_SLABS):
                c0 = (half * SCAN_SLABS + q) * LANES
                outs[d][q, pl.ds(b * 2 + half, tt, stride=SCAN_ROWS), :] = word[:, c0:c0 + LANES]


def _rg_coef(xr, conv_w, conv_b, wblk, ba, bi, lam, *, tt, reset_first):
    b, t, _ = xr.shape
    nt = t // tt
    hb = tt // RG_HALO
    kern = functools.partial(_rg_coef_kernel, tt=tt, nt=nt, reset_first=reset_first)
    full = lambda shape: pl.BlockSpec(shape, lambda i, bb: (0,) * len(shape))
    out_sds = jax.ShapeDtypeStruct((SCAN_SLABS, t * SCAN_ROWS, LANES), U32)
    out_spec = pl.BlockSpec((SCAN_SLABS, tt * SCAN_ROWS, LANES), lambda i, bb: (0, i, 0))
    return pl.pallas_call(
        kern,
        grid=(nt, b),
        in_specs=[pl.BlockSpec((1, tt, D), lambda i, bb: (bb, i, 0)),
                  pl.BlockSpec((1, RG_HALO, D), lambda i, bb: (bb, jnp.maximum(i * hb - 1, 0), 0)),
                  pl.BlockSpec((1, RG_HALO, D), lambda i, bb: (bb, jnp.minimum((i + 1) * hb, t // RG_HALO - 1), 0)),
                  full((2, 4, D)), full((2, D)), full((2, 2, 4, 256, 256)),
                  full((2, D)), full((2, D)), full((2, D))],
        out_specs=[out_spec, out_spec],
        out_shape=[out_sds, out_sds],
        compiler_params=_cparams(("arbitrary", "arbitrary")),
        name="rg_coef",
    )(xr, xr, xr, conv_w, conv_b, wblk, ba, bi, lam)


def _scan_step(word, h):
    la = lax.bitcast_convert_type(word & jnp.uint32(0xFFFF0000), F32)
    bb = lax.bitcast_convert_type(word << 16, F32)
    return jnp.exp(la) * h + bb


def _rg_scan_run(wf, wb, hf, hb, carry, n):
    def body(t, c):
        h_f, h_b = c
        r0 = pl.multiple_of(t * SCAN_ROWS, SCAN_ROWS)
        r1 = pl.multiple_of((n - 1 - t) * SCAN_ROWS, SCAN_ROWS)
        new_f, new_b = [], []
        for q in range(SCAN_SLABS):
            f = _scan_step(wf[q, pl.ds(r0, SCAN_ROWS), :], h_f[q])
            g = _scan_step(wb[q, pl.ds(r1, SCAN_ROWS), :], h_b[q])
            if hf is not None:
                hf[q, pl.ds(r0, SCAN_ROWS), :] = f
                hb[q, pl.ds(r1, SCAN_ROWS), :] = g
            new_f.append(f)
            new_b.append(g)
        return tuple(new_f), tuple(new_b)

    init = (tuple(carry[0, q] for q in range(SCAN_SLABS)), tuple(carry[1, q] for q in range(SCAN_SLABS)))
    h_f, h_b = lax.fori_loop(0, n, body, init, unroll=8)
    for q in range(SCAN_SLABS):
        carry[0, q] = h_f[q]
        carry[1, q] = h_b[q]


def _rg_scan_kernel(wf_c, wb_c, wf_l, wb_l, hf_ref, hb_ref, carry, *, t_ctx, tt):
    s = pl.program_id(0)

    @pl.when(s == 0)
    def _():
        carry[...] = jnp.zeros(carry.shape, F32)
        _rg_scan_run(wf_c, wb_c, None, None, carry, t_ctx)

    @pl.when(s > 0)
    def _():
        _rg_scan_run(wf_l, wb_l, hf_ref, hb_ref, carry, tt)


def _rg_scan(coef_c, coef_l, *, t_ctx, t_lat, tt):
    nt = t_lat // tt
    ctx_spec = pl.BlockSpec((SCAN_SLABS, t_ctx * SCAN_ROWS, LANES), lambda s: (0, 0, 0))
    f_spec = pl.BlockSpec((SCAN_SLABS, tt * SCAN_ROWS, LANES), lambda s: (0, jnp.maximum(s - 1, 0), 0))
    b_spec = pl.BlockSpec((SCAN_SLABS, tt * SCAN_ROWS, LANES), lambda s: (0, nt - jnp.maximum(s, 1), 0))
    out_sds = jax.ShapeDtypeStruct((SCAN_SLABS, t_lat * SCAN_ROWS, LANES), F32)
    kern = functools.partial(_rg_scan_kernel, t_ctx=t_ctx, tt=tt)
    return pl.pallas_call(
        kern,
        grid=(nt + 1,),
        in_specs=[ctx_spec, ctx_spec, f_spec, b_spec],
        out_specs=[f_spec, b_spec],
        out_shape=[out_sds, out_sds],
        scratch_shapes=[pltpu.VMEM((2, SCAN_SLABS, SCAN_ROWS, LANES), F32)],
        compiler_params=_cparams(("arbitrary",)),
        name="rg_scan",
    )(coef_c[0], coef_c[1], coef_l[0], coef_l[1])


N_CTX_CHUNKS = 4
NEUMANN_STEPS = 6


def _delta_prep(d, is_ctx, refs_l, refs_c, prevraw, cw_ref, alog_ref, dtb_ref):
    q_l, k_l, v_l, ab_l = refs_l
    q_c, k_c, v_c, ab_c = refs_c

    def sel(rc, rl):
        return jnp.where(is_ctx, rc[0], rl[0])

    raw = jnp.concatenate([sel(q_c, q_l), sel(k_c, k_l), sel(v_c, v_l)], axis=1).astype(F32)
    abv = jnp.where(is_ctx, ab_c[...], ab_l[...])
    prev = prevraw[d]
    w = cw_ref[d]
    acc = raw * w[3:4]
    if d == 0:
        xe = jnp.concatenate([prev, raw], axis=0)
        for sft in (1, 2, 3):
            acc = acc + pltpu.roll(xe, sft, axis=0)[8:8 + CHUNK] * w[3 - sft:4 - sft]
        prevraw[d] = raw[CHUNK - 8:CHUNK]
    else:
        xe = jnp.concatenate([raw, prev], axis=0)
        for sft in (1, 2, 3):
            acc = acc + pltpu.roll(xe, CHUNK + 8 - sft, axis=0)[0:CHUNK] * w[3 - sft:4 - sft]
        prevraw[d] = raw[0:8]
    y = acc * _sigmoid(acc)

    rp = lax.broadcasted_iota(jnp.int32, (2 * CHUNK, CHUNK), 0)
    cp = lax.broadcasted_iota(jnp.int32, (2 * CHUNK, CHUNK), 1)
    if d == 0:
        m_pad = ((rp >= cp) & (rp < CHUNK)).astype(F32)
    else:
        m_pad = ((rp <= cp) & (rp < CHUNK)).astype(F32)
    g_all = -jnp.exp(alog_ref[...]) * _softplus(abv + dtb_ref[...])
    beta_all = _sigmoid(abv)
    gcum = jnp.dot(m_pad, g_all, preferred_element_type=F32, precision=HIGHEST)
    return y, gcum, gcum.T, beta_all


def _delta_heads(d, y, gcum, gcum_t, beta_all, out_ref, state):
    ri = lax.broadcasted_iota(jnp.int32, (CHUNK, CHUNK), 0)
    ci = lax.broadcasted_iota(jnp.int32, (CHUNK, CHUNK), 1)
    if d == 0:
        incl, strict, last = ri >= ci, ri > ci, CHUNK - 1
    else:
        incl, strict, last = ri <= ci, ri < ci, 0
    eye = (ri == ci).astype(F32)
    keep = lax.broadcasted_iota(jnp.int32, (CHUNK, 2 * CHUNK), 1) >= CHUNK
    zeros_half = jnp.zeros((CHUNK, DK), F32)
    heads = range(HEADS)

    qmat, rhs, qg, qkd, kd, glast = [], [], [], [], [], []
    for h in heads:
        la = d * HEADS + h
        lb = 2 * HEADS + d * HEADS + h
        gcol = gcum[0:CHUNK, la:la + 1]
        grow = gcum_t[la:la + 1, 0:CHUNK]
        bcol = beta_all[:, lb:lb + 1]
        qh = y[:, h * DK:(h + 1) * DK]
        kh = y[:, D + h * DK:D + (h + 1) * DK]
        vh = y[:, 2 * D + h * DK:2 * D + (h + 1) * DK]
        qn = qh * lax.rsqrt(jnp.sum(qh * qh, axis=-1, keepdims=True) + EPS) * (DK ** -0.5)
        kn = kh * lax.rsqrt(jnp.sum(kh * kh, axis=-1, keepdims=True) + EPS)
        kbeta = kn * bcol
        knb = kn.astype(BF16)
        a_mat = _nt_dot(kbeta.astype(BF16), knb)
        qk = _nt_dot(qn.astype(BF16), knb)
        dec = jnp.where(incl, jnp.exp(jnp.where(incl, gcol - grow, 0.0)), 0.0)
        s0 = jnp.where(strict, -a_mat * dec, 0.0)
        qmat.append(jnp.concatenate([s0, eye], axis=1))
        eg = jnp.exp(gcol)
        rhs.append(jnp.concatenate([vh * bcol, kbeta * eg], axis=1).astype(BF16))
        qg.append(qn * eg)
        qkd.append((qk * dec).astype(BF16))
        gl = gcum[last:last + 1, la:la + 1]
        glast.append(gl)
        kd.append(kn * jnp.exp(gl - gcol))

    for _ in range(NEUMANN_STEPS):
        for h in heads:
            qm = qmat[h]
            qmat[h] = jnp.dot(qm[:, 0:CHUNK].astype(BF16), qm.astype(BF16),
                              preferred_element_type=F32) + jnp.where(keep, qm, 0.0)

    sol = [jnp.dot(qmat[h][:, CHUNK:2 * CHUNK].astype(BF16), rhs[h], preferred_element_type=F32) for h in heads]
    hst = [state[d, h] for h in heads]
    wqh = [jnp.dot(jnp.concatenate([sol[h][:, DK:2 * DK], qg[h]], axis=0).astype(BF16), hst[h].astype(BF16),
                   preferred_element_type=F32) for h in heads]
    v_new = [sol[h][:, 0:DK] - wqh[h][0:CHUNK] for h in heads]
    for h in heads:
        o = wqh[h][CHUNK:2 * CHUNK] + jnp.dot(qkd[h], v_new[h].astype(BF16), preferred_element_type=F32)
        out_ref[:, h * DK:(h + 1) * DK] = o.astype(BF16)
    for h in heads:
        kd_t = jnp.concatenate([kd[h], zeros_half], axis=0).T
        v_pad = jnp.concatenate([v_new[h], zeros_half], axis=0)
        state[d, h] = hst[h] * jnp.exp(glast[h]) + jnp.dot(kd_t.astype(BF16), v_pad.astype(BF16),
                                                           preferred_element_type=F32)


def _delta_kernel(qf_l, kf_l, vf_l, abf_l, qb_l, kb_l, vb_l, abb_l,
                  qf_c, kf_c, vf_c, abf_c, qb_c, kb_c, vb_c, abb_c,
                  cw_ref, alog_ref, dtb_ref, of_ref, ob_ref, state, prevraw):
    s = pl.program_id(1)
    is_ctx = s < N_CTX_CHUNKS

    @pl.when(s == 0)
    def _():
        state[...] = jnp.zeros(state.shape, F32)

    @pl.when((s == 0) | (s == N_CTX_CHUNKS))
    def _():
        prevraw[...] = jnp.zeros(prevraw.shape, F32)

    pf = _delta_prep(0, is_ctx, (qf_l, kf_l, vf_l, abf_l), (qf_c, kf_c, vf_c, abf_c), prevraw,
                     cw_ref, alog_ref, dtb_ref)
    pb = _delta_prep(1, is_ctx, (qb_l, kb_l, vb_l, abb_l), (qb_c, kb_c, vb_c, abb_c), prevraw,
                     cw_ref, alog_ref, dtb_ref)
    _delta_heads(0, *pf, of_ref, state)
    _delta_heads(1, *pb, ob_ref, state)


def _delta(qkv_col, ab_col, o_ctx, ab_ctx, conv_w, alog_l, dtb_l, *, b, s_lat, t_ctx):
    ncol = s_lat // CHUNK
    assert s_lat // GRID_W == CHUNK and t_ctx == N_CTX_CHUNKS * CHUNK

    col_f = lambda bb, s: bb * ncol + jnp.maximum(s - N_CTX_CHUNKS, 0)
    col_b = lambda bb, s: bb * ncol + ncol - 1 - jnp.maximum(s - N_CTX_CHUNKS, 0)
    cch_f = lambda bb, s: bb * N_CTX_CHUNKS + jnp.minimum(s, N_CTX_CHUNKS - 1)
    cch_b = lambda bb, s: bb * N_CTX_CHUNKS + N_CTX_CHUNKS - 1 - jnp.minimum(s, N_CTX_CHUNKS - 1)

    def tok_spec(j, fn):
        return pl.BlockSpec((1, CHUNK, D), lambda bb, s: (j, fn(bb, s), 0))

    ab_spec = lambda fn: pl.BlockSpec((CHUNK, LANES), lambda bb, s: (fn(bb, s), 0))
    full = lambda shape: pl.BlockSpec(shape, lambda bb, s: (0,) * len(shape))

    in_specs = ([tok_spec(0, col_f), tok_spec(1, col_f), tok_spec(2, col_f), ab_spec(col_f),
                 tok_spec(0, col_b), tok_spec(1, col_b), tok_spec(2, col_b), ab_spec(col_b),
                 tok_spec(2, cch_f), tok_spec(3, cch_f), tok_spec(4, cch_f), ab_spec(cch_f),
                 tok_spec(2, cch_b), tok_spec(3, cch_b), tok_spec(4, cch_b), ab_spec(cch_b),
                 full((2, 4, 3 * D)), full((1, LANES)), full((1, LANES))])
    out_sds = jax.ShapeDtypeStruct((b * s_lat, D), BF16)
    out_specs = [pl.BlockSpec((CHUNK, D), lambda bb, s: (col_f(bb, s), 0)),
                 pl.BlockSpec((CHUNK, D), lambda bb, s: (col_b(bb, s), 0))]
    return pl.pallas_call(
        _delta_kernel,
        grid=(b, N_CTX_CHUNKS + ncol),
        in_specs=in_specs,
        out_specs=out_specs,
        out_shape=[out_sds, out_sds],
        scratch_shapes=[pltpu.VMEM((2, HEADS, DK, DK), F32), pltpu.VMEM((2, 8, 3 * D), F32)],
        compiler_params=_cparams(("arbitrary", "arbitrary")),
        name="delta",
    )(qkv_col, qkv_col, qkv_col, ab_col, qkv_col, qkv_col, qkv_col, ab_col,
      o_ctx, o_ctx, o_ctx, ab_ctx, o_ctx, o_ctx, o_ctx, ab_ctx,
      conv_w, alog_l, dtb_l)


def _rms(x, g):
    return x * lax.rsqrt(jnp.mean(x * x, axis=-1, keepdims=True) + EPS) * g


def _scan_rows(ref, b, tm):
    return jnp.concatenate(
        [ref[q, pl.ds(b * 2 + half, tm, stride=SCAN_ROWS), :] for half in (0, 1) for q in range(SCAN_SLABS)],
        axis=1)


def _post_kernel(x_ref, gate_ref, z_ref, grg_ref, gdn_ref, hf_ref, hb_ref, of_ref, ob_ref, mod_ref,
                 wrg_ref, wdn_ref, wout_ref, dng_ref, postg_ref, preg_ref, rw_ref, rb_ref,
                 x1_ref, hf2_ref, topi_ref, topw_ref, rank_ref, cnt_ref, carry, *, tm):
    b = pl.program_id(1)

    @pl.when((pl.program_id(0) == 0) & (b == 0))
    def _():
        carry[...] = jnp.zeros(carry.shape, F32)

    gate = gate_ref[0].astype(F32)
    gelu = 0.5 * gate * (1.0 + jnp.tanh(0.7978845608028654 * (gate + 0.044715 * gate * gate * gate)))
    rg_h = _scan_rows(hf_ref, b, tm) + _scan_rows(hb_ref, b, tm)
    y_rg = jnp.dot((rg_h * gelu).astype(BF16), wrg_ref[...], preferred_element_type=F32)

    dn = of_ref[...].astype(F32) + ob_ref[...].astype(F32)
    z = z_ref[0].astype(F32)
    parts = []
    for h in range(HEADS):
        seg = dn[:, h * DK:(h + 1) * DK]
        parts.append(seg * lax.rsqrt(jnp.mean(seg * seg, axis=-1, keepdims=True) + EPS))
    o = jnp.concatenate(parts, axis=1) * dng_ref[...] * (z * _sigmoid(z))
    y_dn = jnp.dot(o.astype(BF16), wdn_ref[...], preferred_element_type=F32)

    merged = _sigmoid(grg_ref[0].astype(F32)) * y_rg + _sigmoid(gdn_ref[0].astype(F32)) * y_dn
    y = jnp.dot(merged.astype(BF16), wout_ref[...], preferred_element_type=F32)

    g1 = mod_ref[pl.ds(b, 1), 2 * D:3 * D]
    sh2 = mod_ref[pl.ds(b, 1), 3 * D:4 * D]
    sc2 = mod_ref[pl.ds(b, 1), 4 * D:5 * D]
    x1 = x_ref[...] + g1 * _rms(y, postg_ref[...])
    x1_ref[...] = x1
    hf2 = _rms(x1, preg_ref[...]) * (1.0 + sc2) + sh2
    hf2_ref[...] = hf2

    lane = lax.broadcasted_iota(jnp.int32, (tm, LANES), 1)
    lane_f = lane.astype(F32)
    logits = jnp.dot(hf2, rw_ref[...], preferred_element_type=F32, precision=HIGHEST) + rb_ref[...]
    neg = jnp.float32(-jnp.inf)
    lg = jnp.where(lane < N_EXPERTS, logits, neg)
    vals, sels = [], []
    topi = jnp.zeros((tm, LANES), F32)
    onehot = jnp.zeros((tm, LANES), F32)
    for k in range(TOP_K):
        m = jnp.max(lg, axis=-1, keepdims=True)
        idx = jnp.min(jnp.where(lg == m, lane_f, float(LANES)), axis=-1, keepdims=True)
        sel = lane_f == idx
        vals.append(m)
        sels.append(sel)
        lg = jnp.where(sel, neg, lg)
        onehot = onehot + sel.astype(F32)
        topi = topi + jnp.where(lane == k, idx, 0.0)
    es = [jnp.exp(v - vals[0]) for v in vals]
    den = es[0] + es[1] + es[2] + es[3]
    topw = jnp.zeros((tm, LANES), F32)
    for k in range(TOP_K):
        topw = topw + jnp.where(lane == k, es[k] / den, 0.0)

    ri = lax.broadcasted_iota(jnp.int32, (tm, tm), 0)
    ci = lax.broadcasted_iota(jnp.int32, (tm, tm), 1)
    tri = (ri > ci).astype(BF16)
    cum = jnp.dot(tri, onehot.astype(BF16), preferred_element_type=F32) + carry[...]
    rank = jnp.zeros((tm, LANES), F32)
    for k in range(TOP_K):
        rk = jnp.sum(jnp.where(sels[k], cum, 0.0), axis=-1, keepdims=True)
        rank = rank + jnp.where(lane == k, rk, 0.0)
    new_carry = carry[...] + jnp.sum(onehot, axis=0, keepdims=True)
    carry[...] = new_carry
    topi_ref[...] = topi.astype(jnp.int32)
    topw_ref[...] = topw
    rank_ref[...] = rank.astype(jnp.int32)
    cnt_ref[...] = jnp.broadcast_to(new_carry, (8, LANES)).astype(jnp.int32)


def _post(x2d, o_lat, hf, hb, of, ob, mod, wrg, wdn, wout, dng, postg, preg, rw, rb, *, tm, s_lat):
    n = x2d.shape[0]
    tpb = s_lat // tm
    nb = n // s_lat
    kern = functools.partial(_post_kernel, tm=tm)
    tok = pl.BlockSpec((tm, D), lambda t, b: (b * tpb + t, 0))
    oj = lambda j: pl.BlockSpec((1, tm, D), lambda t, b: (j, b * tpb + t, 0))
    hspec = pl.BlockSpec((SCAN_SLABS, tm * SCAN_ROWS, LANES), lambda t, b: (0, t, 0))
    full = lambda shape: pl.BlockSpec(shape, lambda t, b: (0,) * len(shape))
    lane_out = pl.BlockSpec((tm, LANES), lambda t, b: (b * tpb + t, 0))
    return pl.pallas_call(
        kern,
        grid=(tpb, nb),
        in_specs=[tok, oj(1), oj(2), oj(3), oj(4), hspec, hspec, tok, tok, full((8, 6 * D)),
                  full((D, D)), full((D, D)), full((D, D)), full((1, D)), full((1, D)), full((1, D)),
                  full((D, LANES)), full((1, LANES))],
        out_specs=[tok, tok, lane_out, lane_out, lane_out, pl.BlockSpec((8, LANES), lambda t, b: (0, 0))],
        out_shape=[jax.ShapeDtypeStruct((n, D), F32), jax.ShapeDtypeStruct((n, D), F32),
                   jax.ShapeDtypeStruct((n, LANES), jnp.int32), jax.ShapeDtypeStruct((n, LANES), F32),
                   jax.ShapeDtypeStruct((n, LANES), jnp.int32), jax.ShapeDtypeStruct((8, LANES), jnp.int32)],
        scratch_shapes=[pltpu.VMEM((1, LANES), F32)],
        compiler_params=_cparams(("arbitrary", "arbitrary")),
        name="post_mix",
    )(x2d, o_lat, o_lat, o_lat, o_lat, hf, hb, of, ob, mod, wrg, wdn, wout, dng, postg, preg, rw, rb)


def _expert_kernel(te_ref, nu_ref, xs_ref, w1_ref, b1_ref, w2_ref, b2_ref, ys_ref, w1b, w2b):
    i = pl.program_id(0)

    @pl.when(i < nu_ref[0])
    def _():
        changed = (i == 0) | (te_ref[i] != te_ref[jnp.maximum(i - 1, 0)])

        @pl.when(changed)
        def _():
            w1b[...] = w1_ref[0].astype(BF16)
            w2b[...] = w2_ref[0].astype(BF16)

        hh = jnp.dot(xs_ref[...], w1b[...], preferred_element_type=F32) + b1_ref[0]
        glu = jnp.minimum(hh[:, 0:D], SWIGLU_LIMIT)
        lin = jnp.clip(hh[:, D:2 * D], -SWIGLU_LIMIT, SWIGLU_LIMIT)
        act = glu * _sigmoid(SWIGLU_ALPHA * glu) * (lin + 1.0)
        ys_ref[...] = jnp.dot(act.astype(BF16), w2b[...], preferred_element_type=F32) + b2_ref[0]


def _experts(tile_expert, n_used, xs, w1, b1, w2, b2):
    n_rows = xs.shape[0]
    n_tiles = n_rows // MOE_TM
    grid_spec = pltpu.PrefetchScalarGridSpec(
        num_scalar_prefetch=2,
        grid=(n_tiles,),
        in_specs=[pl.BlockSpec((MOE_TM, D), lambda i, te, nu: (i, 0)),
                  pl.BlockSpec((1, D, 2 * D), lambda i, te, nu: (te[i], 0, 0)),
                  pl.BlockSpec((1, 1, 2 * D), lambda i, te, nu: (te[i], 0, 0)),
                  pl.BlockSpec((1, D, D), lambda i, te, nu: (te[i], 0, 0)),
                  pl.BlockSpec((1, 1, D), lambda i, te, nu: (te[i], 0, 0))],
        out_specs=pl.BlockSpec((MOE_TM, D), lambda i, te, nu: (i, 0)),
        scratch_shapes=[pltpu.VMEM((D, 2 * D), BF16), pltpu.VMEM((D, D), BF16)],
    )
    return pl.pallas_call(
        _expert_kernel,
        grid_spec=grid_spec,
        out_shape=jax.ShapeDtypeStruct((n_rows, D), F32),
        compiler_params=_cparams(("arbitrary",)),
        name="experts",
    )(tile_expert, n_used, xs, w1, b1, w2, b2)


def _final_kernel(yg_ref, topw_ref, x1_ref, mod_ref, postg_ref, o_ref, *, tiles_per_batch):
    i = pl.program_id(0)
    row = i // tiles_per_batch
    tw = topw_ref[...]
    f = yg_ref[:, 0:D] * tw[:, 0:1]
    for k in range(1, TOP_K):
        f = f + yg_ref[:, k * D:(k + 1) * D] * tw[:, k:k + 1]
    g2 = mod_ref[pl.ds(row, 1), 5 * D:6 * D]
    o_ref[...] = x1_ref[...] + g2 * _rms(f, postg_ref[...])


def _final(yg, topw, x1, mod, postg, *, tm, s_lat):
    n = x1.shape[0]
    kern = functools.partial(_final_kernel, tiles_per_batch=s_lat // tm)
    return pl.pallas_call(
        kern,
        grid=(n // tm,),
        in_specs=[pl.BlockSpec((tm, TOP_K * D), lambda i: (i, 0)),
                  pl.BlockSpec((tm, LANES), lambda i: (i, 0)),
                  pl.BlockSpec((tm, D), lambda i: (i, 0)),
                  pl.BlockSpec((8, 6 * D), lambda i: (0, 0)),
                  pl.BlockSpec((1, D), lambda i: (0, 0))],
        out_specs=pl.BlockSpec((tm, D), lambda i: (i, 0)),
        out_shape=jax.ShapeDtypeStruct((n, D), F32),
        compiler_params=_cparams(("arbitrary",)),
        name="final",
    )(yg, topw, x1, mod, postg)


def _block_diag(w):
    w = w.reshape(2, 4, 4, RG_BW, RG_BW)
    eye = jnp.eye(4, dtype=w.dtype)
    return jnp.einsum('dgiab,ij->dgiajb', w, eye).reshape(2, 4, 4 * RG_BW, 4 * RG_BW)


def _grid_transpose(t, b):
    return t.reshape(b, GRID_W, GRID_W, -1).transpose(0, 2, 1, 3).reshape(t.shape)


def kernel(x, c, ctx, c_ctx, ada_w, ada_b, mix_pre_g, mix_post_g, w_in, rg_conv_w, rg_conv_b, rg_wa, rg_ba,
           rg_wi, rg_bi, rg_lam, rg_w_o, dn_conv_w, dn_a_log, dn_dt_bias, dn_norm_g, dn_w_o, w_out,
           ffn_pre_g, ffn_post_g, router_w, router_b, e_w1, e_b1, e_w2, e_b2):
    b, s_lat, _ = x.shape
    t_ctx = ctx.shape[1]
    depth = ada_w.shape[0]
    assert depth == 1 and b * 2 == SCAN_ROWS and s_lat == GRID_W * GRID_W
    n_lat = b * s_lat
    l = 0

    c8 = jnp.zeros((8, D), F32).at[0:b].set(c).at[b].set(c_ctx)
    mod = _ada_mod(c8, ada_w[l], ada_b[l].reshape(1, -1))

    wi = w_in[l]
    blk = lambda j0: wi[:, j0:j0 + D]
    w_raster = jnp.stack([blk(0), blk(D), blk(5 * D), blk(6 * D + 32), blk(7 * D + 32)]).astype(BF16)
    w_qkv = jnp.stack([blk(2 * D), blk(3 * D), blk(4 * D)]).astype(BF16)
    w_all = jnp.stack([blk(0), blk(D), blk(2 * D), blk(3 * D), blk(4 * D)]).astype(BF16)
    w_ab = jnp.pad(wi[:, 6 * D:6 * D + 32], ((0, 0), (0, LANES - 32))).astype(BF16)
    pre_g = mix_pre_g[l].reshape(1, D)
    x2d = x.reshape(n_lat, D)
    x_col = _grid_transpose(x2d, b)
    tpb = s_lat // 1024
    o_lat, _ = _inproj(x2d, pre_g, mod, w_raster, w_ab, tm=1024, tiles_per_batch=tpb, row_offset=0)
    qkv_col, ab_col = _inproj(x_col, pre_g, mod, w_qkv, w_ab, tm=1024, tiles_per_batch=tpb, row_offset=0)
    o_ctx, ab_ctx = _inproj(ctx.reshape(b * t_ctx, D), pre_g, mod, w_all, w_ab, tm=b * t_ctx,
                            tiles_per_batch=1, row_offset=b)

    wblk = jnp.stack([_block_diag(rg_wa[l]), _block_diag(rg_wi[l])], axis=1).astype(BF16)
    rg_args = (rg_conv_w[l], rg_conv_b[l], wblk, rg_ba[l], rg_bi[l], rg_lam[l])
    coef_c = _rg_coef(o_ctx[0].reshape(b, t_ctx, D), *rg_args, tt=t_ctx, reset_first=True)
    coef_l = _rg_coef(o_lat[0].reshape(b, s_lat, D), *rg_args, tt=256, reset_first=False)
    hf, hb = _rg_scan(coef_c, coef_l, t_ctx=t_ctx, t_lat=s_lat, tt=256)

    alog_l = jnp.zeros((1, LANES), F32).at[0, 0:2 * HEADS].set(dn_a_log[l].reshape(-1))
    dtb_l = jnp.zeros((1, LANES), F32).at[0, 0:2 * HEADS].set(dn_dt_bias[l].reshape(-1))
    of, ob = _delta(qkv_col, ab_col, o_ctx, ab_ctx, dn_conv_w[l], alog_l, dtb_l, b=b, s_lat=s_lat, t_ctx=t_ctx)
    of = _grid_transpose(of, b)
    ob = _grid_transpose(ob, b)

    rw = jnp.pad(router_w[l], ((0, 0), (0, LANES - N_EXPERTS)))
    rb = jnp.pad(router_b[l].reshape(1, -1), ((0, 0), (0, LANES - N_EXPERTS)))
    x1, hf2, topi, topw, rank, cnt = _post(
        x2d, o_lat, hf, hb, of, ob, mod,
        rg_w_o[l].astype(BF16), dn_w_o[l].astype(BF16), w_out[l].astype(BF16),
        jnp.tile(dn_norm_g[l], HEADS).reshape(1, D), mix_post_g[l].reshape(1, D), ffn_pre_g[l].reshape(1, D),
        rw, rb, tm=256, s_lat=s_lat)

    counts = cnt[0, 0:N_EXPERTS]
    padded = ((counts + MOE_TM - 1) // MOE_TM) * MOE_TM
    ends = jnp.cumsum(padded)
    starts = ends - padded
    top_i = topi[:, 0:TOP_K]
    pos = (starts[top_i] + rank[:, 0:TOP_K]).reshape(-1)
    n_tiles = n_lat * TOP_K // MOE_TM + N_EXPERTS
    tile_start = jnp.arange(n_tiles, dtype=jnp.int32) * MOE_TM
    tile_expert = jnp.minimum(jnp.sum(tile_start[:, None] >= ends[None, :], axis=1), N_EXPERTS - 1).astype(jnp.int32)
    n_used = (ends[-1] // MOE_TM).astype(jnp.int32).reshape(1)

    token_of = jnp.zeros((n_tiles * MOE_TM,), jnp.int32).at[pos].set(
        jnp.arange(n_lat * TOP_K, dtype=jnp.int32) // TOP_K)
    xs = jnp.take(hf2.astype(BF16), token_of, axis=0)
    ys = _experts(tile_expert, n_used, xs, e_w1[l], e_b1[l].reshape(N_EXPERTS, 1, -1),
                  e_w2[l], e_b2[l].reshape(N_EXPERTS, 1, -1))
    yg = jnp.take(ys, pos, axis=0).reshape(n_lat, TOP_K * D)

    out = _final(yg, topw, x1, mod, ffn_post_g[l].reshape(1, D), tm=512, s_lat=s_lat)
    return out.reshape(b, s_lat, D)
```

```python
import functools

import jax
import jax.numpy as jnp
from jax import lax
from jax.experimental import pallas as pl
from jax.experimental.pallas import tpu as pltpu

F32 = jnp.float32
BF16 = jnp.bfloat16
U32 = jnp.uint32
HIGHEST = lax.Precision.HIGHEST

D = 1024
EPS = 1e-6
GRID_W = 64
CHUNK = 64
HEADS = 8
DK = 128
RG_C = 8.0
RG_BW = 64
N_EXPERTS = 32
TOP_K = 4
SWIGLU_LIMIT = 7.0
SWIGLU_ALPHA = 1.702
LANES = 128
MOE_TM = 256
MOE_TB = 512
ROW_W = D // 2
VMEM_LIMIT = 56 * 1024 * 1024


def _cparams(sem):
    return pltpu.CompilerParams(dimension_semantics=sem, vmem_limit_bytes=VMEM_LIMIT)


def _sigmoid(x):
    return 1.0 / (1.0 + jnp.exp(-x))


def _softplus(y):
    return jnp.maximum(y, 0.0) + jnp.log1p(jnp.exp(-jnp.abs(y)))


def _nt_dot(a, b):
    return lax.dot_general(a, b, (((1,), (1,)), ((), ())), preferred_element_type=F32)


def _ada_kernel(c_ref, w_ref, b_ref, o_ref):
    c = c_ref[...]
    a = c * _sigmoid(c)
    o_ref[...] = jnp.dot(a, w_ref[...], preferred_element_type=F32, precision=HIGHEST) + b_ref[...]


def _ada_mod(c8, ada_w, ada_b):
    n = ada_w.shape[1]
    tn = 1024
    return pl.pallas_call(
        _ada_kernel,
        grid=(n // tn,),
        in_specs=[pl.BlockSpec((8, D), lambda j: (0, 0)),
                  pl.BlockSpec((D, tn), lambda j: (0, j)),
                  pl.BlockSpec((1, tn), lambda j: (0, j))],
        out_specs=pl.BlockSpec((8, tn), lambda j: (0, j)),
        out_shape=jax.ShapeDtypeStruct((8, n), F32),
        compiler_params=_cparams(("arbitrary",)),
        name="ada_mod",
    )(c8, ada_w, ada_b)


def _inproj_kernel(x_ref, g_ref, mod_ref, w_ref, wab_ref, o_ref, ab_ref, h_scr, *, tiles_per_batch, row_offset):
    i = pl.program_id(0)
    j = pl.program_id(1)

    @pl.when(j == 0)
    def _():
        x = x_ref[...]
        y = x * lax.rsqrt(jnp.mean(x * x, axis=-1, keepdims=True) + EPS) * g_ref[...]
        row = row_offset + i // tiles_per_batch
        sh = mod_ref[pl.ds(row, 1), 0:D]
        sc = mod_ref[pl.ds(row, 1), D:2 * D]
        h = (y * (1.0 + sc) + sh).astype(BF16)
        h_scr[...] = h
        ab_ref[...] = jnp.dot(h, wab_ref[...], preferred_element_type=F32)

    o_ref[0] = jnp.dot(h_scr[...], w_ref[0], preferred_element_type=F32).astype(BF16)


def _inproj(x2d, g, mod, w_main, w_ab, *, tm, tiles_per_batch, row_offset):
    n = x2d.shape[0]
    nj = w_main.shape[0]
    kern = functools.partial(_inproj_kernel, tiles_per_batch=tiles_per_batch, row_offset=row_offset)
    return pl.pallas_call(
        kern,
        grid=(n // tm, nj),
        in_specs=[pl.BlockSpec((tm, D), lambda i, j: (i, 0)),
                  pl.BlockSpec((1, D), lambda i, j: (0, 0)),
                  pl.BlockSpec((8, 6 * D), lambda i, j: (0, 0)),
                  pl.BlockSpec((1, D, D), lambda i, j: (j, 0, 0)),
                  pl.BlockSpec((D, LANES), lambda i, j: (0, 0))],
        out_specs=[pl.BlockSpec((1, tm, D), lambda i, j: (j, i, 0)),
                   pl.BlockSpec((tm, LANES), lambda i, j: (i, 0))],
        out_shape=[jax.ShapeDtypeStruct((nj, n, D), BF16),
                   jax.ShapeDtypeStruct((n, LANES), F32)],
        scratch_shapes=[pltpu.VMEM((tm, D), BF16)],
        compiler_params=_cparams(("arbitrary", "arbitrary")),
        name="inproj",
    )(x2d, g, mod, w_main, w_ab)


RG_HALO = 16
SCAN_ROWS = 8
SCAN_SLABS = 4


def _pack_bf16_pair(hi, lo):
    hi_bits = lax.bitcast_convert_type(hi.astype(BF16).astype(F32), U32)
    lo_bits = lax.bitcast_convert_type(lo.astype(BF16).astype(F32), U32)
    return hi_bits | (lo_bits >> 16)


def _rg_coef_kernel(cur_ref, prev_ref, next_ref, cw_ref, cb_ref, wblk_ref, ba_ref, bi_ref, lam_ref,
                    wf_ref, wb_ref, *, tt, nt, reset_first):
    i = pl.program_id(0)
    b = pl.program_id(1)
    x = cur_ref[0].astype(F32)
    xp = jnp.where(i > 0, prev_ref[0].astype(F32), 0.0)
    xn = jnp.where(i < nt - 1, next_ref[0].astype(F32), 0.0)
    xe_f = jnp.concatenate([xp, x], axis=0)
    xe_b = jnp.concatenate([x, xn], axis=0)
    n_ext = tt + RG_HALO
    row = lax.broadcasted_iota(jnp.int32, (tt, 1), 0)
    outs = (wf_ref, wb_ref)
    for d in (0, 1):
        w = cw_ref[d]
        acc = x * w[3:4]
        for sft in (1, 2, 3):
            if d == 0:
                sh = pltpu.roll(xe_f, sft, axis=0)[RG_HALO:RG_HALO + tt]
            else:
                sh = pltpu.roll(xe_b, n_ext - sft, axis=0)[0:tt]
            acc = acc + sh * w[3 - sft:4 - sft]
        xc = acc + cb_ref[d:d + 1]
        xcb = xc.astype(BF16)
        pre_a = jnp.concatenate(
            [jnp.dot(xcb[:, g * 256:(g + 1) * 256], wblk_ref[d, 0, g], preferred_element_type=F32)
             for g in range(4)], axis=1) + ba_ref[d:d + 1]
        pre_i = jnp.concatenate(
            [jnp.dot(xcb[:, g * 256:(g + 1) * 256], wblk_ref[d, 1, g], preferred_element_type=F32)
             for g in range(4)], axis=1) + bi_ref[d:d + 1]
        r = _sigmoid(pre_a)
        ig = _sigmoid(pre_i)
        la = r * (-RG_C * _softplus(-lam_ref[d:d + 1]))
        a = jnp.exp(la)
        mult = jnp.sqrt(1.0 - a * a)
        if reset_first:
            if d == 0:
                first = (row == 0) & (i == 0)
            else:
                first = (row == tt - 1) & (i == nt - 1)
            mult = jnp.where(first, 1.0, mult)
        word = _pack_bf16_pair(la, mult * ig * xc)
        for half in (0, 1):
            for q in range(SCAN_SLABS):
                c0 = (half * SCAN_SLABS + q) * LANES
                outs[d][q, pl.ds(b * 2 + half, tt, stride=SCAN_ROWS), :] = word[:, c0:c0 + LANES]


def _rg_coef(xr, conv_w, conv_b, wblk, ba, bi, lam, *, tt, reset_first):
    b, t, _ = xr.shape
    nt = t // tt
    hb = tt // RG_HALO
    kern = functools.partial(_rg_coef_kernel, tt=tt, nt=nt, reset_first=reset_first)
    full = lambda shape: pl.BlockSpec(shape, lambda i, bb: (0,) * len(shape))
    out_sds = jax.ShapeDtypeStruct((SCAN_SLABS, t * SCAN_ROWS, LANES), U32)
    out_spec = pl.BlockSpec((SCAN_SLABS, tt * SCAN_ROWS, LANES), lambda i, bb: (0, i, 0))
    return pl.pallas_call(
        kern,
        grid=(nt, b),
        in_specs=[pl.BlockSpec((1, tt, D), lambda i, bb: (bb, i, 0)),
                  pl.BlockSpec((1, RG_HALO, D), lambda i, bb: (bb, jnp.maximum(i * hb - 1, 0), 0)),
                  pl.BlockSpec((1, RG_HALO, D), lambda i, bb: (bb, jnp.minimum((i + 1) * hb, t // RG_HALO - 1), 0)),
                  full((2, 4, D)), full((2, D)), full((2, 2, 4, 256, 256)),
                  full((2, D)), full((2, D)), full((2, D))],
        out_specs=[out_spec, out_spec],
        out_shape=[out_sds, out_sds],
        compiler_params=_cparams(("arbitrary", "arbitrary")),
        name="rg_coef",
    )(xr, xr, xr, conv_w, conv_b, wblk, ba, bi, lam)


def _scan_step(word, h):
    la = lax.bitcast_convert_type(word & jnp.uint32(0xFFFF0000), F32)
    bb = lax.bitcast_convert_type(word << 16, F32)
    return jnp.exp(la) * h + bb


def _rg_scan_run(wf, wb, hf, hb, carry, n):
    def body(t, c):
        h_f, h_b = c
        r0 = pl.multiple_of(t * SCAN_ROWS, SCAN_ROWS)
        r1 = pl.multiple_of((n - 1 - t) * SCAN_ROWS, SCAN_ROWS)
        new_f, new_b = [], []
        for q in range(SCAN_SLABS):
            f = _scan_step(wf[q, pl.ds(r0, SCAN_ROWS), :], h_f[q])
            g = _scan_step(wb[q, pl.ds(r1, SCAN_ROWS), :], h_b[q])
            if hf is not None:
                hf[q, pl.ds(r0, SCAN_ROWS), :] = f
                hb[q, pl.ds(r1, SCAN_ROWS), :] = g
            new_f.append(f)
            new_b.append(g)
        return tuple(new_f), tuple(new_b)

    init = (tuple(carry[0, q] for q in range(SCAN_SLABS)), tuple(carry[1, q] for q in range(SCAN_SLABS)))
    h_f, h_b = lax.fori_loop(0, n, body, init, unroll=8)
    for q in range(SCAN_SLABS):
        carry[0, q] = h_f[q]
        carry[1, q] = h_b[q]


def _rg_scan_kernel(wf_c, wb_c, wf_l, wb_l, hf_ref, hb_ref, carry, *, t_ctx, tt):
    s = pl.program_id(0)

    @pl.when(s == 0)
    def _():
        carry[...] = jnp.zeros(carry.shape, F32)
        _rg_scan_run(wf_c, wb_c, None, None, carry, t_ctx)

    @pl.when(s > 0)
    def _():
        _rg_scan_run(wf_l, wb_l, hf_ref, hb_ref, carry, tt)


def _rg_scan(coef_c, coef_l, *, t_ctx, t_lat, tt):
    nt = t_lat // tt
    ctx_spec = pl.BlockSpec((SCAN_SLABS, t_ctx * SCAN_ROWS, LANES), lambda s: (0, 0, 0))
    f_spec = pl.BlockSpec((SCAN_SLABS, tt * SCAN_ROWS, LANES), lambda s: (0, jnp.maximum(s - 1, 0), 0))
    b_spec = pl.BlockSpec((SCAN_SLABS, tt * SCAN_ROWS, LANES), lambda s: (0, nt - jnp.maximum(s, 1), 0))
    out_sds = jax.ShapeDtypeStruct((SCAN_SLABS, t_lat * SCAN_ROWS, LANES), F32)
    kern = functools.partial(_rg_scan_kernel, t_ctx=t_ctx, tt=tt)
    return pl.pallas_call(
        kern,
        grid=(nt + 1,),
        in_specs=[ctx_spec, ctx_spec, f_spec, b_spec],
        out_specs=[f_spec, b_spec],
        out_shape=[out_sds, out_sds],
        scratch_shapes=[pltpu.VMEM((2, SCAN_SLABS, SCAN_ROWS, LANES), F32)],
        compiler_params=_cparams(("arbitrary",)),
        name="rg_scan",
    )(coef_c[0], coef_c[1], coef_l[0], coef_l[1])


N_CTX_CHUNKS = 4
CONV_SLABS = 3 * D // LANES
NEUMANN_STEPS = 6


def _delta_prep(d, is_ctx, refs_l, refs_c, prevraw, cw_ref, alog_ref, dtb_ref):
    q_l, k_l, v_l, ab_l = refs_l
    q_c, k_c, v_c, ab_c = refs_c

    def sel(rc, rl):
        return jnp.where(is_ctx, rc[0], rl[0])

    raw = jnp.concatenate([sel(q_c, q_l), sel(k_c, k_l), sel(v_c, v_l)], axis=1).astype(F32)
    abv = jnp.where(is_ctx, ab_c[...], ab_l[...])
    base = 8 if d == 0 else 0
    for j in range(CONV_SLABS):
        prevraw[d, j, base:base + CHUNK, :] = raw[:, j * LANES:(j + 1) * LANES]
    w = cw_ref[d]
    outs = []
    for j in range(CONV_SLABS):
        wj = w[:, j * LANES:(j + 1) * LANES]
        acc = raw[:, j * LANES:(j + 1) * LANES] * wj[3:4]
        for sft in (1, 2, 3):
            off = base - sft if d == 0 else sft
            acc = acc + prevraw[d, j, off:off + CHUNK, :] * wj[3 - sft:4 - sft]
        outs.append(acc)
    acc = jnp.concatenate(outs, axis=1)
    if d == 0:
        prevraw[d, :, 0:8, :] = prevraw[d, :, CHUNK:CHUNK + 8, :]
    else:
        prevraw[d, :, CHUNK:CHUNK + 8, :] = prevraw[d, :, 0:8, :]
    y = acc * _sigmoid(acc)

    bi = lax.broadcasted_iota(jnp.int32, (2 * DK, 2 * DK), 0) // DK
    bj = lax.broadcasted_iota(jnp.int32, (2 * DK, 2 * DK), 1) // DK
    ones_blk = (bi == bj).astype(BF16)
    qk_sq = (y[:, 0:2 * D] * y[:, 0:2 * D]).astype(BF16)
    ss = jnp.concatenate(
        [jnp.dot(qk_sq[:, g * 2 * DK:(g + 1) * 2 * DK], ones_blk, preferred_element_type=F32)
         for g in range(2 * D // (2 * DK))], axis=1)
    inv = lax.rsqrt(ss + EPS)
    qn_all = y[:, 0:D] * inv[:, 0:D] * (DK ** -0.5)
    kn_all = y[:, D:2 * D] * inv[:, D:2 * D]
    y = jnp.concatenate([qn_all, kn_all, y[:, 2 * D:3 * D]], axis=1)

    rp = lax.broadcasted_iota(jnp.int32, (2 * CHUNK, CHUNK), 0)
    cp = lax.broadcasted_iota(jnp.int32, (2 * CHUNK, CHUNK), 1)
    if d == 0:
        m_pad = ((rp >= cp) & (rp < CHUNK)).astype(F32)
    else:
        m_pad = ((rp <= cp) & (rp < CHUNK)).astype(F32)
    g_all = -jnp.exp(alog_ref[...]) * _softplus(abv + dtb_ref[...])
    beta_all = _sigmoid(abv)
    gcum = jnp.dot(m_pad, g_all, preferred_element_type=F32, precision=HIGHEST)
    return y, gcum, gcum.T, beta_all


def _delta_heads(d, y, gcum, gcum_t, beta_all, out_ref, state, ybuf):
    ri = lax.broadcasted_iota(jnp.int32, (CHUNK, CHUNK), 0)
    ci = lax.broadcasted_iota(jnp.int32, (CHUNK, CHUNK), 1)
    if d == 0:
        incl, strict, last = ri >= ci, ri > ci, CHUNK - 1
    else:
        incl, strict, last = ri <= ci, ri < ci, 0
    eye = (ri == ci).astype(F32)
    keep = lax.broadcasted_iota(jnp.int32, (CHUNK, 2 * CHUNK), 1) >= CHUNK
    zeros_half = jnp.zeros((CHUNK, DK), F32)
    heads = range(HEADS)

    ybuf[d] = y
    q_of = lambda h: ybuf[d, :, h * DK:(h + 1) * DK]
    k_of = lambda h: ybuf[d, :, D + h * DK:D + (h + 1) * DK]
    v_of = lambda h: ybuf[d, :, 2 * D + h * DK:2 * D + (h + 1) * DK]
    gcol_of = lambda h: gcum[0:CHUNK, d * HEADS + h:d * HEADS + h + 1]
    bcol_of = lambda h: beta_all[:, 2 * HEADS + d * HEADS + h:2 * HEADS + d * HEADS + h + 1]

    qmat, qkd = [], []
    for h in heads:
        la = d * HEADS + h
        grow = gcum_t[la:la + 1, 0:CHUNK]
        kn = k_of(h)
        knb = kn.astype(BF16)
        a_mat = _nt_dot((kn * bcol_of(h)).astype(BF16), knb)
        qk = _nt_dot(q_of(h).astype(BF16), knb)
        dec = jnp.where(incl, jnp.exp(jnp.where(incl, gcol_of(h) - grow, 0.0)), 0.0)
        s0 = jnp.where(strict, -a_mat * dec, 0.0)
        qmat.append(jnp.concatenate([s0, eye], axis=1))
        qkd.append((qk * dec).astype(BF16))

    for _ in range(NEUMANN_STEPS):
        for h in heads:
            qm = qmat[h]
            qmat[h] = jnp.dot(qm[:, 0:CHUNK].astype(BF16), qm.astype(BF16),
                              preferred_element_type=F32) + jnp.where(keep, qm, 0.0)

    sol = []
    for h in heads:
        bcol = bcol_of(h)
        rhs = jnp.concatenate([v_of(h) * bcol, k_of(h) * (bcol * jnp.exp(gcol_of(h)))], axis=1).astype(BF16)
        sol.append(jnp.dot(qmat[h][:, CHUNK:2 * CHUNK].astype(BF16), rhs, preferred_element_type=F32))
    wqh = []
    for h in heads:
        wq = jnp.concatenate([sol[h][:, DK:2 * DK], q_of(h) * jnp.exp(gcol_of(h))], axis=0)
        wqh.append(jnp.dot(wq.astype(BF16), state[d, h].astype(BF16), preferred_element_type=F32))
    v_new = [sol[h][:, 0:DK] - wqh[h][0:CHUNK] for h in heads]
    for h in heads:
        o = wqh[h][CHUNK:2 * CHUNK] + jnp.dot(qkd[h], v_new[h].astype(BF16), preferred_element_type=F32)
        out_ref[:, h * DK:(h + 1) * DK] = o.astype(BF16)
    for h in heads:
        la = d * HEADS + h
        gl = gcum[last:last + 1, la:la + 1]
        kd = k_of(h) * jnp.exp(gl - gcol_of(h))
        kd_t = jnp.concatenate([kd, zeros_half], axis=0).T
        v_pad = jnp.concatenate([v_new[h], zeros_half], axis=0)
        state[d, h] = state[d, h] * jnp.exp(gl) + jnp.dot(kd_t.astype(BF16), v_pad.astype(BF16),
                                                         preferred_element_type=F32)


def _delta_kernel(qf_l, kf_l, vf_l, abf_l, qb_l, kb_l, vb_l, abb_l,
                  qf_c, kf_c, vf_c, abf_c, qb_c, kb_c, vb_c, abb_c,
                  cw_ref, alog_ref, dtb_ref, of_ref, ob_ref, state, prevraw, ybuf):
    s = pl.program_id(1)
    is_ctx = s < N_CTX_CHUNKS

    @pl.when(s == 0)
    def _():
        state[...] = jnp.zeros(state.shape, F32)

    @pl.when((s == 0) | (s == N_CTX_CHUNKS))
    def _():
        prevraw[...] = jnp.zeros(prevraw.shape, F32)

    pf = _delta_prep(0, is_ctx, (qf_l, kf_l, vf_l, abf_l), (qf_c, kf_c, vf_c, abf_c), prevraw,
                     cw_ref, alog_ref, dtb_ref)
    pb = _delta_prep(1, is_ctx, (qb_l, kb_l, vb_l, abb_l), (qb_c, kb_c, vb_c, abb_c), prevraw,
                     cw_ref, alog_ref, dtb_ref)
    _delta_heads(0, *pf, of_ref, state, ybuf)
    _delta_heads(1, *pb, ob_ref, state, ybuf)


def _delta(qkv_col, ab_col, o_ctx, ab_ctx, conv_w, alog_l, dtb_l, *, b, s_lat, t_ctx):
    ncol = s_lat // CHUNK
    assert s_lat // GRID_W == CHUNK and t_ctx == N_CTX_CHUNKS * CHUNK

    col_f = lambda bb, s: bb * ncol + jnp.maximum(s - N_CTX_CHUNKS, 0)
    col_b = lambda bb, s: bb * ncol + ncol - 1 - jnp.maximum(s - N_CTX_CHUNKS, 0)
    cch_f = lambda bb, s: bb * N_CTX_CHUNKS + jnp.minimum(s, N_CTX_CHUNKS - 1)
    cch_b = lambda bb, s: bb * N_CTX_CHUNKS + N_CTX_CHUNKS - 1 - jnp.minimum(s, N_CTX_CHUNKS - 1)

    def tok_spec(j, fn):
        return pl.BlockSpec((1, CHUNK, D), lambda bb, s: (j, fn(bb, s), 0))

    ab_spec = lambda fn: pl.BlockSpec((CHUNK, LANES), lambda bb, s: (fn(bb, s), 0))
    full = lambda shape: pl.BlockSpec(shape, lambda bb, s: (0,) * len(shape))

    in_specs = ([tok_spec(0, col_f), tok_spec(1, col_f), tok_spec(2, col_f), ab_spec(col_f),
                 tok_spec(0, col_b), tok_spec(1, col_b), tok_spec(2, col_b), ab_spec(col_b),
                 tok_spec(2, cch_f), tok_spec(3, cch_f), tok_spec(4, cch_f), ab_spec(cch_f),
                 tok_spec(2, cch_b), tok_spec(3, cch_b), tok_spec(4, cch_b), ab_spec(cch_b),
                 full((2, 4, 3 * D)), full((1, LANES)), full((1, LANES))])
    out_sds = jax.ShapeDtypeStruct((b * s_lat, D), BF16)
    out_specs = [pl.BlockSpec((CHUNK, D), lambda bb, s: (col_f(bb, s), 0)),
                 pl.BlockSpec((CHUNK, D), lambda bb, s: (col_b(bb, s), 0))]
    return pl.pallas_call(
        _delta_kernel,
        grid=(b, N_CTX_CHUNKS + ncol),
        in_specs=in_specs,
        out_specs=out_specs,
        out_shape=[out_sds, out_sds],
        scratch_shapes=[pltpu.VMEM((2, HEADS, DK, DK), F32), pltpu.VMEM((2, CONV_SLABS, CHUNK + 8, LANES), F32),
                        pltpu.VMEM((2, CHUNK, 3 * D), F32)],
        compiler_params=_cparams(("arbitrary", "arbitrary")),
        name="delta",
    )(qkv_col, qkv_col, qkv_col, ab_col, qkv_col, qkv_col, qkv_col, ab_col,
      o_ctx, o_ctx, o_ctx, ab_ctx, o_ctx, o_ctx, o_ctx, ab_ctx,
      conv_w, alog_l, dtb_l)


def _rms(x, g):
    return x * lax.rsqrt(jnp.mean(x * x, axis=-1, keepdims=True) + EPS) * g


def _scan_rows(ref, b, tm):
    return jnp.concatenate(
        [ref[q, pl.ds(b * 2 + half, tm, stride=SCAN_ROWS), :] for half in (0, 1) for q in range(SCAN_SLABS)],
        axis=1)


def _post_kernel(x_ref, gate_ref, z_ref, grg_ref, gdn_ref, hf_ref, hb_ref, of_ref, ob_ref, mod_ref,
                 wrg_ref, wdn_ref, wout_ref, dng_ref, postg_ref, preg_ref, rw_ref, rb_ref,
                 x1_ref, hf2_ref, topi_ref, topw_ref, rank_ref, cnt_ref, carry, *, tm):
    b = pl.program_id(1)

    @pl.when((pl.program_id(0) == 0) & (b == 0))
    def _():
        carry[...] = jnp.zeros(carry.shape, F32)

    gate = gate_ref[0].astype(F32)
    gelu = 0.5 * gate * (1.0 + jnp.tanh(0.7978845608028654 * (gate + 0.044715 * gate * gate * gate)))
    rg_h = _scan_rows(hf_ref, b, tm) + _scan_rows(hb_ref, b, tm)
    y_rg = jnp.dot((rg_h * gelu).astype(BF16), wrg_ref[...], preferred_element_type=F32)

    dn = of_ref[...].astype(F32) + ob_ref[...].astype(F32)
    z = z_ref[0].astype(F32)
    parts = []
    for h in range(HEADS):
        seg = dn[:, h * DK:(h + 1) * DK]
        parts.append(seg * lax.rsqrt(jnp.mean(seg * seg, axis=-1, keepdims=True) + EPS))
    o = jnp.concatenate(parts, axis=1) * dng_ref[...] * (z * _sigmoid(z))
    y_dn = jnp.dot(o.astype(BF16), wdn_ref[...], preferred_element_type=F32)

    merged = _sigmoid(grg_ref[0].astype(F32)) * y_rg + _sigmoid(gdn_ref[0].astype(F32)) * y_dn
    y = jnp.dot(merged.astype(BF16), wout_ref[...], preferred_element_type=F32)

    g1 = mod_ref[pl.ds(b, 1), 2 * D:3 * D]
    sh2 = mod_ref[pl.ds(b, 1), 3 * D:4 * D]
    sc2 = mod_ref[pl.ds(b, 1), 4 * D:5 * D]
    x1 = x_ref[...] + g1 * _rms(y, postg_ref[...])
    x1_ref[...] = x1
    hf2 = _rms(x1, preg_ref[...]) * (1.0 + sc2) + sh2
    hf2_ref[...] = _pack_bf16_pair(hf2[:, 0:ROW_W], hf2[:, ROW_W:D])

    lane = lax.broadcasted_iota(jnp.int32, (tm, LANES), 1)
    lane_f = lane.astype(F32)
    logits = jnp.dot(hf2, rw_ref[...], preferred_element_type=F32, precision=HIGHEST) + rb_ref[...]
    neg = jnp.float32(-jnp.inf)
    lg = jnp.where(lane < N_EXPERTS, logits, neg)
    vals, sels = [], []
    topi = jnp.zeros((tm, LANES), F32)
    onehot = jnp.zeros((tm, LANES), F32)
    for k in range(TOP_K):
        m = jnp.max(lg, axis=-1, keepdims=True)
        idx = jnp.min(jnp.where(lg == m, lane_f, float(LANES)), axis=-1, keepdims=True)
        sel = lane_f == idx
        vals.append(m)
        sels.append(sel)
        lg = jnp.where(sel, neg, lg)
        onehot = onehot + sel.astype(F32)
        topi = topi + jnp.where(lane == k, idx, 0.0)
    es = [jnp.exp(v - vals[0]) for v in vals]
    den = es[0] + es[1] + es[2] + es[3]
    topw = jnp.zeros((tm, LANES), F32)
    for k in range(TOP_K):
        topw = topw + jnp.where(lane == k, es[k] / den, 0.0)

    ri = lax.broadcasted_iota(jnp.int32, (tm, tm), 0)
    ci = lax.broadcasted_iota(jnp.int32, (tm, tm), 1)
    tri = (ri > ci).astype(BF16)
    cum = jnp.dot(tri, onehot.astype(BF16), preferred_element_type=F32) + carry[...]
    rank = jnp.zeros((tm, LANES), F32)
    for k in range(TOP_K):
        rk = jnp.sum(jnp.where(sels[k], cum, 0.0), axis=-1, keepdims=True)
        rank = rank + jnp.where(lane == k, rk, 0.0)
    new_carry = carry[...] + jnp.sum(onehot, axis=0, keepdims=True)
    carry[...] = new_carry
    topi_ref[...] = topi.astype(jnp.int32)
    topw_ref[...] = topw
    rank_ref[...] = rank.astype(jnp.int32)
    cnt_ref[...] = jnp.broadcast_to(new_carry, (8, LANES)).astype(jnp.int32)


def _post(x2d, o_lat, hf, hb, of, ob, mod, wrg, wdn, wout, dng, postg, preg, rw, rb, *, tm, s_lat):
    n = x2d.shape[0]
    tpb = s_lat // tm
    nb = n // s_lat
    kern = functools.partial(_post_kernel, tm=tm)
    tok = pl.BlockSpec((tm, D), lambda t, b: (b * tpb + t, 0))
    oj = lambda j: pl.BlockSpec((1, tm, D), lambda t, b: (j, b * tpb + t, 0))
    hspec = pl.BlockSpec((SCAN_SLABS, tm * SCAN_ROWS, LANES), lambda t, b: (0, t, 0))
    full = lambda shape: pl.BlockSpec(shape, lambda t, b: (0,) * len(shape))
    lane_out = pl.BlockSpec((tm, LANES), lambda t, b: (b * tpb + t, 0))
    return pl.pallas_call(
        kern,
        grid=(tpb, nb),
        in_specs=[tok, oj(1), oj(2), oj(3), oj(4), hspec, hspec, tok, tok, full((8, 6 * D)),
                  full((D, D)), full((D, D)), full((D, D)), full((1, D)), full((1, D)), full((1, D)),
                  full((D, LANES)), full((1, LANES))],
        out_specs=[tok, pl.BlockSpec((tm, ROW_W), lambda t, b: (b * tpb + t, 0)), lane_out, lane_out, lane_out,
                   pl.BlockSpec((8, LANES), lambda t, b: (0, 0))],
        out_shape=[jax.ShapeDtypeStruct((n, D), F32), jax.ShapeDtypeStruct((n, ROW_W), U32),
                   jax.ShapeDtypeStruct((n, LANES), jnp.int32), jax.ShapeDtypeStruct((n, LANES), F32),
                   jax.ShapeDtypeStruct((n, LANES), jnp.int32), jax.ShapeDtypeStruct((8, LANES), jnp.int32)],
        scratch_shapes=[pltpu.VMEM((1, LANES), F32)],
        compiler_params=_cparams(("arbitrary", "arbitrary")),
        name="post_mix",
    )(x2d, o_lat, o_lat, o_lat, o_lat, hf, hb, of, ob, mod, wrg, wdn, wout, dng, postg, preg, rw, rb)


def _unpack_bf16_pair(w):
    hi = lax.bitcast_convert_type(w & jnp.uint32(0xFFFF0000), F32)
    lo = lax.bitcast_convert_type(w << 16, F32)
    return hi, lo


def _idx_copy(pos_hbm, idx, isem, tile, slot):
    return pltpu.make_async_copy(pos_hbm.at[tile], idx.at[slot], isem.at[slot])


def _dispatch_kernel(starts_ref, ends_ref, pos_hbm, x_ref, xs_hbm, idx, zbuf, sem, isem, *, tb):
    i = pl.program_id(0)
    n = pl.num_programs(0)
    slot = i % 2

    @pl.when(i == 0)
    def _():
        _idx_copy(pos_hbm, idx, isem, 0, 0).start()
        zbuf[...] = jnp.zeros(zbuf.shape, U32)

        def zero_copy(e):
            r0 = pl.multiple_of(ends_ref[e] - MOE_TM, MOE_TM)
            return pltpu.make_async_copy(zbuf, xs_hbm.at[pl.ds(r0, MOE_TM)], sem)

        for e in range(N_EXPERTS):
            @pl.when(ends_ref[e] > starts_ref[e])
            def _():
                zero_copy(e).start()
        for e in range(N_EXPERTS):
            @pl.when(ends_ref[e] > starts_ref[e])
            def _():
                zero_copy(e).wait()

        def tail_copy(t):
            return pltpu.make_async_copy(zbuf, xs_hbm.at[pl.ds(pl.multiple_of(t * MOE_TM, MOE_TM), MOE_TM)], sem)

        n_used = ends_ref[N_EXPERTS - 1] // MOE_TM
        n_tiles = xs_hbm.shape[0] // MOE_TM

        def tail_start(t, c):
            tail_copy(t).start()
            return c

        def tail_wait(t, c):
            tail_copy(t).wait()
            return c

        lax.fori_loop(n_used, n_tiles, tail_start, 0)
        lax.fori_loop(n_used, n_tiles, tail_wait, 0)

    @pl.when(i + 1 < n)
    def _():
        _idx_copy(pos_hbm, idx, isem, i + 1, 1 - slot).start()

    _idx_copy(pos_hbm, idx, isem, i, slot).wait()

    def body(r, c):
        for k in range(TOP_K):
            p = idx[slot, r * TOP_K + k]
            pltpu.make_async_copy(x_ref.at[pl.ds(r, 1)], xs_hbm.at[pl.ds(p, 1)], sem).start(priority=k % 2)
        return c

    lax.fori_loop(0, tb, body, 0, unroll=8)
    for k in range(TOP_K):
        pltpu.make_async_copy(x_ref, xs_hbm.at[pl.ds(0, tb)], sem).wait()


def _dispatch(starts, ends, pos2d, xp, n_rows, *, tb):
    n = xp.shape[0]
    grid_spec = pltpu.PrefetchScalarGridSpec(
        num_scalar_prefetch=2,
        grid=(n // tb,),
        in_specs=[pl.BlockSpec(memory_space=pl.ANY),
                  pl.BlockSpec((tb, ROW_W), lambda i, st, en: (i, 0))],
        out_specs=pl.BlockSpec(memory_space=pl.ANY),
        scratch_shapes=[pltpu.SMEM((2, tb * TOP_K), jnp.int32), pltpu.VMEM((MOE_TM, ROW_W), U32),
                        pltpu.SemaphoreType.DMA, pltpu.SemaphoreType.DMA((2,))],
    )
    return pl.pallas_call(
        functools.partial(_dispatch_kernel, tb=tb),
        grid_spec=grid_spec,
        out_shape=jax.ShapeDtypeStruct((n_rows, ROW_W), U32),
        compiler_params=_cparams(("arbitrary",)),
        name="dispatch",
    )(starts, ends, pos2d, xp)


def _expert_kernel(te_ref, nu_ref, xs_ref, w1_ref, b1_ref, w2_ref, b2_ref, ys_ref, w1b, w2b):
    i = pl.program_id(0)

    @pl.when(i < nu_ref[0])
    def _():
        changed = (i == 0) | (te_ref[i] != te_ref[jnp.maximum(i - 1, 0)])

        @pl.when(changed)
        def _():
            w1b[...] = w1_ref[0].astype(BF16)
            w2b[...] = w2_ref[0].astype(BF16)

        hi, lo = _unpack_bf16_pair(xs_ref[...])
        x = jnp.concatenate([hi, lo], axis=1).astype(BF16)
        hh = jnp.dot(x, w1b[...], preferred_element_type=F32) + b1_ref[0]
        glu = jnp.minimum(hh[:, 0:D], SWIGLU_LIMIT)
        lin = jnp.clip(hh[:, D:2 * D], -SWIGLU_LIMIT, SWIGLU_LIMIT)
        act = glu * _sigmoid(SWIGLU_ALPHA * glu) * (lin + 1.0)
        out = jnp.dot(act.astype(BF16), w2b[...], preferred_element_type=F32) + b2_ref[0]
        ys_ref[...] = _pack_bf16_pair(out[:, 0:ROW_W], out[:, ROW_W:D])

    @pl.when(i >= nu_ref[0])
    def _():
        ys_ref[...] = jnp.zeros(ys_ref.shape, U32)


def _experts(tile_expert, n_used, xs, w1, b1, w2, b2):
    n_rows = xs.shape[0]
    n_tiles = n_rows // MOE_TM
    row_map = lambda i, te, nu: (jnp.minimum(i, nu[0] - 1), 0)
    grid_spec = pltpu.PrefetchScalarGridSpec(
        num_scalar_prefetch=2,
        grid=(n_tiles,),
        in_specs=[pl.BlockSpec((MOE_TM, ROW_W), row_map),
                  pl.BlockSpec((1, D, 2 * D), lambda i, te, nu: (te[i], 0, 0)),
                  pl.BlockSpec((1, 1, 2 * D), lambda i, te, nu: (te[i], 0, 0)),
                  pl.BlockSpec((1, D, D), lambda i, te, nu: (te[i], 0, 0)),
                  pl.BlockSpec((1, 1, D), lambda i, te, nu: (te[i], 0, 0))],
        out_specs=pl.BlockSpec((MOE_TM, ROW_W), lambda i, te, nu: (i, 0)),
        scratch_shapes=[pltpu.VMEM((D, 2 * D), BF16), pltpu.VMEM((D, D), BF16)],
    )
    return pl.pallas_call(
        _expert_kernel,
        grid_spec=grid_spec,
        out_shape=jax.ShapeDtypeStruct((n_rows, ROW_W), U32),
        compiler_params=_cparams(("arbitrary",)),
        name="experts",
    )(tile_expert, n_used, xs, w1, b1, w2, b2)


def _final_kernel(pos_hbm, ys_hbm, topw_ref, x1_ref, mod_ref, postg_ref, o_ref, idx, buf, sem, isem,
                  *, tb, tiles_per_batch):
    i = pl.program_id(0)
    n = pl.num_programs(0)
    slot = i % 2

    @pl.when(i == 0)
    def _():
        _idx_copy(pos_hbm, idx, isem, 0, 0).start()

    @pl.when(i + 1 < n)
    def _():
        _idx_copy(pos_hbm, idx, isem, i + 1, 1 - slot).start()

    _idx_copy(pos_hbm, idx, isem, i, slot).wait()

    def body(r, c):
        for k in range(TOP_K):
            p = idx[slot, r * TOP_K + k]
            pltpu.make_async_copy(ys_hbm.at[pl.ds(p, 1)], buf.at[k, pl.ds(r, 1)], sem).start(priority=k % 2)
        return c

    lax.fori_loop(0, tb, body, 0, unroll=8)
    for k in range(TOP_K):
        pltpu.make_async_copy(ys_hbm.at[pl.ds(0, tb)], buf.at[k], sem).wait()

    row = i // tiles_per_batch
    tw = topw_ref[...]
    f_hi = jnp.zeros((tb, ROW_W), F32)
    f_lo = jnp.zeros((tb, ROW_W), F32)
    for k in range(TOP_K):
        hi, lo = _unpack_bf16_pair(buf[k])
        f_hi = f_hi + hi * tw[:, k:k + 1]
        f_lo = f_lo + lo * tw[:, k:k + 1]
    f = jnp.concatenate([f_hi, f_lo], axis=1)
    g2 = mod_ref[pl.ds(row, 1), 5 * D:6 * D]
    o_ref[...] = x1_ref[...] + g2 * _rms(f, postg_ref[...])


def _final(pos2d, ys, topw, x1, mod, postg, *, tb, s_lat):
    n = x1.shape[0]
    kern = functools.partial(_final_kernel, tb=tb, tiles_per_batch=s_lat // tb)
    return pl.pallas_call(
        kern,
        grid=(n // tb,),
        in_specs=[pl.BlockSpec(memory_space=pl.ANY),
                  pl.BlockSpec(memory_space=pl.ANY),
                  pl.BlockSpec((tb, LANES), lambda i: (i, 0)),
                  pl.BlockSpec((tb, D), lambda i: (i, 0)),
                  pl.BlockSpec((8, 6 * D), lambda i: (0, 0)),
                  pl.BlockSpec((1, D), lambda i: (0, 0))],
        out_specs=pl.BlockSpec((tb, D), lambda i: (i, 0)),
        out_shape=jax.ShapeDtypeStruct((n, D), F32),
        scratch_shapes=[pltpu.SMEM((2, tb * TOP_K), jnp.int32), pltpu.VMEM((TOP_K, tb, ROW_W), U32),
                        pltpu.SemaphoreType.DMA, pltpu.SemaphoreType.DMA((2,))],
        compiler_params=_cparams(("arbitrary",)),
        name="final",
    )(pos2d, ys, topw, x1, mod, postg)


def _block_diag(w):
    w = w.reshape(2, 4, 4, RG_BW, RG_BW)
    eye = jnp.eye(4, dtype=w.dtype)
    return jnp.einsum('dgiab,ij->dgiajb', w, eye).reshape(2, 4, 4 * RG_BW, 4 * RG_BW)


def _grid_transpose(t, b):
    return t.reshape(b, GRID_W, GRID_W, -1).transpose(0, 2, 1, 3).reshape(t.shape)


def kernel(x, c, ctx, c_ctx, ada_w, ada_b, mix_pre_g, mix_post_g, w_in, rg_conv_w, rg_conv_b, rg_wa, rg_ba,
           rg_wi, rg_bi, rg_lam, rg_w_o, dn_conv_w, dn_a_log, dn_dt_bias, dn_norm_g, dn_w_o, w_out,
           ffn_pre_g, ffn_post_g, router_w, router_b, e_w1, e_b1, e_w2, e_b2):
    b, s_lat, _ = x.shape
    t_ctx = ctx.shape[1]
    depth = ada_w.shape[0]
    assert depth == 1 and b * 2 == SCAN_ROWS and s_lat == GRID_W * GRID_W
    n_lat = b * s_lat
    l = 0

    c8 = jnp.zeros((8, D), F32).at[0:b].set(c).at[b].set(c_ctx)
    mod = _ada_mod(c8, ada_w[l], ada_b[l].reshape(1, -1))

    wi = w_in[l]
    blk = lambda j0: wi[:, j0:j0 + D]
    w_raster = jnp.stack([blk(0), blk(D), blk(5 * D), blk(6 * D + 32), blk(7 * D + 32)]).astype(BF16)
    w_qkv = jnp.stack([blk(2 * D), blk(3 * D), blk(4 * D)]).astype(BF16)
    w_all = jnp.stack([blk(0), blk(D), blk(2 * D), blk(3 * D), blk(4 * D)]).astype(BF16)
    w_ab = jnp.pad(wi[:, 6 * D:6 * D + 32], ((0, 0), (0, LANES - 32))).astype(BF16)
    pre_g = mix_pre_g[l].reshape(1, D)
    x2d = x.reshape(n_lat, D)
    x_col = _grid_transpose(x2d, b)
    tpb = s_lat // 1024
    o_lat, _ = _inproj(x2d, pre_g, mod, w_raster, w_ab, tm=1024, tiles_per_batch=tpb, row_offset=0)
    qkv_col, ab_col = _inproj(x_col, pre_g, mod, w_qkv, w_ab, tm=1024, tiles_per_batch=tpb, row_offset=0)
    o_ctx, ab_ctx = _inproj(ctx.reshape(b * t_ctx, D), pre_g, mod, w_all, w_ab, tm=b * t_ctx,
                            tiles_per_batch=1, row_offset=b)

    wblk = jnp.stack([_block_diag(rg_wa[l]), _block_diag(rg_wi[l])], axis=1).astype(BF16)
    rg_args = (rg_conv_w[l], rg_conv_b[l], wblk, rg_ba[l], rg_bi[l], rg_lam[l])
    coef_c = _rg_coef(o_ctx[0].reshape(b, t_ctx, D), *rg_args, tt=t_ctx, reset_first=True)
    coef_l = _rg_coef(o_lat[0].reshape(b, s_lat, D), *rg_args, tt=256, reset_first=False)
    hf, hb = _rg_scan(coef_c, coef_l, t_ctx=t_ctx, t_lat=s_lat, tt=256)

    alog_l = jnp.zeros((1, LANES), F32).at[0, 0:2 * HEADS].set(dn_a_log[l].reshape(-1))
    dtb_l = jnp.zeros((1, LANES), F32).at[0, 0:2 * HEADS].set(dn_dt_bias[l].reshape(-1))
    of, ob = _delta(qkv_col, ab_col, o_ctx, ab_ctx, dn_conv_w[l], alog_l, dtb_l, b=b, s_lat=s_lat, t_ctx=t_ctx)
    of = _grid_transpose(of, b)
    ob = _grid_transpose(ob, b)

    rw = jnp.pad(router_w[l], ((0, 0), (0, LANES - N_EXPERTS)))
    rb = jnp.pad(router_b[l].reshape(1, -1), ((0, 0), (0, LANES - N_EXPERTS)))
    x1, hf2, topi, topw, rank, cnt = _post(
        x2d, o_lat, hf, hb, of, ob, mod,
        rg_w_o[l].astype(BF16), dn_w_o[l].astype(BF16), w_out[l].astype(BF16),
        jnp.tile(dn_norm_g[l], HEADS).reshape(1, D), mix_post_g[l].reshape(1, D), ffn_pre_g[l].reshape(1, D),
        rw, rb, tm=256, s_lat=s_lat)

    counts = cnt[0, 0:N_EXPERTS]
    padded = ((counts + MOE_TM - 1) // MOE_TM) * MOE_TM
    ends = jnp.cumsum(padded)
    starts = ends - padded
    top_i = topi[:, 0:TOP_K]
    pos2d = (starts[top_i] + rank[:, 0:TOP_K]).reshape(n_lat // MOE_TB, MOE_TB * TOP_K)
    n_tiles = n_lat * TOP_K // MOE_TM + N_EXPERTS
    tile_start = jnp.arange(n_tiles, dtype=jnp.int32) * MOE_TM
    tile_expert = jnp.minimum(jnp.sum(tile_start[:, None] >= ends[None, :], axis=1), N_EXPERTS - 1).astype(jnp.int32)
    n_used = (ends[-1] // MOE_TM).astype(jnp.int32).reshape(1)

    xs = _dispatch(starts, ends, pos2d, hf2, n_tiles * MOE_TM, tb=MOE_TB)
    ys = _experts(tile_expert, n_used, xs, e_w1[l], e_b1[l].reshape(N_EXPERTS, 1, -1),
                  e_w2[l], e_b2[l].reshape(N_EXPERTS, 1, -1))
    out = _final(pos2d, ys, topw, x1, mod, ffn_post_g[l].reshape(1, D), tb=MOE_TB, s_lat=s_lat)
    return out.reshape(b, s_lat, D)
```

```python
import functools

import jax
import jax.numpy as jnp
from jax import lax
from jax.experimental import pallas as pl
from jax.experimental.pallas import tpu as pltpu

F32 = jnp.float32
BF16 = jnp.bfloat16
U32 = jnp.uint32
HIGHEST = lax.Precision.HIGHEST

D = 1024
EPS = 1e-6
GRID_W = 64
CHUNK = 64
HEADS = 8
DK = 128
RG_C = 8.0
RG_BW = 64
N_EXPERTS = 32
TOP_K = 4
SWIGLU_LIMIT = 7.0
SWIGLU_ALPHA = 1.702
LANES = 128
MOE_TM = 256
MOE_TB = 512
ROW_SUB = D // LANES
VMEM_LIMIT = 56 * 1024 * 1024


def _cparams(sem):
    return pltpu.CompilerParams(dimension_semantics=sem, vmem_limit_bytes=VMEM_LIMIT)


def _sigmoid(x):
    return 1.0 / (1.0 + jnp.exp(-x))


def _softplus(y):
    return jnp.maximum(y, 0.0) + jnp.log1p(jnp.exp(-jnp.abs(y)))


def _nt_dot(a, b):
    return lax.dot_general(a, b, (((1,), (1,)), ((), ())), preferred_element_type=F32)


def _ada_kernel(c_ref, w_ref, b_ref, o_ref):
    c = c_ref[...]
    a = c * _sigmoid(c)
    o_ref[...] = jnp.dot(a, w_ref[...], preferred_element_type=F32, precision=HIGHEST) + b_ref[...]


def _ada_mod(c8, ada_w, ada_b):
    n = ada_w.shape[1]
    tn = 1024
    return pl.pallas_call(
        _ada_kernel,
        grid=(n // tn,),
        in_specs=[pl.BlockSpec((8, D), lambda j: (0, 0)),
                  pl.BlockSpec((D, tn), lambda j: (0, j)),
                  pl.BlockSpec((1, tn), lambda j: (0, j))],
        out_specs=pl.BlockSpec((8, tn), lambda j: (0, j)),
        out_shape=jax.ShapeDtypeStruct((8, n), F32),
        compiler_params=_cparams(("arbitrary",)),
        name="ada_mod",
    )(c8, ada_w, ada_b)


def _inproj_kernel(x_ref, g_ref, mod_ref, w_ref, wab_ref, o_ref, ab_ref, h_scr, *, tiles_per_batch, row_offset):
    i = pl.program_id(0)
    j = pl.program_id(1)

    @pl.when(j == 0)
    def _():
        x = x_ref[...]
        y = x * lax.rsqrt(jnp.mean(x * x, axis=-1, keepdims=True) + EPS) * g_ref[...]
        row = row_offset + i // tiles_per_batch
        sh = mod_ref[pl.ds(row, 1), 0:D]
        sc = mod_ref[pl.ds(row, 1), D:2 * D]
        h = (y * (1.0 + sc) + sh).astype(BF16)
        h_scr[...] = h
        ab_ref[...] = jnp.dot(h, wab_ref[...], preferred_element_type=F32)

    o_ref[0] = jnp.dot(h_scr[...], w_ref[0], preferred_element_type=F32).astype(BF16)


def _inproj(x2d, g, mod, w_main, w_ab, *, tm, tiles_per_batch, row_offset):
    n = x2d.shape[0]
    nj = w_main.shape[0]
    kern = functools.partial(_inproj_kernel, tiles_per_batch=tiles_per_batch, row_offset=row_offset)
    return pl.pallas_call(
        kern,
        grid=(n // tm, nj),
        in_specs=[pl.BlockSpec((tm, D), lambda i, j: (i, 0)),
                  pl.BlockSpec((1, D), lambda i, j: (0, 0)),
                  pl.BlockSpec((8, 6 * D), lambda i, j: (0, 0)),
                  pl.BlockSpec((1, D, D), lambda i, j: (j, 0, 0)),
                  pl.BlockSpec((D, LANES), lambda i, j: (0, 0))],
        out_specs=[pl.BlockSpec((1, tm, D), lambda i, j: (j, i, 0)),
                   pl.BlockSpec((tm, LANES), lambda i, j: (i, 0))],
        out_shape=[jax.ShapeDtypeStruct((nj, n, D), BF16),
                   jax.ShapeDtypeStruct((n, LANES), F32)],
        scratch_shapes=[pltpu.VMEM((tm, D), BF16)],
        compiler_params=_cparams(("arbitrary", "arbitrary")),
        name="inproj",
    )(x2d, g, mod, w_main, w_ab)


RG_HALO = 16
SCAN_ROWS = 8
SCAN_SLABS = 4


def _pack_bf16_pair(hi, lo):
    hi_bits = lax.bitcast_convert_type(hi.astype(BF16).astype(F32), U32)
    lo_bits = lax.bitcast_convert_type(lo.astype(BF16).astype(F32), U32)
    return hi_bits | (lo_bits >> 16)


def _rg_coef_kernel(cur_ref, prev_ref, next_ref, cw_ref, cb_ref, wblk_ref, ba_ref, bi_ref, lam_ref,
                    wf_ref, wb_ref, *, tt, nt, reset_first):
    i = pl.program_id(0)
    b = pl.program_id(1)
    x = cur_ref[0].astype(F32)
    xp = jnp.where(i > 0, prev_ref[0].astype(F32), 0.0)
    xn = jnp.where(i < nt - 1, next_ref[0].astype(F32), 0.0)
    xe_f = jnp.concatenate([xp, x], axis=0)
    xe_b = jnp.concatenate([x, xn], axis=0)
    n_ext = tt + RG_HALO
    row = lax.broadcasted_iota(jnp.int32, (tt, 1), 0)
    outs = (wf_ref, wb_ref)
    for d in (0, 1):
        w = cw_ref[d]
        acc = x * w[3:4]
        for sft in (1, 2, 3):
            if d == 0:
                sh = pltpu.roll(xe_f, sft, axis=0)[RG_HALO:RG_HALO + tt]
            else:
                sh = pltpu.roll(xe_b, n_ext - sft, axis=0)[0:tt]
            acc = acc + sh * w[3 - sft:4 - sft]
        xc = acc + cb_ref[d:d + 1]
        xcb = xc.astype(BF16)
        pre_a = jnp.concatenate(
            [jnp.dot(xcb[:, g * 256:(g + 1) * 256], wblk_ref[d, 0, g], preferred_element_type=F32)
             for g in range(4)], axis=1) + ba_ref[d:d + 1]
        pre_i = jnp.concatenate(
            [jnp.dot(xcb[:, g * 256:(g + 1) * 256], wblk_ref[d, 1, g], preferred_element_type=F32)
             for g in range(4)], axis=1) + bi_ref[d:d + 1]
        r = _sigmoid(pre_a)
        ig = _sigmoid(pre_i)
        la = r * (-RG_C * _softplus(-lam_ref[d:d + 1]))
        a = jnp.exp(la)
        mult = jnp.sqrt(1.0 - a * a)
        if reset_first:
            if d == 0:
                first = (row == 0) & (i == 0)
            else:
                first = (row == tt - 1) & (i == nt - 1)
            mult = jnp.where(first, 1.0, mult)
        word = _pack_bf16_pair(la, mult * ig * xc)
        for half in (0, 1):
            for q in range(SCAN_SLABS):
                c0 = (half * SCAN_SLABS + q) * LANES
                outs[d][q, pl.ds(b * 2 + half, tt, stride=SCAN_ROWS), :] = word[:, c0:c0 + LANES]


def _rg_coef(xr, conv_w, conv_b, wblk, ba, bi, lam, *, tt, reset_first):
    b, t, _ = xr.shape
    nt = t // tt
    hb = tt // RG_HALO
    kern = functools.partial(_rg_coef_kernel, tt=tt, nt=nt, reset_first=reset_first)
    full = lambda shape: pl.BlockSpec(shape, lambda i, bb: (0,) * len(shape))
    out_sds = jax.ShapeDtypeStruct((SCAN_SLABS, t * SCAN_ROWS, LANES), U32)
    out_spec = pl.BlockSpec((SCAN_SLABS, tt * SCAN_ROWS, LANES), lambda i, bb: (0, i, 0))
    return pl.pallas_call(
        kern,
        grid=(nt, b),
        in_specs=[pl.BlockSpec((1, tt, D), lambda i, bb: (bb, i, 0)),
                  pl.BlockSpec((1, RG_HALO, D), lambda i, bb: (bb, jnp.maximum(i * hb - 1, 0), 0)),
                  pl.BlockSpec((1, RG_HALO, D), lambda i, bb: (bb, jnp.minimum((i + 1) * hb, t // RG_HALO - 1), 0)),
                  full((2, 4, D)), full((2, D)), full((2, 2, 4, 256, 256)),
                  full((2, D)), full((2, D)), full((2, D))],
        out_specs=[out_spec, out_spec],
        out_shape=[out_sds, out_sds],
        compiler_params=_cparams(("arbitrary", "arbitrary")),
        name="rg_coef",
    )(xr, xr, xr, conv_w, conv_b, wblk, ba, bi, lam)


def _scan_step(word, h):
    la = lax.bitcast_convert_type(word & jnp.uint32(0xFFFF0000), F32)
    bb = lax.bitcast_convert_type(word << 16, F32)
    return jnp.exp(la) * h + bb


def _rg_scan_run(wf, wb, hf, hb, carry, n):
    def body(t, c):
        h_f, h_b = c
        r0 = pl.multiple_of(t * SCAN_ROWS, SCAN_ROWS)
        r1 = pl.multiple_of((n - 1 - t) * SCAN_ROWS, SCAN_ROWS)
        new_f, new_b = [], []
        for q in range(SCAN_SLABS):
            f = _scan_step(wf[q, pl.ds(r0, SCAN_ROWS), :], h_f[q])
            g = _scan_step(wb[q, pl.ds(r1, SCAN_ROWS), :], h_b[q])
            if hf is not None:
                hf[q, pl.ds(r0, SCAN_ROWS), :] = f
                hb[q, pl.ds(r1, SCAN_ROWS), :] = g
            new_f.append(f)
            new_b.append(g)
        return tuple(new_f), tuple(new_b)

    init = (tuple(carry[0, q] for q in range(SCAN_SLABS)), tuple(carry[1, q] for q in range(SCAN_SLABS)))
    h_f, h_b = lax.fori_loop(0, n, body, init, unroll=8)
    for q in range(SCAN_SLABS):
        carry[0, q] = h_f[q]
        carry[1, q] = h_b[q]


def _rg_scan_kernel(wf_c, wb_c, wf_l, wb_l, hf_ref, hb_ref, carry, *, t_ctx, tt):
    s = pl.program_id(0)

    @pl.when(s == 0)
    def _():
        carry[...] = jnp.zeros(carry.shape, F32)
        _rg_scan_run(wf_c, wb_c, None, None, carry, t_ctx)

    @pl.when(s > 0)
    def _():
        _rg_scan_run(wf_l, wb_l, hf_ref, hb_ref, carry, tt)


def _rg_scan(coef_c, coef_l, *, t_ctx, t_lat, tt):
    nt = t_lat // tt
    ctx_spec = pl.BlockSpec((SCAN_SLABS, t_ctx * SCAN_ROWS, LANES), lambda s: (0, 0, 0))
    f_spec = pl.BlockSpec((SCAN_SLABS, tt * SCAN_ROWS, LANES), lambda s: (0, jnp.maximum(s - 1, 0), 0))
    b_spec = pl.BlockSpec((SCAN_SLABS, tt * SCAN_ROWS, LANES), lambda s: (0, nt - jnp.maximum(s, 1), 0))
    out_sds = jax.ShapeDtypeStruct((SCAN_SLABS, t_lat * SCAN_ROWS, LANES), F32)
    kern = functools.partial(_rg_scan_kernel, t_ctx=t_ctx, tt=tt)
    return pl.pallas_call(
        kern,
        grid=(nt + 1,),
        in_specs=[ctx_spec, ctx_spec, f_spec, b_spec],
        out_specs=[f_spec, b_spec],
        out_shape=[out_sds, out_sds],
        scratch_shapes=[pltpu.VMEM((2, SCAN_SLABS, SCAN_ROWS, LANES), F32)],
        compiler_params=_cparams(("arbitrary",)),
        name="rg_scan",
    )(coef_c[0], coef_c[1], coef_l[0], coef_l[1])


N_CTX_CHUNKS = 4
CONV_SLABS = 3 * D // LANES
CHAIN_GROUP = 16
NEUMANN_STEPS = 6


def _delta_prep(d, is_ctx, refs_l, refs_c, prevraw, cw_ref, alog_ref, dtb_ref):
    q_l, k_l, v_l, ab_l = refs_l
    q_c, k_c, v_c, ab_c = refs_c

    def sel(rc, rl):
        return jnp.where(is_ctx, rc[0], rl[0])

    raw = jnp.concatenate([sel(q_c, q_l), sel(k_c, k_l), sel(v_c, v_l)], axis=1).astype(F32)
    abv = jnp.where(is_ctx, ab_c[...], ab_l[...])
    base = 8 if d == 0 else 0
    for j in range(CONV_SLABS):
        prevraw[d, j, base:base + CHUNK, :] = raw[:, j * LANES:(j + 1) * LANES]
    w = cw_ref[d]
    outs = []
    for j in range(CONV_SLABS):
        wj = w[:, j * LANES:(j + 1) * LANES]
        acc = raw[:, j * LANES:(j + 1) * LANES] * wj[3:4]
        for sft in (1, 2, 3):
            off = base - sft if d == 0 else sft
            acc = acc + prevraw[d, j, off:off + CHUNK, :] * wj[3 - sft:4 - sft]
        outs.append(acc)
    acc = jnp.concatenate(outs, axis=1)
    if d == 0:
        prevraw[d, :, 0:8, :] = prevraw[d, :, CHUNK:CHUNK + 8, :]
    else:
        prevraw[d, :, CHUNK:CHUNK + 8, :] = prevraw[d, :, 0:8, :]
    y = acc * _sigmoid(acc)

    bi = lax.broadcasted_iota(jnp.int32, (2 * DK, 2 * DK), 0) // DK
    bj = lax.broadcasted_iota(jnp.int32, (2 * DK, 2 * DK), 1) // DK
    ones_blk = (bi == bj).astype(BF16)
    qk_sq = (y[:, 0:2 * D] * y[:, 0:2 * D]).astype(BF16)
    ss = jnp.concatenate(
        [jnp.dot(qk_sq[:, g * 2 * DK:(g + 1) * 2 * DK], ones_blk, preferred_element_type=F32)
         for g in range(2 * D // (2 * DK))], axis=1)
    inv = lax.rsqrt(ss + EPS)
    qn_all = y[:, 0:D] * inv[:, 0:D] * (DK ** -0.5)
    kn_all = y[:, D:2 * D] * inv[:, D:2 * D]
    y = jnp.concatenate([qn_all, kn_all, y[:, 2 * D:3 * D]], axis=1)

    rp = lax.broadcasted_iota(jnp.int32, (2 * CHUNK, CHUNK), 0)
    cp = lax.broadcasted_iota(jnp.int32, (2 * CHUNK, CHUNK), 1)
    if d == 0:
        m_pad = ((rp >= cp) & (rp < CHUNK)).astype(F32)
    else:
        m_pad = ((rp <= cp) & (rp < CHUNK)).astype(F32)
    g_all = -jnp.exp(alog_ref[...]) * _softplus(abv + dtb_ref[...])
    beta_all = _sigmoid(abv)
    gcum = jnp.dot(m_pad, g_all, preferred_element_type=F32, precision=HIGHEST)
    return y, gcum, gcum.T, beta_all


def _delta_heads(preps, out_refs, state, ybuf):
    ri = lax.broadcasted_iota(jnp.int32, (CHUNK, CHUNK), 0)
    ci = lax.broadcasted_iota(jnp.int32, (CHUNK, CHUNK), 1)
    incl = (ri >= ci, ri <= ci)
    strict = (ri > ci, ri < ci)
    last = (CHUNK - 1, 0)
    eye = (ri == ci).astype(F32)
    keep = lax.broadcasted_iota(jnp.int32, (CHUNK, 2 * CHUNK), 1) >= CHUNK
    zeros_half = jnp.zeros((CHUNK, DK), F32)
    for d in (0, 1):
        ybuf[d] = preps[d][0]
    gcum = (preps[0][1], preps[1][1])
    gcum_t = (preps[0][2], preps[1][2])
    beta_all = (preps[0][3], preps[1][3])
    q_of = lambda c: ybuf[c[0], :, c[1] * DK:(c[1] + 1) * DK]
    k_of = lambda c: ybuf[c[0], :, D + c[1] * DK:D + (c[1] + 1) * DK]
    v_of = lambda c: ybuf[c[0], :, 2 * D + c[1] * DK:2 * D + (c[1] + 1) * DK]
    lane_of = lambda c: c[0] * HEADS + c[1]
    gcol_of = lambda c: gcum[c[0]][0:CHUNK, lane_of(c):lane_of(c) + 1]
    bcol_of = lambda c: beta_all[c[0]][:, 2 * HEADS + lane_of(c):2 * HEADS + lane_of(c) + 1]

    all_chains = [(d, h) for h in range(HEADS) for d in (0, 1)]
    for g0 in range(0, len(all_chains), CHAIN_GROUP):
        chains = all_chains[g0:g0 + CHAIN_GROUP]
        qmat, qkd, sol, wqh, v_new = {}, {}, {}, {}, {}
        for c in chains:
            d = c[0]
            grow = gcum_t[d][lane_of(c):lane_of(c) + 1, 0:CHUNK]
            kn = k_of(c)
            knb = kn.astype(BF16)
            a_mat = _nt_dot((kn * bcol_of(c)).astype(BF16), knb)
            qk = _nt_dot(q_of(c).astype(BF16), knb)
            dec = jnp.where(incl[d], jnp.exp(jnp.where(incl[d], gcol_of(c) - grow, 0.0)), 0.0)
            s0 = jnp.where(strict[d], -a_mat * dec, 0.0)
            qmat[c] = jnp.concatenate([s0, eye], axis=1)
            qkd[c] = (qk * dec).astype(BF16)

        for _ in range(NEUMANN_STEPS):
            for c in chains:
                qm = qmat[c]
                qmat[c] = jnp.dot(qm[:, 0:CHUNK].astype(BF16), qm.astype(BF16),
                                  preferred_element_type=F32) + jnp.where(keep, qm, 0.0)

        for c in chains:
            bcol = bcol_of(c)
            rhs = jnp.concatenate([v_of(c) * bcol, k_of(c) * (bcol * jnp.exp(gcol_of(c)))], axis=1).astype(BF16)
            sol[c] = jnp.dot(qmat[c][:, CHUNK:2 * CHUNK].astype(BF16), rhs, preferred_element_type=F32)
        for c in chains:
            wq = jnp.concatenate([sol[c][:, DK:2 * DK], q_of(c) * jnp.exp(gcol_of(c))], axis=0)
            wqh[c] = jnp.dot(wq.astype(BF16), state[c[0], c[1]].astype(BF16), preferred_element_type=F32)
        for c in chains:
            v_new[c] = sol[c][:, 0:DK] - wqh[c][0:CHUNK]
        for c in chains:
            o = wqh[c][CHUNK:2 * CHUNK] + jnp.dot(qkd[c], v_new[c].astype(BF16), preferred_element_type=F32)
            out_refs[c[0]][:, c[1] * DK:(c[1] + 1) * DK] = o.astype(BF16)
        for c in chains:
            d = c[0]
            gl = gcum[d][last[d]:last[d] + 1, lane_of(c):lane_of(c) + 1]
            kd = k_of(c) * jnp.exp(gl - gcol_of(c))
            kd_t = jnp.concatenate([kd, zeros_half], axis=0).T
            v_pad = jnp.concatenate([v_new[c], zeros_half], axis=0)
            state[c[0], c[1]] = state[c[0], c[1]] * jnp.exp(gl) + jnp.dot(
                kd_t.astype(BF16), v_pad.astype(BF16), preferred_element_type=F32)


def _delta_kernel(qf_l, kf_l, vf_l, abf_l, qb_l, kb_l, vb_l, abb_l,
                  qf_c, kf_c, vf_c, abf_c, qb_c, kb_c, vb_c, abb_c,
                  cw_ref, alog_ref, dtb_ref, of_ref, ob_ref, state, prevraw, ybuf):
    s = pl.program_id(1)
    is_ctx = s < N_CTX_CHUNKS

    @pl.when(s == 0)
    def _():
        state[...] = jnp.zeros(state.shape, F32)

    @pl.when((s == 0) | (s == N_CTX_CHUNKS))
    def _():
        prevraw[...] = jnp.zeros(prevraw.shape, F32)

    pf = _delta_prep(0, is_ctx, (qf_l, kf_l, vf_l, abf_l), (qf_c, kf_c, vf_c, abf_c), prevraw,
                     cw_ref, alog_ref, dtb_ref)
    pb = _delta_prep(1, is_ctx, (qb_l, kb_l, vb_l, abb_l), (qb_c, kb_c, vb_c, abb_c), prevraw,
                     cw_ref, alog_ref, dtb_ref)
    _delta_heads((pf, pb), (of_ref, ob_ref), state, ybuf)


def _delta(qkv_col, ab_col, o_ctx, ab_ctx, conv_w, alog_l, dtb_l, *, b, s_lat, t_ctx):
    ncol = s_lat // CHUNK
    assert s_lat // GRID_W == CHUNK and t_ctx == N_CTX_CHUNKS * CHUNK

    col_f = lambda bb, s: bb * ncol + jnp.maximum(s - N_CTX_CHUNKS, 0)
    col_b = lambda bb, s: bb * ncol + ncol - 1 - jnp.maximum(s - N_CTX_CHUNKS, 0)
    cch_f = lambda bb, s: bb * N_CTX_CHUNKS + jnp.minimum(s, N_CTX_CHUNKS - 1)
    cch_b = lambda bb, s: bb * N_CTX_CHUNKS + N_CTX_CHUNKS - 1 - jnp.minimum(s, N_CTX_CHUNKS - 1)

    def tok_spec(j, fn):
        return pl.BlockSpec((1, CHUNK, D), lambda bb, s: (j, fn(bb, s), 0))

    ab_spec = lambda fn: pl.BlockSpec((CHUNK, LANES), lambda bb, s: (fn(bb, s), 0))
    full = lambda shape: pl.BlockSpec(shape, lambda bb, s: (0,) * len(shape))

    in_specs = ([tok_spec(0, col_f), tok_spec(1, col_f), tok_spec(2, col_f), ab_spec(col_f),
                 tok_spec(0, col_b), tok_spec(1, col_b), tok_spec(2, col_b), ab_spec(col_b),
                 tok_spec(2, cch_f), tok_spec(3, cch_f), tok_spec(4, cch_f), ab_spec(cch_f),
                 tok_spec(2, cch_b), tok_spec(3, cch_b), tok_spec(4, cch_b), ab_spec(cch_b),
                 full((2, 4, 3 * D)), full((1, LANES)), full((1, LANES))])
    out_sds = jax.ShapeDtypeStruct((b * s_lat, D), BF16)
    out_specs = [pl.BlockSpec((CHUNK, D), lambda bb, s: (col_f(bb, s), 0)),
                 pl.BlockSpec((CHUNK, D), lambda bb, s: (col_b(bb, s), 0))]
    return pl.pallas_call(
        _delta_kernel,
        grid=(b, N_CTX_CHUNKS + ncol),
        in_specs=in_specs,
        out_specs=out_specs,
        out_shape=[out_sds, out_sds],
        scratch_shapes=[pltpu.VMEM((2, HEADS, DK, DK), F32), pltpu.VMEM((2, CONV_SLABS, CHUNK + 8, LANES), F32),
                        pltpu.VMEM((2, CHUNK, 3 * D), F32)],
        compiler_params=_cparams(("arbitrary", "arbitrary")),
        name="delta",
    )(qkv_col, qkv_col, qkv_col, ab_col, qkv_col, qkv_col, qkv_col, ab_col,
      o_ctx, o_ctx, o_ctx, ab_ctx, o_ctx, o_ctx, o_ctx, ab_ctx,
      conv_w, alog_l, dtb_l)


def _rms(x, g):
    return x * lax.rsqrt(jnp.mean(x * x, axis=-1, keepdims=True) + EPS) * g


def _store_row_tiles(ref, val, n):
    for s in range(ROW_SUB):
        ref[pl.ds(s, n, stride=ROW_SUB), :] = val[:, s * LANES:(s + 1) * LANES]


def _load_row_tiles(ref, n):
    return jnp.concatenate([ref[pl.ds(s, n, stride=ROW_SUB), :] for s in range(ROW_SUB)], axis=1)


def _scan_rows(ref, b, tm):
    return jnp.concatenate(
        [ref[q, pl.ds(b * 2 + half, tm, stride=SCAN_ROWS), :] for half in (0, 1) for q in range(SCAN_SLABS)],
        axis=1)


def _post_kernel(x_ref, gate_ref, z_ref, grg_ref, gdn_ref, hf_ref, hb_ref, of_ref, ob_ref, mod_ref,
                 wrg_ref, wdn_ref, wout_ref, dng_ref, postg_ref, preg_ref, rw_ref, rb_ref,
                 x1_ref, hf2_ref, topi_ref, topw_ref, rank_ref, cnt_ref, carry, *, tm):
    b = pl.program_id(1)

    @pl.when((pl.program_id(0) == 0) & (b == 0))
    def _():
        carry[...] = jnp.zeros(carry.shape, F32)

    gate = gate_ref[0].astype(F32)
    gelu = 0.5 * gate * (1.0 + jnp.tanh(0.7978845608028654 * (gate + 0.044715 * gate * gate * gate)))
    rg_h = _scan_rows(hf_ref, b, tm) + _scan_rows(hb_ref, b, tm)
    y_rg = jnp.dot((rg_h * gelu).astype(BF16), wrg_ref[...], preferred_element_type=F32)

    dn = of_ref[...].astype(F32) + ob_ref[...].astype(F32)
    z = z_ref[0].astype(F32)
    parts = []
    for h in range(HEADS):
        seg = dn[:, h * DK:(h + 1) * DK]
        parts.append(seg * lax.rsqrt(jnp.mean(seg * seg, axis=-1, keepdims=True) + EPS))
    o = jnp.concatenate(parts, axis=1) * dng_ref[...] * (z * _sigmoid(z))
    y_dn = jnp.dot(o.astype(BF16), wdn_ref[...], preferred_element_type=F32)

    merged = _sigmoid(grg_ref[0].astype(F32)) * y_rg + _sigmoid(gdn_ref[0].astype(F32)) * y_dn
    y = jnp.dot(merged.astype(BF16), wout_ref[...], preferred_element_type=F32)

    g1 = mod_ref[pl.ds(b, 1), 2 * D:3 * D]
    sh2 = mod_ref[pl.ds(b, 1), 3 * D:4 * D]
    sc2 = mod_ref[pl.ds(b, 1), 4 * D:5 * D]
    x1 = x_ref[...] + g1 * _rms(y, postg_ref[...])
    x1_ref[...] = x1
    hf2 = _rms(x1, preg_ref[...]) * (1.0 + sc2) + sh2
    _store_row_tiles(hf2_ref, hf2, tm)

    lane = lax.broadcasted_iota(jnp.int32, (tm, LANES), 1)
    lane_f = lane.astype(F32)
    logits = jnp.dot(hf2, rw_ref[...], preferred_element_type=F32, precision=HIGHEST) + rb_ref[...]
    neg = jnp.float32(-jnp.inf)
    lg = jnp.where(lane < N_EXPERTS, logits, neg)
    vals, sels = [], []
    topi = jnp.zeros((tm, LANES), F32)
    onehot = jnp.zeros((tm, LANES), F32)
    for k in range(TOP_K):
        m = jnp.max(lg, axis=-1, keepdims=True)
        idx = jnp.min(jnp.where(lg == m, lane_f, float(LANES)), axis=-1, keepdims=True)
        sel = lane_f == idx
        vals.append(m)
        sels.append(sel)
        lg = jnp.where(sel, neg, lg)
        onehot = onehot + sel.astype(F32)
        topi = topi + jnp.where(lane == k, idx, 0.0)
    es = [jnp.exp(v - vals[0]) for v in vals]
    den = es[0] + es[1] + es[2] + es[3]
    topw = jnp.zeros((tm, LANES), F32)
    for k in range(TOP_K):
        topw = topw + jnp.where(lane == k, es[k] / den, 0.0)

    ri = lax.broadcasted_iota(jnp.int32, (tm, tm), 0)
    ci = lax.broadcasted_iota(jnp.int32, (tm, tm), 1)
    tri = (ri > ci).astype(BF16)
    cum = jnp.dot(tri, onehot.astype(BF16), preferred_element_type=F32) + carry[...]
    rank = jnp.zeros((tm, LANES), F32)
    for k in range(TOP_K):
        rk = jnp.sum(jnp.where(sels[k], cum, 0.0), axis=-1, keepdims=True)
        rank = rank + jnp.where(lane == k, rk, 0.0)
    new_carry = carry[...] + jnp.sum(onehot, axis=0, keepdims=True)
    carry[...] = new_carry
    topi_ref[...] = topi.astype(jnp.int32)
    topw_ref[...] = topw
    rank_ref[...] = rank.astype(jnp.int32)
    cnt_ref[...] = jnp.broadcast_to(new_carry, (8, LANES)).astype(jnp.int32)


def _post(x2d, o_lat, hf, hb, of, ob, mod, wrg, wdn, wout, dng, postg, preg, rw, rb, *, tm, s_lat):
    n = x2d.shape[0]
    tpb = s_lat // tm
    nb = n // s_lat
    kern = functools.partial(_post_kernel, tm=tm)
    tok = pl.BlockSpec((tm, D), lambda t, b: (b * tpb + t, 0))
    oj = lambda j: pl.BlockSpec((1, tm, D), lambda t, b: (j, b * tpb + t, 0))
    hspec = pl.BlockSpec((SCAN_SLABS, tm * SCAN_ROWS, LANES), lambda t, b: (0, t, 0))
    full = lambda shape: pl.BlockSpec(shape, lambda t, b: (0,) * len(shape))
    lane_out = pl.BlockSpec((tm, LANES), lambda t, b: (b * tpb + t, 0))
    return pl.pallas_call(
        kern,
        grid=(tpb, nb),
        in_specs=[tok, oj(1), oj(2), oj(3), oj(4), hspec, hspec, tok, tok, full((8, 6 * D)),
                  full((D, D)), full((D, D)), full((D, D)), full((1, D)), full((1, D)), full((1, D)),
                  full((D, LANES)), full((1, LANES))],
        out_specs=[tok, pl.BlockSpec((tm * ROW_SUB, LANES), lambda t, b: (b * tpb + t, 0)), lane_out, lane_out,
                   lane_out, pl.BlockSpec((8, LANES), lambda t, b: (0, 0))],
        out_shape=[jax.ShapeDtypeStruct((n, D), F32), jax.ShapeDtypeStruct((n * ROW_SUB, LANES), F32),
                   jax.ShapeDtypeStruct((n, LANES), jnp.int32), jax.ShapeDtypeStruct((n, LANES), F32),
                   jax.ShapeDtypeStruct((n, LANES), jnp.int32), jax.ShapeDtypeStruct((8, LANES), jnp.int32)],
        scratch_shapes=[pltpu.VMEM((1, LANES), F32)],
        compiler_params=_cparams(("arbitrary", "arbitrary")),
        name="post_mix",
    )(x2d, o_lat, o_lat, o_lat, o_lat, hf, hb, of, ob, mod, wrg, wdn, wout, dng, postg, preg, rw, rb)


def _idx_copy(pos_hbm, idx, isem, tile, slot):
    n_idx = pos_hbm.shape[1]
    return pltpu.make_async_copy(pos_hbm.at[tile], idx.at[pl.ds(pl.multiple_of(slot * n_idx, n_idx), n_idx)],
                                 isem.at[slot])


def _row_dma_loop(idx, slot, tb, issue):
    base = slot * (tb * TOP_K)

    def body(rt, c):
        r0 = pl.multiple_of(rt * 8, 8)
        i0 = base + rt * (8 * TOP_K)
        for j in range(8):
            for k in range(TOP_K):
                issue(r0 + j, k, idx[i0 + j * TOP_K + k])
        return c

    lax.fori_loop(0, tb // 8, body, 0)


def _dispatch_kernel(starts_ref, ends_ref, pos_hbm, x_ref, xs_hbm, idx, zbuf, sem, isem, *, tb):
    i = pl.program_id(0)
    n = pl.num_programs(0)
    slot = i % 2

    @pl.when(i == 0)
    def _():
        _idx_copy(pos_hbm, idx, isem, 0, 0).start()
        zbuf[...] = jnp.zeros(zbuf.shape, F32)

        def zero_copy(e):
            r0 = pl.multiple_of(ends_ref[e] - MOE_TM, MOE_TM)
            return pltpu.make_async_copy(zbuf, xs_hbm.at[pl.ds(r0, MOE_TM)], sem)

        for e in range(N_EXPERTS):
            @pl.when(ends_ref[e] > starts_ref[e])
            def _():
                zero_copy(e).start()
        for e in range(N_EXPERTS):
            @pl.when(ends_ref[e] > starts_ref[e])
            def _():
                zero_copy(e).wait()

        def tail_copy(t):
            return pltpu.make_async_copy(zbuf, xs_hbm.at[pl.ds(pl.multiple_of(t * MOE_TM, MOE_TM), MOE_TM)], sem)

        n_used = ends_ref[N_EXPERTS - 1] // MOE_TM
        n_tiles = xs_hbm.shape[0] // MOE_TM

        def tail_start(t, c):
            tail_copy(t).start()
            return c

        def tail_wait(t, c):
            tail_copy(t).wait()
            return c

        lax.fori_loop(n_used, n_tiles, tail_start, 0)
        lax.fori_loop(n_used, n_tiles, tail_wait, 0)

    @pl.when(i + 1 < n)
    def _():
        _idx_copy(pos_hbm, idx, isem, i + 1, 1 - slot).start()

    _idx_copy(pos_hbm, idx, isem, i, slot).wait()

    def issue(r, k, p):
        pltpu.make_async_copy(x_ref.at[r], xs_hbm.at[p], sem).start(priority=k % 2)

    _row_dma_loop(idx, slot, tb, issue)
    for k in range(TOP_K):
        pltpu.make_async_copy(x_ref, xs_hbm.at[pl.ds(0, tb)], sem).wait()


def _dispatch(starts, ends, pos2d, xp, n_rows, *, tb):
    n = xp.shape[0]
    grid_spec = pltpu.PrefetchScalarGridSpec(
        num_scalar_prefetch=2,
        grid=(n // tb,),
        in_specs=[pl.BlockSpec(memory_space=pl.ANY),
                  pl.BlockSpec((tb, ROW_SUB, LANES), lambda i, st, en: (i, 0, 0))],
        out_specs=pl.BlockSpec(memory_space=pl.ANY),
        scratch_shapes=[pltpu.SMEM((2 * tb * TOP_K,), jnp.int32), pltpu.VMEM((MOE_TM, ROW_SUB, LANES), F32),
                        pltpu.SemaphoreType.DMA, pltpu.SemaphoreType.DMA((2,))],
    )
    return pl.pallas_call(
        functools.partial(_dispatch_kernel, tb=tb),
        grid_spec=grid_spec,
        out_shape=jax.ShapeDtypeStruct((n_rows, ROW_SUB, LANES), F32),
        compiler_params=_cparams(("arbitrary",)),
        name="dispatch",
    )(starts, ends, pos2d, xp)


def _expert_kernel(te_ref, nu_ref, xs_ref, w1_ref, b1_ref, w2_ref, b2_ref, ys_ref, w1b, w2b):
    i = pl.program_id(0)

    @pl.when(i < nu_ref[0])
    def _():
        changed = (i == 0) | (te_ref[i] != te_ref[jnp.maximum(i - 1, 0)])

        @pl.when(changed)
        def _():
            w1b[...] = w1_ref[0].astype(BF16)
            w2b[...] = w2_ref[0].astype(BF16)

        x = _load_row_tiles(xs_ref, MOE_TM).astype(BF16)
        hh = jnp.dot(x, w1b[...], preferred_element_type=F32) + b1_ref[0]
        glu = jnp.minimum(hh[:, 0:D], SWIGLU_LIMIT)
        lin = jnp.clip(hh[:, D:2 * D], -SWIGLU_LIMIT, SWIGLU_LIMIT)
        act = glu * _sigmoid(SWIGLU_ALPHA * glu) * (lin + 1.0)
        out = jnp.dot(act.astype(BF16), w2b[...], preferred_element_type=F32) + b2_ref[0]
        _store_row_tiles(ys_ref, out, MOE_TM)

    @pl.when(i >= nu_ref[0])
    def _():
        ys_ref[...] = jnp.zeros(ys_ref.shape, F32)


def _experts(tile_expert, n_used, xs, w1, b1, w2, b2):
    n_rows = xs.shape[0] // ROW_SUB
    n_tiles = n_rows // MOE_TM
    row_map = lambda i, te, nu: (jnp.minimum(i, nu[0] - 1), 0)
    grid_spec = pltpu.PrefetchScalarGridSpec(
        num_scalar_prefetch=2,
        grid=(n_tiles,),
        in_specs=[pl.BlockSpec((MOE_TM * ROW_SUB, LANES), row_map),
                  pl.BlockSpec((1, D, 2 * D), lambda i, te, nu: (te[i], 0, 0)),
                  pl.BlockSpec((1, 1, 2 * D), lambda i, te, nu: (te[i], 0, 0)),
                  pl.BlockSpec((1, D, D), lambda i, te, nu: (te[i], 0, 0)),
                  pl.BlockSpec((1, 1, D), lambda i, te, nu: (te[i], 0, 0))],
        out_specs=pl.BlockSpec((MOE_TM * ROW_SUB, LANES), lambda i, te, nu: (i, 0)),
        scratch_shapes=[pltpu.VMEM((D, 2 * D), BF16), pltpu.VMEM((D, D), BF16)],
    )
    return pl.pallas_call(
        _expert_kernel,
        grid_spec=grid_spec,
        out_shape=jax.ShapeDtypeStruct((n_rows * ROW_SUB, LANES), F32),
        compiler_params=_cparams(("arbitrary",)),
        name="experts",
    )(tile_expert, n_used, xs, w1, b1, w2, b2)


def _final_kernel(pos_hbm, ys_hbm, topw_ref, x1_ref, mod_ref, postg_ref, o_ref, idx, buf, sem, isem,
                  *, tb, tiles_per_batch):
    i = pl.program_id(0)
    n = pl.num_programs(0)
    slot = i % 2

    @pl.when(i == 0)
    def _():
        _idx_copy(pos_hbm, idx, isem, 0, 0).start()

    @pl.when(i + 1 < n)
    def _():
        _idx_copy(pos_hbm, idx, isem, i + 1, 1 - slot).start()

    _idx_copy(pos_hbm, idx, isem, i, slot).wait()

    def issue(r, k, p):
        src = ys_hbm.at[pl.ds(pl.multiple_of(p * ROW_SUB, ROW_SUB), ROW_SUB)]
        dst = buf.at[k, pl.ds(pl.multiple_of(r * ROW_SUB, ROW_SUB), ROW_SUB)]
        pltpu.make_async_copy(src, dst, sem).start(priority=k % 2)

    _row_dma_loop(idx, slot, tb, issue)
    for k in range(TOP_K):
        pltpu.make_async_copy(ys_hbm.at[pl.ds(0, tb * ROW_SUB)], buf.at[k], sem).wait()

    row = i // tiles_per_batch
    tw = topw_ref[...]
    f = _load_row_tiles(buf.at[0], tb) * tw[:, 0:1]
    for k in range(1, TOP_K):
        f = f + _load_row_tiles(buf.at[k], tb) * tw[:, k:k + 1]
    g2 = mod_ref[pl.ds(row, 1), 5 * D:6 * D]
    o_ref[...] = x1_ref[...] + g2 * _rms(f, postg_ref[...])


def _final(pos2d, ys, topw, x1, mod, postg, *, tb, s_lat):
    n = x1.shape[0]
    kern = functools.partial(_final_kernel, tb=tb, tiles_per_batch=s_lat // tb)
    return pl.pallas_call(
        kern,
        grid=(n // tb,),
        in_specs=[pl.BlockSpec(memory_space=pl.ANY),
                  pl.BlockSpec(memory_space=pl.ANY),
                  pl.BlockSpec((tb, LANES), lambda i: (i, 0)),
                  pl.BlockSpec((tb, D), lambda i: (i, 0)),
                  pl.BlockSpec((8, 6 * D), lambda i: (0, 0)),
                  pl.BlockSpec((1, D), lambda i: (0, 0))],
        out_specs=pl.BlockSpec((tb, D), lambda i: (i, 0)),
        out_shape=jax.ShapeDtypeStruct((n, D), F32),
        scratch_shapes=[pltpu.SMEM((2 * tb * TOP_K,), jnp.int32), pltpu.VMEM((TOP_K, tb * ROW_SUB, LANES), F32),
                        pltpu.SemaphoreType.DMA, pltpu.SemaphoreType.DMA((2,))],
        compiler_params=_cparams(("arbitrary",)),
        name="final",
    )(pos2d, ys, topw, x1, mod, postg)


def _block_diag(w):
    w = w.reshape(2, 4, 4, RG_BW, RG_BW)
    eye = jnp.eye(4, dtype=w.dtype)
    return jnp.einsum('dgiab,ij->dgiajb', w, eye).reshape(2, 4, 4 * RG_BW, 4 * RG_BW)


def _grid_transpose(t, b):
    return t.reshape(b, GRID_W, GRID_W, -1).transpose(0, 2, 1, 3).reshape(t.shape)


def kernel(x, c, ctx, c_ctx, ada_w, ada_b, mix_pre_g, mix_post_g, w_in, rg_conv_w, rg_conv_b, rg_wa, rg_ba,
           rg_wi, rg_bi, rg_lam, rg_w_o, dn_conv_w, dn_a_log, dn_dt_bias, dn_norm_g, dn_w_o, w_out,
           ffn_pre_g, ffn_post_g, router_w, router_b, e_w1, e_b1, e_w2, e_b2):
    b, s_lat, _ = x.shape
    t_ctx = ctx.shape[1]
    depth = ada_w.shape[0]
    assert depth == 1 and b * 2 == SCAN_ROWS and s_lat == GRID_W * GRID_W
    n_lat = b * s_lat
    l = 0

    c8 = jnp.zeros((8, D), F32).at[0:b].set(c).at[b].set(c_ctx)
    mod = _ada_mod(c8, ada_w[l], ada_b[l].reshape(1, -1))

    wi = w_in[l]
    blk = lambda j0: wi[:, j0:j0 + D]
    w_raster = jnp.stack([blk(0), blk(D), blk(5 * D), blk(6 * D + 32), blk(7 * D + 32)]).astype(BF16)
    w_qkv = jnp.stack([blk(2 * D), blk(3 * D), blk(4 * D)]).astype(BF16)
    w_all = jnp.stack([blk(0), blk(D), blk(2 * D), blk(3 * D), blk(4 * D)]).astype(BF16)
    w_ab = jnp.pad(wi[:, 6 * D:6 * D + 32], ((0, 0), (0, LANES - 32))).astype(BF16)
    pre_g = mix_pre_g[l].reshape(1, D)
    x2d = x.reshape(n_lat, D)
    x_col = _grid_transpose(x2d, b)
    tpb = s_lat // 1024
    o_lat, _ = _inproj(x2d, pre_g, mod, w_raster, w_ab, tm=1024, tiles_per_batch=tpb, row_offset=0)
    qkv_col, ab_col = _inproj(x_col, pre_g, mod, w_qkv, w_ab, tm=1024, tiles_per_batch=tpb, row_offset=0)
    o_ctx, ab_ctx = _inproj(ctx.reshape(b * t_ctx, D), pre_g, mod, w_all, w_ab, tm=b * t_ctx,
                            tiles_per_batch=1, row_offset=b)

    wblk = jnp.stack([_block_diag(rg_wa[l]), _block_diag(rg_wi[l])], axis=1).astype(BF16)
    rg_args = (rg_conv_w[l], rg_conv_b[l], wblk, rg_ba[l], rg_bi[l], rg_lam[l])
    coef_c = _rg_coef(o_ctx[0].reshape(b, t_ctx, D), *rg_args, tt=t_ctx, reset_first=True)
    coef_l = _rg_coef(o_lat[0].reshape(b, s_lat, D), *rg_args, tt=256, reset_first=False)
    hf, hb = _rg_scan(coef_c, coef_l, t_ctx=t_ctx, t_lat=s_lat, tt=256)

    alog_l = jnp.zeros((1, LANES), F32).at[0, 0:2 * HEADS].set(dn_a_log[l].reshape(-1))
    dtb_l = jnp.zeros((1, LANES), F32).at[0, 0:2 * HEADS].set(dn_dt_bias[l].reshape(-1))
    of, ob = _delta(qkv_col, ab_col, o_ctx, ab_ctx, dn_conv_w[l], alog_l, dtb_l, b=b, s_lat=s_lat, t_ctx=t_ctx)
    of = _grid_transpose(of, b)
    ob = _grid_transpose(ob, b)

    rw = jnp.pad(router_w[l], ((0, 0), (0, LANES - N_EXPERTS)))
    rb = jnp.pad(router_b[l].reshape(1, -1), ((0, 0), (0, LANES - N_EXPERTS)))
    x1, hf2, topi, topw, rank, cnt = _post(
        x2d, o_lat, hf, hb, of, ob, mod,
        rg_w_o[l].astype(BF16), dn_w_o[l].astype(BF16), w_out[l].astype(BF16),
        jnp.tile(dn_norm_g[l], HEADS).reshape(1, D), mix_post_g[l].reshape(1, D), ffn_pre_g[l].reshape(1, D),
        rw, rb, tm=256, s_lat=s_lat)

    counts = cnt[0, 0:N_EXPERTS]
    padded = ((counts + MOE_TM - 1) // MOE_TM) * MOE_TM
    ends = jnp.cumsum(padded)
    starts = ends - padded
    top_i = topi[:, 0:TOP_K]
    pos2d = (starts[top_i] + rank[:, 0:TOP_K]).reshape(n_lat // MOE_TB, MOE_TB * TOP_K)
    n_tiles = n_lat * TOP_K // MOE_TM + N_EXPERTS
    tile_start = jnp.arange(n_tiles, dtype=jnp.int32) * MOE_TM
    tile_expert = jnp.minimum(jnp.sum(tile_start[:, None] >= ends[None, :], axis=1), N_EXPERTS - 1).astype(jnp.int32)
    n_used = (ends[-1] // MOE_TM).astype(jnp.int32).reshape(1)

    xs = _dispatch(starts, ends, pos2d, hf2.reshape(n_lat, ROW_SUB, LANES), n_tiles * MOE_TM, tb=MOE_TB)
    xs = xs.reshape(n_tiles * MOE_TM * ROW_SUB, LANES)
    ys = _experts(tile_expert, n_used, xs, e_w1[l], e_b1[l].reshape(N_EXPERTS, 1, -1),
                  e_w2[l], e_b2[l].reshape(N_EXPERTS, 1, -1))
    out = _final(pos2d, ys, topw, x1, mod, ffn_post_g[l].reshape(1, D), tb=MOE_TB, s_lat=s_lat)
    return out.reshape(b, s_lat, D)
```

```python
import functools

import jax
import jax.numpy as jnp
from jax import lax
from jax.experimental import pallas as pl
from jax.experimental.pallas import tpu as pltpu

F32 = jnp.float32
BF16 = jnp.bfloat16
U32 = jnp.uint32
HIGHEST = lax.Precision.HIGHEST

D = 1024
EPS = 1e-6
GRID_W = 64
CHUNK = 64
HEADS = 8
DK = 128
RG_C = 8.0
RG_BW = 64
N_EXPERTS = 32
TOP_K = 4
SWIGLU_LIMIT = 7.0
SWIGLU_ALPHA = 1.702
LANES = 128
MOE_TM = 512
MOE_TB = 512
ROW_SUB = D // LANES
VMEM_LIMIT = 56 * 1024 * 1024


def _cparams(sem):
    return pltpu.CompilerParams(dimension_semantics=sem, vmem_limit_bytes=VMEM_LIMIT)


def _sigmoid(x):
    return 1.0 / (1.0 + jnp.exp(-x))


def _softplus(y):
    return jnp.maximum(y, 0.0) + jnp.log1p(jnp.exp(-jnp.abs(y)))


def _nt_dot(a, b):
    return lax.dot_general(a, b, (((1,), (1,)), ((), ())), preferred_element_type=F32)


def _ada_kernel(c_ref, w_ref, b_ref, o_ref):
    c = c_ref[...]
    a = c * _sigmoid(c)
    o_ref[...] = jnp.dot(a, w_ref[...], preferred_element_type=F32, precision=HIGHEST) + b_ref[...]


def _ada_mod(c8, ada_w, ada_b):
    n = ada_w.shape[1]
    tn = 1024
    return pl.pallas_call(
        _ada_kernel,
        grid=(n // tn,),
        in_specs=[pl.BlockSpec((8, D), lambda j: (0, 0)),
                  pl.BlockSpec((D, tn), lambda j: (0, j)),
                  pl.BlockSpec((1, tn), lambda j: (0, j))],
        out_specs=pl.BlockSpec((8, tn), lambda j: (0, j)),
        out_shape=jax.ShapeDtypeStruct((8, n), F32),
        compiler_params=_cparams(("arbitrary",)),
        name="ada_mod",
    )(c8, ada_w, ada_b)


def _inproj_kernel(x_ref, g_ref, mod_ref, w_ref, wab_ref, o_ref, ab_ref, h_scr, *, tiles_per_batch, row_offset):
    i = pl.program_id(0)
    j = pl.program_id(1)

    @pl.when(j == 0)
    def _():
        x = x_ref[...]
        y = x * lax.rsqrt(jnp.mean(x * x, axis=-1, keepdims=True) + EPS) * g_ref[...]
        row = row_offset + i // tiles_per_batch
        sh = mod_ref[pl.ds(row, 1), 0:D]
        sc = mod_ref[pl.ds(row, 1), D:2 * D]
        h = (y * (1.0 + sc) + sh).astype(BF16)
        h_scr[...] = h
        ab_ref[...] = jnp.dot(h, wab_ref[...], preferred_element_type=F32)

    o_ref[0] = jnp.dot(h_scr[...], w_ref[0], preferred_element_type=F32).astype(BF16)


def _inproj(x2d, g, mod, w_main, w_ab, *, tm, tiles_per_batch, row_offset):
    n = x2d.shape[0]
    nj = w_main.shape[0]
    kern = functools.partial(_inproj_kernel, tiles_per_batch=tiles_per_batch, row_offset=row_offset)
    return pl.pallas_call(
        kern,
        grid=(n // tm, nj),
        in_specs=[pl.BlockSpec((tm, D), lambda i, j: (i, 0)),
                  pl.BlockSpec((1, D), lambda i, j: (0, 0)),
                  pl.BlockSpec((8, 6 * D), lambda i, j: (0, 0)),
                  pl.BlockSpec((1, D, D), lambda i, j: (j, 0, 0)),
                  pl.BlockSpec((D, LANES), lambda i, j: (0, 0))],
        out_specs=[pl.BlockSpec((1, tm, D), lambda i, j: (j, i, 0)),
                   pl.BlockSpec((tm, LANES), lambda i, j: (i, 0))],
        out_shape=[jax.ShapeDtypeStruct((nj, n, D), BF16),
                   jax.ShapeDtypeStruct((n, LANES), F32)],
        scratch_shapes=[pltpu.VMEM((tm, D), BF16)],
        compiler_params=_cparams(("arbitrary", "arbitrary")),
        name="inproj",
    )(x2d, g, mod, w_main, w_ab)


RG_HALO = 16
SCAN_ROWS = 8
SCAN_SLABS = 4


def _pack_bf16_pair(hi, lo):
    hi_bits = lax.bitcast_convert_type(hi.astype(BF16).astype(F32), U32)
    lo_bits = lax.bitcast_convert_type(lo.astype(BF16).astype(F32), U32)
    return hi_bits | (lo_bits >> 16)


def _rg_coef_kernel(cur_ref, prev_ref, next_ref, cw_ref, cb_ref, wblk_ref, ba_ref, bi_ref, lam_ref,
                    wf_ref, wb_ref, xbuf, *, tt, nt, reset_first):
    i = pl.program_id(0)
    b = pl.program_id(1)
    x = cur_ref[0].astype(F32)
    xp = jnp.where(i > 0, prev_ref[0].astype(F32), 0.0)
    xn = jnp.where(i < nt - 1, next_ref[0].astype(F32), 0.0)
    n_slab = D // LANES
    for j in range(n_slab):
        ls = slice(j * LANES, (j + 1) * LANES)
        xbuf[j, 0:RG_HALO, :] = xp[:, ls]
        xbuf[j, RG_HALO:RG_HALO + tt, :] = x[:, ls]
        xbuf[j, RG_HALO + tt:2 * RG_HALO + tt, :] = xn[:, ls]
    row = lax.broadcasted_iota(jnp.int32, (tt, 1), 0)
    outs = (wf_ref, wb_ref)
    for d in (0, 1):
        w = cw_ref[d]
        cols = []
        for j in range(n_slab):
            ls = slice(j * LANES, (j + 1) * LANES)
            acc = x[:, ls] * w[3:4, ls]
            for sft in (1, 2, 3):
                off = RG_HALO - sft if d == 0 else RG_HALO + sft
                acc = acc + xbuf[j, off:off + tt, :] * w[3 - sft:4 - sft, ls]
            cols.append(acc)
        xc = jnp.concatenate(cols, axis=1) + cb_ref[d:d + 1]
        xcb = xc.astype(BF16)
        pre_a = jnp.concatenate(
            [jnp.dot(xcb[:, g * 256:(g + 1) * 256], wblk_ref[d, 0, g], preferred_element_type=F32)
             for g in range(4)], axis=1) + ba_ref[d:d + 1]
        pre_i = jnp.concatenate(
            [jnp.dot(xcb[:, g * 256:(g + 1) * 256], wblk_ref[d, 1, g], preferred_element_type=F32)
             for g in range(4)], axis=1) + bi_ref[d:d + 1]
        r = _sigmoid(pre_a)
        ig = _sigmoid(pre_i)
        la = r * (-RG_C * _softplus(-lam_ref[d:d + 1]))
        a = jnp.exp(la)
        mult = jnp.sqrt(1.0 - a * a)
        if reset_first:
            if d == 0:
                first = (row == 0) & (i == 0)
            else:
                first = (row == tt - 1) & (i == nt - 1)
            mult = jnp.where(first, 1.0, mult)
        word = _pack_bf16_pair(la, mult * ig * xc)
        for half in (0, 1):
            for q in range(SCAN_SLABS):
                c0 = (half * SCAN_SLABS + q) * LANES
                outs[d][q, pl.ds(b * 2 + half, tt, stride=SCAN_ROWS), :] = word[:, c0:c0 + LANES]


def _rg_coef(xr, conv_w, conv_b, wblk, ba, bi, lam, *, tt, reset_first):
    b, t, _ = xr.shape
    nt = t // tt
    hb = tt // RG_HALO
    kern = functools.partial(_rg_coef_kernel, tt=tt, nt=nt, reset_first=reset_first)
    full = lambda shape: pl.BlockSpec(shape, lambda i, bb: (0,) * len(shape))
    out_sds = jax.ShapeDtypeStruct((SCAN_SLABS, t * SCAN_ROWS, LANES), U32)
    out_spec = pl.BlockSpec((SCAN_SLABS, tt * SCAN_ROWS, LANES), lambda i, bb: (0, i, 0))
    return pl.pallas_call(
        kern,
        grid=(nt, b),
        in_specs=[pl.BlockSpec((1, tt, D), lambda i, bb: (bb, i, 0)),
                  pl.BlockSpec((1, RG_HALO, D), lambda i, bb: (bb, jnp.maximum(i * hb - 1, 0), 0)),
                  pl.BlockSpec((1, RG_HALO, D), lambda i, bb: (bb, jnp.minimum((i + 1) * hb, t // RG_HALO - 1), 0)),
                  full((2, 4, D)), full((2, D)), full((2, 2, 4, 256, 256)),
                  full((2, D)), full((2, D)), full((2, D))],
        out_specs=[out_spec, out_spec],
        out_shape=[out_sds, out_sds],
        scratch_shapes=[pltpu.VMEM((D // LANES, tt + 2 * RG_HALO, LANES), F32)],
        compiler_params=_cparams(("arbitrary", "arbitrary")),
        name="rg_coef",
    )(xr, xr, xr, conv_w, conv_b, wblk, ba, bi, lam)


def _scan_step(word, h):
    la = lax.bitcast_convert_type(word & jnp.uint32(0xFFFF0000), F32)
    bb = lax.bitcast_convert_type(word << 16, F32)
    return jnp.exp(la) * h + bb


def _rg_scan_run(wf, wb, hf, hb, carry, n):
    def body(t, c):
        h_f, h_b = c
        r0 = pl.multiple_of(t * SCAN_ROWS, SCAN_ROWS)
        r1 = pl.multiple_of((n - 1 - t) * SCAN_ROWS, SCAN_ROWS)
        new_f, new_b = [], []
        for q in range(SCAN_SLABS):
            f = _scan_step(wf[q, pl.ds(r0, SCAN_ROWS), :], h_f[q])
            g = _scan_step(wb[q, pl.ds(r1, SCAN_ROWS), :], h_b[q])
            if hf is not None:
                hf[q, pl.ds(r0, SCAN_ROWS), :] = f
                hb[q, pl.ds(r1, SCAN_ROWS), :] = g
            new_f.append(f)
            new_b.append(g)
        return tuple(new_f), tuple(new_b)

    init = (tuple(carry[0, q] for q in range(SCAN_SLABS)), tuple(carry[1, q] for q in range(SCAN_SLABS)))
    h_f, h_b = lax.fori_loop(0, n, body, init, unroll=8)
    for q in range(SCAN_SLABS):
        carry[0, q] = h_f[q]
        carry[1, q] = h_b[q]


def _rg_scan_kernel(wf_c, wb_c, wf_l, wb_l, hf_ref, hb_ref, carry, *, t_ctx, tt):
    s = pl.program_id(0)

    @pl.when(s == 0)
    def _():
        carry[...] = jnp.zeros(carry.shape, F32)
        _rg_scan_run(wf_c, wb_c, None, None, carry, t_ctx)

    @pl.when(s > 0)
    def _():
        _rg_scan_run(wf_l, wb_l, hf_ref, hb_ref, carry, tt)


def _rg_scan(coef_c, coef_l, *, t_ctx, t_lat, tt):
    nt = t_lat // tt
    ctx_spec = pl.BlockSpec((SCAN_SLABS, t_ctx * SCAN_ROWS, LANES), lambda s: (0, 0, 0))
    f_spec = pl.BlockSpec((SCAN_SLABS, tt * SCAN_ROWS, LANES), lambda s: (0, jnp.maximum(s - 1, 0), 0))
    b_spec = pl.BlockSpec((SCAN_SLABS, tt * SCAN_ROWS, LANES), lambda s: (0, nt - jnp.maximum(s, 1), 0))
    out_sds = jax.ShapeDtypeStruct((SCAN_SLABS, t_lat * SCAN_ROWS, LANES), F32)
    kern = functools.partial(_rg_scan_kernel, t_ctx=t_ctx, tt=tt)
    return pl.pallas_call(
        kern,
        grid=(nt + 1,),
        in_specs=[ctx_spec, ctx_spec, f_spec, b_spec],
        out_specs=[f_spec, b_spec],
        out_shape=[out_sds, out_sds],
        scratch_shapes=[pltpu.VMEM((2, SCAN_SLABS, SCAN_ROWS, LANES), F32)],
        compiler_params=_cparams(("arbitrary",)),
        name="rg_scan",
    )(coef_c[0], coef_c[1], coef_l[0], coef_l[1])


N_CTX_CHUNKS = 4
CONV_SLABS = 3 * D // LANES
CHAIN_GROUP = 16
NEUMANN_STEPS = 6


def _delta_prep(d, is_ctx, refs_l, refs_c, prevraw, cw_ref, alog_ref, dtb_ref):
    q_l, k_l, v_l, ab_l = refs_l
    q_c, k_c, v_c, ab_c = refs_c

    def sel(rc, rl):
        return jnp.where(is_ctx, rc[0], rl[0])

    raw = jnp.concatenate([sel(q_c, q_l), sel(k_c, k_l), sel(v_c, v_l)], axis=1).astype(F32)
    abv = jnp.where(is_ctx, ab_c[...], ab_l[...])
    base = 8 if d == 0 else 0
    for j in range(CONV_SLABS):
        prevraw[d, j, base:base + CHUNK, :] = raw[:, j * LANES:(j + 1) * LANES]
    w = cw_ref[d]
    outs = []
    for j in range(CONV_SLABS):
        wj = w[:, j * LANES:(j + 1) * LANES]
        acc = raw[:, j * LANES:(j + 1) * LANES] * wj[3:4]
        for sft in (1, 2, 3):
            off = base - sft if d == 0 else sft
            acc = acc + prevraw[d, j, off:off + CHUNK, :] * wj[3 - sft:4 - sft]
        outs.append(acc)
    acc = jnp.concatenate(outs, axis=1)
    if d == 0:
        prevraw[d, :, 0:8, :] = prevraw[d, :, CHUNK:CHUNK + 8, :]
    else:
        prevraw[d, :, CHUNK:CHUNK + 8, :] = prevraw[d, :, 0:8, :]
    y = acc * _sigmoid(acc)

    bi = lax.broadcasted_iota(jnp.int32, (2 * DK, 2 * DK), 0) // DK
    bj = lax.broadcasted_iota(jnp.int32, (2 * DK, 2 * DK), 1) // DK
    ones_blk = (bi == bj).astype(BF16)
    qk_sq = (y[:, 0:2 * D] * y[:, 0:2 * D]).astype(BF16)
    ss = jnp.concatenate(
        [jnp.dot(qk_sq[:, g * 2 * DK:(g + 1) * 2 * DK], ones_blk, preferred_element_type=F32)
         for g in range(2 * D // (2 * DK))], axis=1)
    inv = lax.rsqrt(ss + EPS)
    qn_all = y[:, 0:D] * inv[:, 0:D] * (DK ** -0.5)
    kn_all = y[:, D:2 * D] * inv[:, D:2 * D]
    y = jnp.concatenate([qn_all, kn_all, y[:, 2 * D:3 * D]], axis=1)

    rp = lax.broadcasted_iota(jnp.int32, (2 * CHUNK, CHUNK), 0)
    cp = lax.broadcasted_iota(jnp.int32, (2 * CHUNK, CHUNK), 1)
    if d == 0:
        m_pad = ((rp >= cp) & (rp < CHUNK)).astype(F32)
    else:
        m_pad = ((rp <= cp) & (rp < CHUNK)).astype(F32)
    g_all = -jnp.exp(alog_ref[...]) * _softplus(abv + dtb_ref[...])
    beta_all = _sigmoid(abv)
    gcum = jnp.dot(m_pad, g_all, preferred_element_type=F32, precision=HIGHEST)
    return y, gcum, gcum.T, beta_all


def _delta_heads(preps, out_refs, state, ybuf):
    ri = lax.broadcasted_iota(jnp.int32, (CHUNK, CHUNK), 0)
    ci = lax.broadcasted_iota(jnp.int32, (CHUNK, CHUNK), 1)
    incl = (ri >= ci, ri <= ci)
    strict = (ri > ci, ri < ci)
    last = (CHUNK - 1, 0)
    eye = (ri == ci).astype(F32)
    keep = lax.broadcasted_iota(jnp.int32, (CHUNK, 2 * CHUNK), 1) >= CHUNK
    zeros_half = jnp.zeros((CHUNK, DK), F32)
    for d in (0, 1):
        ybuf[d] = preps[d][0]
    gcum = (preps[0][1], preps[1][1])
    gcum_t = (preps[0][2], preps[1][2])
    beta_all = (preps[0][3], preps[1][3])
    q_of = lambda c: ybuf[c[0], :, c[1] * DK:(c[1] + 1) * DK]
    k_of = lambda c: ybuf[c[0], :, D + c[1] * DK:D + (c[1] + 1) * DK]
    v_of = lambda c: ybuf[c[0], :, 2 * D + c[1] * DK:2 * D + (c[1] + 1) * DK]
    lane_of = lambda c: c[0] * HEADS + c[1]
    gcol_of = lambda c: gcum[c[0]][0:CHUNK, lane_of(c):lane_of(c) + 1]
    bcol_of = lambda c: beta_all[c[0]][:, 2 * HEADS + lane_of(c):2 * HEADS + lane_of(c) + 1]

    all_chains = [(d, h) for h in range(HEADS) for d in (0, 1)]
    for g0 in range(0, len(all_chains), CHAIN_GROUP):
        chains = all_chains[g0:g0 + CHAIN_GROUP]
        qmat, qkd, sol, wqh, v_new = {}, {}, {}, {}, {}
        for c in chains:
            d = c[0]
            grow = gcum_t[d][lane_of(c):lane_of(c) + 1, 0:CHUNK]
            kn = k_of(c)
            knb = kn.astype(BF16)
            a_mat = _nt_dot((kn * bcol_of(c)).astype(BF16), knb)
            qk = _nt_dot(q_of(c).astype(BF16), knb)
            dec = jnp.where(incl[d], jnp.exp(jnp.where(incl[d], gcol_of(c) - grow, 0.0)), 0.0)
            s0 = jnp.where(strict[d], -a_mat * dec, 0.0)
            qmat[c] = jnp.concatenate([s0, eye], axis=1)
            qkd[c] = (qk * dec).astype(BF16)

        for _ in range(NEUMANN_STEPS):
            for c in chains:
                qm = qmat[c]
                qmat[c] = jnp.dot(qm[:, 0:CHUNK].astype(BF16), qm.astype(BF16),
                                  preferred_element_type=F32) + jnp.where(keep, qm, 0.0)

        for c in chains:
            bcol = bcol_of(c)
            rhs = jnp.concatenate([v_of(c) * bcol, k_of(c) * (bcol * jnp.exp(gcol_of(c)))], axis=1).astype(BF16)
            sol[c] = jnp.dot(qmat[c][:, CHUNK:2 * CHUNK].astype(BF16), rhs, preferred_element_type=F32)
        for c in chains:
            wq = jnp.concatenate([sol[c][:, DK:2 * DK], q_of(c) * jnp.exp(gcol_of(c))], axis=0)
            wqh[c] = jnp.dot(wq.astype(BF16), state[c[0], c[1]].astype(BF16), preferred_element_type=F32)
        for c in chains:
            v_new[c] = sol[c][:, 0:DK] - wqh[c][0:CHUNK]
        for c in chains:
            o = wqh[c][CHUNK:2 * CHUNK] + jnp.dot(qkd[c], v_new[c].astype(BF16), preferred_element_type=F32)
            out_refs[c[0]][:, c[1] * DK:(c[1] + 1) * DK] = o.astype(BF16)
        for c in chains:
            d = c[0]
            gl = gcum[d][last[d]:last[d] + 1, lane_of(c):lane_of(c) + 1]
            kd = k_of(c) * jnp.exp(gl - gcol_of(c))
            kd_t = jnp.concatenate([kd, zeros_half], axis=0).T
            v_pad = jnp.concatenate([v_new[c], zeros_half], axis=0)
            state[c[0], c[1]] = state[c[0], c[1]] * jnp.exp(gl) + jnp.dot(
                kd_t.astype(BF16), v_pad.astype(BF16), preferred_element_type=F32)


def _delta_kernel(qf_l, kf_l, vf_l, abf_l, qb_l, kb_l, vb_l, abb_l,
                  qf_c, kf_c, vf_c, abf_c, qb_c, kb_c, vb_c, abb_c,
                  cw_ref, alog_ref, dtb_ref, of_ref, ob_ref, state, prevraw, ybuf):
    s = pl.program_id(1)
    is_ctx = s < N_CTX_CHUNKS

    @pl.when(s == 0)
    def _():
        state[...] = jnp.zeros(state.shape, F32)

    @pl.when((s == 0) | (s == N_CTX_CHUNKS))
    def _():
        prevraw[...] = jnp.zeros(prevraw.shape, F32)

    pf = _delta_prep(0, is_ctx, (qf_l, kf_l, vf_l, abf_l), (qf_c, kf_c, vf_c, abf_c), prevraw,
                     cw_ref, alog_ref, dtb_ref)
    pb = _delta_prep(1, is_ctx, (qb_l, kb_l, vb_l, abb_l), (qb_c, kb_c, vb_c, abb_c), prevraw,
                     cw_ref, alog_ref, dtb_ref)
    _delta_heads((pf, pb), (of_ref, ob_ref), state, ybuf)


def _delta(qkv_col, ab_col, o_ctx, ab_ctx, conv_w, alog_l, dtb_l, *, b, s_lat, t_ctx):
    ncol = s_lat // CHUNK
    assert s_lat // GRID_W == CHUNK and t_ctx == N_CTX_CHUNKS * CHUNK

    col_f = lambda bb, s: bb * ncol + jnp.maximum(s - N_CTX_CHUNKS, 0)
    col_b = lambda bb, s: bb * ncol + ncol - 1 - jnp.maximum(s - N_CTX_CHUNKS, 0)
    cch_f = lambda bb, s: bb * N_CTX_CHUNKS + jnp.minimum(s, N_CTX_CHUNKS - 1)
    cch_b = lambda bb, s: bb * N_CTX_CHUNKS + N_CTX_CHUNKS - 1 - jnp.minimum(s, N_CTX_CHUNKS - 1)

    def tok_spec(j, fn):
        return pl.BlockSpec((1, CHUNK, D), lambda bb, s: (j, fn(bb, s), 0))

    ab_spec = lambda fn: pl.BlockSpec((CHUNK, LANES), lambda bb, s: (fn(bb, s), 0))
    full = lambda shape: pl.BlockSpec(shape, lambda bb, s: (0,) * len(shape))

    in_specs = ([tok_spec(0, col_f), tok_spec(1, col_f), tok_spec(2, col_f), ab_spec(col_f),
                 tok_spec(0, col_b), tok_spec(1, col_b), tok_spec(2, col_b), ab_spec(col_b),
                 tok_spec(2, cch_f), tok_spec(3, cch_f), tok_spec(4, cch_f), ab_spec(cch_f),
                 tok_spec(2, cch_b), tok_spec(3, cch_b), tok_spec(4, cch_b), ab_spec(cch_b),
                 full((2, 4, 3 * D)), full((1, LANES)), full((1, LANES))])
    out_sds = jax.ShapeDtypeStruct((b * s_lat, D), BF16)
    out_specs = [pl.BlockSpec((CHUNK, D), lambda bb, s: (col_f(bb, s), 0)),
                 pl.BlockSpec((CHUNK, D), lambda bb, s: (col_b(bb, s), 0))]
    return pl.pallas_call(
        _delta_kernel,
        grid=(b, N_CTX_CHUNKS + ncol),
        in_specs=in_specs,
        out_specs=out_specs,
        out_shape=[out_sds, out_sds],
        scratch_shapes=[pltpu.VMEM((2, HEADS, DK, DK), F32), pltpu.VMEM((2, CONV_SLABS, CHUNK + 8, LANES), F32),
                        pltpu.VMEM((2, CHUNK, 3 * D), F32)],
        compiler_params=_cparams(("arbitrary", "arbitrary")),
        name="delta",
    )(qkv_col, qkv_col, qkv_col, ab_col, qkv_col, qkv_col, qkv_col, ab_col,
      o_ctx, o_ctx, o_ctx, ab_ctx, o_ctx, o_ctx, o_ctx, ab_ctx,
      conv_w, alog_l, dtb_l)


def _rms(x, g):
    return x * lax.rsqrt(jnp.mean(x * x, axis=-1, keepdims=True) + EPS) * g


def _store_row_tiles(ref, val, n):
    for s in range(ROW_SUB):
        ref[pl.ds(s, n, stride=ROW_SUB), :] = val[:, s * LANES:(s + 1) * LANES]


def _load_row_tiles(ref, n):
    return jnp.concatenate([ref[pl.ds(s, n, stride=ROW_SUB), :] for s in range(ROW_SUB)], axis=1)


def _scan_rows(ref, b, tm):
    return jnp.concatenate(
        [ref[q, pl.ds(b * 2 + half, tm, stride=SCAN_ROWS), :] for half in (0, 1) for q in range(SCAN_SLABS)],
        axis=1)


def _post_kernel(x_ref, gate_ref, z_ref, grg_ref, gdn_ref, hf_ref, hb_ref, of_ref, ob_ref, mod_ref,
                 wrg_ref, wdn_ref, wout_ref, dng_ref, postg_ref, preg_ref, rw_ref, rb_ref,
                 x1_ref, hf2_ref, topi_ref, topw_ref, rank_ref, cnt_ref, carry, *, tm):
    b = pl.program_id(1)

    @pl.when((pl.program_id(0) == 0) & (b == 0))
    def _():
        carry[...] = jnp.zeros(carry.shape, F32)

    gate = gate_ref[0].astype(F32)
    gelu = 0.5 * gate * (1.0 + jnp.tanh(0.7978845608028654 * (gate + 0.044715 * gate * gate * gate)))
    rg_h = _scan_rows(hf_ref, b, tm) + _scan_rows(hb_ref, b, tm)
    y_rg = jnp.dot((rg_h * gelu).astype(BF16), wrg_ref[...], preferred_element_type=F32)

    dn = of_ref[...].astype(F32) + ob_ref[...].astype(F32)
    z = z_ref[0].astype(F32)
    parts = []
    for h in range(HEADS):
        seg = dn[:, h * DK:(h + 1) * DK]
        parts.append(seg * lax.rsqrt(jnp.mean(seg * seg, axis=-1, keepdims=True) + EPS))
    o = jnp.concatenate(parts, axis=1) * dng_ref[...] * (z * _sigmoid(z))
    y_dn = jnp.dot(o.astype(BF16), wdn_ref[...], preferred_element_type=F32)

    merged = _sigmoid(grg_ref[0].astype(F32)) * y_rg + _sigmoid(gdn_ref[0].astype(F32)) * y_dn
    y = jnp.dot(merged.astype(BF16), wout_ref[...], preferred_element_type=F32)

    g1 = mod_ref[pl.ds(b, 1), 2 * D:3 * D]
    sh2 = mod_ref[pl.ds(b, 1), 3 * D:4 * D]
    sc2 = mod_ref[pl.ds(b, 1), 4 * D:5 * D]
    x1 = x_ref[...] + g1 * _rms(y, postg_ref[...])
    x1_ref[...] = x1
    hf2 = _rms(x1, preg_ref[...]) * (1.0 + sc2) + sh2
    _store_row_tiles(hf2_ref, hf2, tm)

    lane = lax.broadcasted_iota(jnp.int32, (tm, LANES), 1)
    lane_f = lane.astype(F32)
    h_hi = hf2.astype(BF16)
    h_lo = (hf2 - h_hi.astype(F32)).astype(BF16)
    rw2 = rw_ref[...]
    hh = jnp.dot(h_hi, rw2, preferred_element_type=F32)
    logits = (hh[:, 0:LANES] + hh[:, LANES:2 * LANES]
              + jnp.dot(h_lo, rw2[:, 0:LANES], preferred_element_type=F32) + rb_ref[...])
    neg = jnp.float32(-jnp.inf)
    lg = jnp.where(lane < N_EXPERTS, logits, neg)
    vals, sels = [], []
    topi = jnp.zeros((tm, LANES), F32)
    onehot = jnp.zeros((tm, LANES), F32)
    for k in range(TOP_K):
        m = jnp.max(lg, axis=-1, keepdims=True)
        idx = jnp.min(jnp.where(lg == m, lane_f, float(LANES)), axis=-1, keepdims=True)
        sel = lane_f == idx
        vals.append(m)
        sels.append(sel)
        lg = jnp.where(sel, neg, lg)
        onehot = onehot + sel.astype(F32)
        topi = topi + jnp.where(lane == k, idx, 0.0)
    es = [jnp.exp(v - vals[0]) for v in vals]
    den = es[0] + es[1] + es[2] + es[3]
    topw = jnp.zeros((tm, LANES), F32)
    for k in range(TOP_K):
        topw = topw + jnp.where(lane == k, es[k] / den, 0.0)

    ri = lax.broadcasted_iota(jnp.int32, (tm, tm), 0)
    ci = lax.broadcasted_iota(jnp.int32, (tm, tm), 1)
    tri = (ri > ci).astype(BF16)
    cum = jnp.dot(tri, onehot.astype(BF16), preferred_element_type=F32) + carry[...]
    rank = jnp.zeros((tm, LANES), F32)
    for k in range(TOP_K):
        rk = jnp.sum(jnp.where(sels[k], cum, 0.0), axis=-1, keepdims=True)
        rank = rank + jnp.where(lane == k, rk, 0.0)
    new_carry = carry[...] + jnp.sum(onehot, axis=0, keepdims=True)
    carry[...] = new_carry
    topi_ref[...] = topi.astype(jnp.int32)
    topw_ref[...] = topw
    rank_ref[...] = rank.astype(jnp.int32)
    cnt_ref[...] = jnp.broadcast_to(new_carry, (8, LANES)).astype(jnp.int32)


def _post(x2d, o_lat, hf, hb, of, ob, mod, wrg, wdn, wout, dng, postg, preg, rw, rb, *, tm, s_lat):
    n = x2d.shape[0]
    tpb = s_lat // tm
    nb = n // s_lat
    kern = functools.partial(_post_kernel, tm=tm)
    tok = pl.BlockSpec((tm, D), lambda t, b: (b * tpb + t, 0))
    oj = lambda j: pl.BlockSpec((1, tm, D), lambda t, b: (j, b * tpb + t, 0))
    hspec = pl.BlockSpec((SCAN_SLABS, tm * SCAN_ROWS, LANES), lambda t, b: (0, t, 0))
    full = lambda shape: pl.BlockSpec(shape, lambda t, b: (0,) * len(shape))
    lane_out = pl.BlockSpec((tm, LANES), lambda t, b: (b * tpb + t, 0))
    return pl.pallas_call(
        kern,
        grid=(tpb, nb),
        in_specs=[tok, oj(1), oj(2), oj(3), oj(4), hspec, hspec, tok, tok, full((8, 6 * D)),
                  full((D, D)), full((D, D)), full((D, D)), full((1, D)), full((1, D)), full((1, D)),
                  full((D, 2 * LANES)), full((1, LANES))],
        out_specs=[tok, pl.BlockSpec((tm * ROW_SUB, LANES), lambda t, b: (b * tpb + t, 0)), lane_out, lane_out,
                   lane_out, pl.BlockSpec((8, LANES), lambda t, b: (0, 0))],
        out_shape=[jax.ShapeDtypeStruct((n, D), F32), jax.ShapeDtypeStruct((n * ROW_SUB, LANES), F32),
                   jax.ShapeDtypeStruct((n, LANES), jnp.int32), jax.ShapeDtypeStruct((n, LANES), F32),
                   jax.ShapeDtypeStruct((n, LANES), jnp.int32), jax.ShapeDtypeStruct((8, LANES), jnp.int32)],
        scratch_shapes=[pltpu.VMEM((1, LANES), F32)],
        compiler_params=_cparams(("arbitrary", "arbitrary")),
        name="post_mix",
    )(x2d, o_lat, o_lat, o_lat, o_lat, hf, hb, of, ob, mod, wrg, wdn, wout, dng, postg, preg, rw, rb)


def _idx_copy(pos_hbm, idx, isem, tile, slot):
    n_idx = pos_hbm.shape[1]
    return pltpu.make_async_copy(pos_hbm.at[tile], idx.at[pl.ds(pl.multiple_of(slot * n_idx, n_idx), n_idx)],
                                 isem.at[slot])


def _row_dma_loop(idx, slot, tb, issue):
    base = slot * (tb * TOP_K)

    def body(rt, c):
        r0 = pl.multiple_of(rt * 8, 8)
        i0 = base + rt * (8 * TOP_K)
        for j in range(8):
            for k in range(TOP_K):
                issue(r0 + j, k, idx[i0 + j * TOP_K + k])
        return c

    lax.fori_loop(0, tb // 8, body, 0)


def _dispatch_kernel(starts_ref, ends_ref, pos_hbm, x_ref, xs_hbm, idx, zbuf, sem, isem, *, tb):
    i = pl.program_id(0)
    n = pl.num_programs(0)
    slot = i % 2

    @pl.when(i == 0)
    def _():
        _idx_copy(pos_hbm, idx, isem, 0, 0).start()
        zbuf[...] = jnp.zeros(zbuf.shape, F32)

        def zero_copy(e):
            r0 = pl.multiple_of(ends_ref[e] - MOE_TM, MOE_TM)
            return pltpu.make_async_copy(zbuf, xs_hbm.at[pl.ds(r0, MOE_TM)], sem)

        for e in range(N_EXPERTS):
            @pl.when(ends_ref[e] > starts_ref[e])
            def _():
                zero_copy(e).start()
        for e in range(N_EXPERTS):
            @pl.when(ends_ref[e] > starts_ref[e])
            def _():
                zero_copy(e).wait()

        def tail_copy(t):
            return pltpu.make_async_copy(zbuf, xs_hbm.at[pl.ds(pl.multiple_of(t * MOE_TM, MOE_TM), MOE_TM)], sem)

        n_used = ends_ref[N_EXPERTS - 1] // MOE_TM
        n_tiles = xs_hbm.shape[0] // MOE_TM

        def tail_start(t, c):
            tail_copy(t).start()
            return c

        def tail_wait(t, c):
            tail_copy(t).wait()
            return c

        lax.fori_loop(n_used, n_tiles, tail_start, 0)
        lax.fori_loop(n_used, n_tiles, tail_wait, 0)

    @pl.when(i + 1 < n)
    def _():
        _idx_copy(pos_hbm, idx, isem, i + 1, 1 - slot).start()

    _idx_copy(pos_hbm, idx, isem, i, slot).wait()

    def issue(r, k, p):
        pltpu.make_async_copy(x_ref.at[r], xs_hbm.at[p], sem).start(priority=k % 2)

    _row_dma_loop(idx, slot, tb, issue)
    for k in range(TOP_K):
        pltpu.make_async_copy(x_ref, xs_hbm.at[pl.ds(0, tb)], sem).wait()


def _dispatch(starts, ends, pos2d, xp, n_rows, *, tb):
    n = xp.shape[0]
    grid_spec = pltpu.PrefetchScalarGridSpec(
        num_scalar_prefetch=2,
        grid=(n // tb,),
        in_specs=[pl.BlockSpec(memory_space=pl.ANY),
                  pl.BlockSpec((tb, ROW_SUB, LANES), lambda i, st, en: (i, 0, 0))],
        out_specs=pl.BlockSpec(memory_space=pl.ANY),
        scratch_shapes=[pltpu.SMEM((2 * tb * TOP_K,), jnp.int32), pltpu.VMEM((MOE_TM, ROW_SUB, LANES), F32),
                        pltpu.SemaphoreType.DMA, pltpu.SemaphoreType.DMA((2,))],
    )
    return pl.pallas_call(
        functools.partial(_dispatch_kernel, tb=tb),
        grid_spec=grid_spec,
        out_shape=jax.ShapeDtypeStruct((n_rows, ROW_SUB, LANES), F32),
        compiler_params=_cparams(("arbitrary",)),
        name="dispatch",
    )(starts, ends, pos2d, xp)


def _expert_kernel(te_ref, nu_ref, xs_ref, w1_ref, b1_ref, w2_ref, b2_ref, ys_ref, w1b, w2b):
    i = pl.program_id(0)

    @pl.when(i < nu_ref[0])
    def _():
        changed = (i == 0) | (te_ref[i] != te_ref[jnp.maximum(i - 1, 0)])

        @pl.when(changed)
        def _():
            w1b[...] = w1_ref[0].astype(BF16)
            w2b[...] = w2_ref[0].astype(BF16)

        x = _load_row_tiles(xs_ref, MOE_TM).astype(BF16)
        hh = jnp.dot(x, w1b[...], preferred_element_type=F32) + b1_ref[0]
        glu = jnp.minimum(hh[:, 0:D], SWIGLU_LIMIT)
        lin = jnp.clip(hh[:, D:2 * D], -SWIGLU_LIMIT, SWIGLU_LIMIT)
        act = glu * _sigmoid(SWIGLU_ALPHA * glu) * (lin + 1.0)
        out = jnp.dot(act.astype(BF16), w2b[...], preferred_element_type=F32) + b2_ref[0]
        _store_row_tiles(ys_ref, out, MOE_TM)

    @pl.when(i >= nu_ref[0])
    def _():
        ys_ref[...] = jnp.zeros(ys_ref.shape, F32)


def _experts(tile_expert, n_used, xs, w1, b1, w2, b2):
    n_rows = xs.shape[0] // ROW_SUB
    n_tiles = n_rows // MOE_TM
    row_map = lambda i, te, nu: (jnp.minimum(i, nu[0] - 1), 0)
    grid_spec = pltpu.PrefetchScalarGridSpec(
        num_scalar_prefetch=2,
        grid=(n_tiles,),
        in_specs=[pl.BlockSpec((MOE_TM * ROW_SUB, LANES), row_map),
                  pl.BlockSpec((1, D, 2 * D), lambda i, te, nu: (te[i], 0, 0)),
                  pl.BlockSpec((1, 1, 2 * D), lambda i, te, nu: (te[i], 0, 0)),
                  pl.BlockSpec((1, D, D), lambda i, te, nu: (te[i], 0, 0)),
                  pl.BlockSpec((1, 1, D), lambda i, te, nu: (te[i], 0, 0))],
        out_specs=pl.BlockSpec((MOE_TM * ROW_SUB, LANES), lambda i, te, nu: (i, 0)),
        scratch_shapes=[pltpu.VMEM((D, 2 * D), BF16), pltpu.VMEM((D, D), BF16)],
    )
    return pl.pallas_call(
        _expert_kernel,
        grid_spec=grid_spec,
        out_shape=jax.ShapeDtypeStruct((n_rows * ROW_SUB, LANES), F32),
        compiler_params=_cparams(("arbitrary",)),
        name="experts",
    )(tile_expert, n_used, xs, w1, b1, w2, b2)


def _final_kernel(pos_hbm, ys_hbm, topw_ref, x1_ref, mod_ref, postg_ref, o_ref, idx, buf, sem, isem,
                  *, tb, tiles_per_batch):
    i = pl.program_id(0)
    n = pl.num_programs(0)
    slot = i % 2

    @pl.when(i == 0)
    def _():
        _idx_copy(pos_hbm, idx, isem, 0, 0).start()

    @pl.when(i + 1 < n)
    def _():
        _idx_copy(pos_hbm, idx, isem, i + 1, 1 - slot).start()

    _idx_copy(pos_hbm, idx, isem, i, slot).wait()

    def issue(r, k, p):
        src = ys_hbm.at[pl.ds(pl.multiple_of(p * ROW_SUB, ROW_SUB), ROW_SUB)]
        dst = buf.at[k, pl.ds(pl.multiple_of(r * ROW_SUB, ROW_SUB), ROW_SUB)]
        pltpu.make_async_copy(src, dst, sem).start(priority=k % 2)

    _row_dma_loop(idx, slot, tb, issue)
    for k in range(TOP_K):
        pltpu.make_async_copy(ys_hbm.at[pl.ds(0, tb * ROW_SUB)], buf.at[k], sem).wait()

    row = i // tiles_per_batch
    tw = topw_ref[...]
    f = _load_row_tiles(buf.at[0], tb) * tw[:, 0:1]
    for k in range(1, TOP_K):
        f = f + _load_row_tiles(buf.at[k], tb) * tw[:, k:k + 1]
    g2 = mod_ref[pl.ds(row, 1), 5 * D:6 * D]
    o_ref[...] = x1_ref[...] + g2 * _rms(f, postg_ref[...])


def _final(pos2d, ys, topw, x1, mod, postg, *, tb, s_lat):
    n = x1.shape[0]
    kern = functools.partial(_final_kernel, tb=tb, tiles_per_batch=s_lat // tb)
    return pl.pallas_call(
        kern,
        grid=(n // tb,),
        in_specs=[pl.BlockSpec(memory_space=pl.ANY),
                  pl.BlockSpec(memory_space=pl.ANY),
                  pl.BlockSpec((tb, LANES), lambda i: (i, 0)),
                  pl.BlockSpec((tb, D), lambda i: (i, 0)),
                  pl.BlockSpec((8, 6 * D), lambda i: (0, 0)),
                  pl.BlockSpec((1, D), lambda i: (0, 0))],
        out_specs=pl.BlockSpec((tb, D), lambda i: (i, 0)),
        out_shape=jax.ShapeDtypeStruct((n, D), F32),
        scratch_shapes=[pltpu.SMEM((2 * tb * TOP_K,), jnp.int32), pltpu.VMEM((TOP_K, tb * ROW_SUB, LANES), F32),
                        pltpu.SemaphoreType.DMA, pltpu.SemaphoreType.DMA((2,))],
        compiler_params=_cparams(("arbitrary",)),
        name="final",
    )(pos2d, ys, topw, x1, mod, postg)


def _block_diag(w):
    w = w.reshape(2, 4, 4, RG_BW, RG_BW)
    eye = jnp.eye(4, dtype=w.dtype)
    return jnp.einsum('dgiab,ij->dgiajb', w, eye).reshape(2, 4, 4 * RG_BW, 4 * RG_BW)


def _grid_transpose(t, b):
    return t.reshape(b, GRID_W, GRID_W, -1).transpose(0, 2, 1, 3).reshape(t.shape)


def kernel(x, c, ctx, c_ctx, ada_w, ada_b, mix_pre_g, mix_post_g, w_in, rg_conv_w, rg_conv_b, rg_wa, rg_ba,
           rg_wi, rg_bi, rg_lam, rg_w_o, dn_conv_w, dn_a_log, dn_dt_bias, dn_norm_g, dn_w_o, w_out,
           ffn_pre_g, ffn_post_g, router_w, router_b, e_w1, e_b1, e_w2, e_b2):
    b, s_lat, _ = x.shape
    t_ctx = ctx.shape[1]
    depth = ada_w.shape[0]
    assert depth == 1 and b * 2 == SCAN_ROWS and s_lat == GRID_W * GRID_W
    n_lat = b * s_lat
    l = 0

    c8 = jnp.zeros((8, D), F32).at[0:b].set(c).at[b].set(c_ctx)
    mod = _ada_mod(c8, ada_w[l], ada_b[l].reshape(1, -1))

    wi = w_in[l]
    blk = lambda j0: wi[:, j0:j0 + D]
    w_raster = jnp.stack([blk(0), blk(D), blk(5 * D), blk(6 * D + 32), blk(7 * D + 32)]).astype(BF16)
    w_qkv = jnp.stack([blk(2 * D), blk(3 * D), blk(4 * D)]).astype(BF16)
    w_all = jnp.stack([blk(0), blk(D), blk(2 * D), blk(3 * D), blk(4 * D)]).astype(BF16)
    w_ab = jnp.pad(wi[:, 6 * D:6 * D + 32], ((0, 0), (0, LANES - 32))).astype(BF16)
    pre_g = mix_pre_g[l].reshape(1, D)
    x2d = x.reshape(n_lat, D)
    x_col = _grid_transpose(x2d, b)
    tpb = s_lat // 1024
    o_lat, _ = _inproj(x2d, pre_g, mod, w_raster, w_ab, tm=1024, tiles_per_batch=tpb, row_offset=0)
    qkv_col, ab_col = _inproj(x_col, pre_g, mod, w_qkv, w_ab, tm=1024, tiles_per_batch=tpb, row_offset=0)
    o_ctx, ab_ctx = _inproj(ctx.reshape(b * t_ctx, D), pre_g, mod, w_all, w_ab, tm=b * t_ctx,
                            tiles_per_batch=1, row_offset=b)

    wblk = jnp.stack([_block_diag(rg_wa[l]), _block_diag(rg_wi[l])], axis=1).astype(BF16)
    rg_args = (rg_conv_w[l], rg_conv_b[l], wblk, rg_ba[l], rg_bi[l], rg_lam[l])
    coef_c = _rg_coef(o_ctx[0].reshape(b, t_ctx, D), *rg_args, tt=t_ctx, reset_first=True)
    coef_l = _rg_coef(o_lat[0].reshape(b, s_lat, D), *rg_args, tt=256, reset_first=False)
    hf, hb = _rg_scan(coef_c, coef_l, t_ctx=t_ctx, t_lat=s_lat, tt=256)

    alog_l = jnp.zeros((1, LANES), F32).at[0, 0:2 * HEADS].set(dn_a_log[l].reshape(-1))
    dtb_l = jnp.zeros((1, LANES), F32).at[0, 0:2 * HEADS].set(dn_dt_bias[l].reshape(-1))
    of, ob = _delta(qkv_col, ab_col, o_ctx, ab_ctx, dn_conv_w[l], alog_l, dtb_l, b=b, s_lat=s_lat, t_ctx=t_ctx)
    of = _grid_transpose(of, b)
    ob = _grid_transpose(ob, b)

    rw = jnp.pad(router_w[l], ((0, 0), (0, LANES - N_EXPERTS)))
    rw_hi = rw.astype(BF16)
    rw = jnp.concatenate([rw_hi, (rw - rw_hi.astype(F32)).astype(BF16)], axis=1)
    rb = jnp.pad(router_b[l].reshape(1, -1), ((0, 0), (0, LANES - N_EXPERTS)))
    x1, hf2, topi, topw, rank, cnt = _post(
        x2d, o_lat, hf, hb, of, ob, mod,
        rg_w_o[l].astype(BF16), dn_w_o[l].astype(BF16), w_out[l].astype(BF16),
        jnp.tile(dn_norm_g[l], HEADS).reshape(1, D), mix_post_g[l].reshape(1, D), ffn_pre_g[l].reshape(1, D),
        rw, rb, tm=256, s_lat=s_lat)

    counts = cnt[0, 0:N_EXPERTS]
    padded = ((counts + MOE_TM - 1) // MOE_TM) * MOE_TM
    ends = jnp.cumsum(padded)
    starts = ends - padded
    top_i = topi[:, 0:TOP_K]
    pos2d = (starts[top_i] + rank[:, 0:TOP_K]).reshape(n_lat // MOE_TB, MOE_TB * TOP_K)
    n_tiles = n_lat * TOP_K // MOE_TM + N_EXPERTS
    tile_start = jnp.arange(n_tiles, dtype=jnp.int32) * MOE_TM
    tile_expert = jnp.minimum(jnp.sum(tile_start[:, None] >= ends[None, :], axis=1), N_EXPERTS - 1).astype(jnp.int32)
    n_used = (ends[-1] // MOE_TM).astype(jnp.int32).reshape(1)

    xs = _dispatch(starts, ends, pos2d, hf2.reshape(n_lat, ROW_SUB, LANES), n_tiles * MOE_TM, tb=MOE_TB)
    xs = xs.reshape(n_tiles * MOE_TM * ROW_SUB, LANES)
    ys = _experts(tile_expert, n_used, xs, e_w1[l], e_b1[l].reshape(N_EXPERTS, 1, -1),
                  e_w2[l], e_b2[l].reshape(N_EXPERTS, 1, -1))
    out = _final(pos2d, ys, topw, x1, mod, ffn_post_g[l].reshape(1, D), tb=MOE_TB, s_lat=s_lat)
    return out.reshape(b, s_lat, D)
```

```python
import functools

import jax
import jax.numpy as jnp
from jax import lax
from jax.experimental import pallas as pl
from jax.experimental.pallas import tpu as pltpu

F32 = jnp.float32
BF16 = jnp.bfloat16
U32 = jnp.uint32
HIGHEST = lax.Precision.HIGHEST

D = 1024
EPS = 1e-6
GRID_W = 64
CHUNK = 64
HEADS = 8
DK = 128
RG_C = 8.0
RG_BW = 64
N_EXPERTS = 32
TOP_K = 4
SWIGLU_LIMIT = 7.0
SWIGLU_ALPHA = 1.702
LANES = 128
INPROJ_TM = 2048
MOE_TM = 512
MOE_TB = 512
ROW_SUB = D // LANES
VMEM_LIMIT = 56 * 1024 * 1024


def _cparams(sem):
    return pltpu.CompilerParams(dimension_semantics=sem, vmem_limit_bytes=VMEM_LIMIT)


def _sigmoid(x):
    return 1.0 / (1.0 + jnp.exp(-x))


def _softplus(y):
    return jnp.maximum(y, 0.0) + jnp.log1p(jnp.exp(-jnp.abs(y)))


def _nt_dot(a, b):
    return lax.dot_general(a, b, (((1,), (1,)), ((), ())), preferred_element_type=F32)


def _ada_kernel(c_ref, w_ref, b_ref, o_ref):
    c = c_ref[...]
    a = c * _sigmoid(c)
    o_ref[...] = jnp.dot(a, w_ref[...], preferred_element_type=F32, precision=HIGHEST) + b_ref[...]


def _ada_mod(c8, ada_w, ada_b):
    n = ada_w.shape[1]
    tn = 1024
    return pl.pallas_call(
        _ada_kernel,
        grid=(n // tn,),
        in_specs=[pl.BlockSpec((8, D), lambda j: (0, 0)),
                  pl.BlockSpec((D, tn), lambda j: (0, j)),
                  pl.BlockSpec((1, tn), lambda j: (0, j))],
        out_specs=pl.BlockSpec((8, tn), lambda j: (0, j)),
        out_shape=jax.ShapeDtypeStruct((8, n), F32),
        compiler_params=_cparams(("arbitrary",)),
        name="ada_mod",
    )(c8, ada_w, ada_b)


def _inproj_kernel(x_ref, g_ref, mod_ref, w_ref, wab_ref, o_ref, ab_ref, h_scr, *, tiles_per_batch, row_offset):
    i = pl.program_id(0)
    j = pl.program_id(1)

    @pl.when(j == 0)
    def _():
        x = x_ref[...]
        y = x * lax.rsqrt(jnp.mean(x * x, axis=-1, keepdims=True) + EPS) * g_ref[...]
        row = row_offset + i // tiles_per_batch
        sh = mod_ref[pl.ds(row, 1), 0:D]
        sc = mod_ref[pl.ds(row, 1), D:2 * D]
        h = (y * (1.0 + sc) + sh).astype(BF16)
        h_scr[...] = h
        ab_ref[...] = jnp.dot(h, wab_ref[...], preferred_element_type=F32)

    o_ref[0] = jnp.dot(h_scr[...], w_ref[0], preferred_element_type=F32).astype(BF16)


def _inproj(x2d, g, mod, w_main, w_ab, *, tm, tiles_per_batch, row_offset):
    n = x2d.shape[0]
    nj = w_main.shape[0]
    kern = functools.partial(_inproj_kernel, tiles_per_batch=tiles_per_batch, row_offset=row_offset)
    return pl.pallas_call(
        kern,
        grid=(n // tm, nj),
        in_specs=[pl.BlockSpec((tm, D), lambda i, j: (i, 0)),
                  pl.BlockSpec((1, D), lambda i, j: (0, 0)),
                  pl.BlockSpec((8, 6 * D), lambda i, j: (0, 0)),
                  pl.BlockSpec((1, D, D), lambda i, j: (j, 0, 0)),
                  pl.BlockSpec((D, LANES), lambda i, j: (0, 0))],
        out_specs=[pl.BlockSpec((1, tm, D), lambda i, j: (j, i, 0)),
                   pl.BlockSpec((tm, LANES), lambda i, j: (i, 0))],
        out_shape=[jax.ShapeDtypeStruct((nj, n, D), BF16),
                   jax.ShapeDtypeStruct((n, LANES), F32)],
        scratch_shapes=[pltpu.VMEM((tm, D), BF16)],
        compiler_params=_cparams(("arbitrary", "arbitrary")),
        name="inproj",
    )(x2d, g, mod, w_main, w_ab)


RG_HALO = 16
SCAN_ROWS = 8
SCAN_SLABS = 4


def _pack_bf16_pair(hi, lo):
    hi_bits = lax.bitcast_convert_type(hi.astype(BF16).astype(F32), U32)
    lo_bits = lax.bitcast_convert_type(lo.astype(BF16).astype(F32), U32)
    return hi_bits | (lo_bits >> 16)


def _rg_coef_kernel(cur_ref, prev_ref, next_ref, cw_ref, cb_ref, wblk_ref, ba_ref, bi_ref, lam_ref,
                    wf_ref, wb_ref, xbuf, *, tt, nt, reset_first):
    i = pl.program_id(0)
    b = pl.program_id(1)
    x = cur_ref[0].astype(F32)
    xp = jnp.where(i > 0, prev_ref[0].astype(F32), 0.0)
    xn = jnp.where(i < nt - 1, next_ref[0].astype(F32), 0.0)
    n_slab = D // LANES
    for j in range(n_slab):
        ls = slice(j * LANES, (j + 1) * LANES)
        xbuf[j, 0:RG_HALO, :] = xp[:, ls]
        xbuf[j, RG_HALO:RG_HALO + tt, :] = x[:, ls]
        xbuf[j, RG_HALO + tt:2 * RG_HALO + tt, :] = xn[:, ls]
    row = lax.broadcasted_iota(jnp.int32, (tt, 1), 0)
    outs = (wf_ref, wb_ref)
    for d in (0, 1):
        w = cw_ref[d]
        cols = []
        for j in range(n_slab):
            ls = slice(j * LANES, (j + 1) * LANES)
            acc = x[:, ls] * w[3:4, ls]
            for sft in (1, 2, 3):
                off = RG_HALO - sft if d == 0 else RG_HALO + sft
                acc = acc + xbuf[j, off:off + tt, :] * w[3 - sft:4 - sft, ls]
            cols.append(acc)
        xc = jnp.concatenate(cols, axis=1) + cb_ref[d:d + 1]
        xcb = xc.astype(BF16)
        pre_a = jnp.concatenate(
            [jnp.dot(xcb[:, g * 256:(g + 1) * 256], wblk_ref[d, 0, g], preferred_element_type=F32)
             for g in range(4)], axis=1) + ba_ref[d:d + 1]
        pre_i = jnp.concatenate(
            [jnp.dot(xcb[:, g * 256:(g + 1) * 256], wblk_ref[d, 1, g], preferred_element_type=F32)
             for g in range(4)], axis=1) + bi_ref[d:d + 1]
        r = _sigmoid(pre_a)
        ig = _sigmoid(pre_i)
        la = r * (-RG_C * _softplus(-lam_ref[d:d + 1]))
        a = jnp.exp(la)
        om = 1.0 - a * a
        mult = jnp.where(om > 0.0, om * lax.rsqrt(om), 0.0)
        if reset_first:
            if d == 0:
                first = (row == 0) & (i == 0)
            else:
                first = (row == tt - 1) & (i == nt - 1)
            mult = jnp.where(first, 1.0, mult)
        word = _pack_bf16_pair(la, mult * ig * xc)
        for half in (0, 1):
            for q in range(SCAN_SLABS):
                c0 = (half * SCAN_SLABS + q) * LANES
                outs[d][q, pl.ds(b * 2 + half, tt, stride=SCAN_ROWS), :] = word[:, c0:c0 + LANES]


def _rg_coef(xr, conv_w, conv_b, wblk, ba, bi, lam, *, tt, reset_first):
    b, t, _ = xr.shape
    nt = t // tt
    hb = tt // RG_HALO
    kern = functools.partial(_rg_coef_kernel, tt=tt, nt=nt, reset_first=reset_first)
    full = lambda shape: pl.BlockSpec(shape, lambda i, bb: (0,) * len(shape))
    out_sds = jax.ShapeDtypeStruct((SCAN_SLABS, t * SCAN_ROWS, LANES), U32)
    out_spec = pl.BlockSpec((SCAN_SLABS, tt * SCAN_ROWS, LANES), lambda i, bb: (0, i, 0))
    return pl.pallas_call(
        kern,
        grid=(nt, b),
        in_specs=[pl.BlockSpec((1, tt, D), lambda i, bb: (bb, i, 0)),
                  pl.BlockSpec((1, RG_HALO, D), lambda i, bb: (bb, jnp.maximum(i * hb - 1, 0), 0)),
                  pl.BlockSpec((1, RG_HALO, D), lambda i, bb: (bb, jnp.minimum((i + 1) * hb, t // RG_HALO - 1), 0)),
                  full((2, 4, D)), full((2, D)), full((2, 2, 4, 256, 256)),
                  full((2, D)), full((2, D)), full((2, D))],
        out_specs=[out_spec, out_spec],
        out_shape=[out_sds, out_sds],
        scratch_shapes=[pltpu.VMEM((D // LANES, tt + 2 * RG_HALO, LANES), F32)],
        compiler_params=_cparams(("arbitrary", "arbitrary")),
        name="rg_coef",
    )(xr, xr, xr, conv_w, conv_b, wblk, ba, bi, lam)


def _scan_step(word, h):
    la = lax.bitcast_convert_type(word & jnp.uint32(0xFFFF0000), F32)
    bb = lax.bitcast_convert_type(word << 16, F32)
    return jnp.exp(la) * h + bb


def _rg_scan_run(wf, wb, hf, hb, carry, n):
    def body(t, c):
        h_f, h_b = c
        r0 = pl.multiple_of(t * SCAN_ROWS, SCAN_ROWS)
        r1 = pl.multiple_of((n - 1 - t) * SCAN_ROWS, SCAN_ROWS)
        new_f, new_b = [], []
        for q in range(SCAN_SLABS):
            f = _scan_step(wf[q, pl.ds(r0, SCAN_ROWS), :], h_f[q])
            g = _scan_step(wb[q, pl.ds(r1, SCAN_ROWS), :], h_b[q])
            if hf is not None:
                hf[q, pl.ds(r0, SCAN_ROWS), :] = f
                hb[q, pl.ds(r1, SCAN_ROWS), :] = g
            new_f.append(f)
            new_b.append(g)
        return tuple(new_f), tuple(new_b)

    init = (tuple(carry[0, q] for q in range(SCAN_SLABS)), tuple(carry[1, q] for q in range(SCAN_SLABS)))
    h_f, h_b = lax.fori_loop(0, n, body, init, unroll=8)
    for q in range(SCAN_SLABS):
        carry[0, q] = h_f[q]
        carry[1, q] = h_b[q]


def _rg_scan_kernel(wf_c, wb_c, wf_l, wb_l, hf_ref, hb_ref, carry, *, t_ctx, tt):
    s = pl.program_id(0)

    @pl.when(s == 0)
    def _():
        carry[...] = jnp.zeros(carry.shape, F32)
        _rg_scan_run(wf_c, wb_c, None, None, carry, t_ctx)

    @pl.when(s > 0)
    def _():
        _rg_scan_run(wf_l, wb_l, hf_ref, hb_ref, carry, tt)


def _rg_scan(coef_c, coef_l, *, t_ctx, t_lat, tt):
    nt = t_lat // tt
    ctx_spec = pl.BlockSpec((SCAN_SLABS, t_ctx * SCAN_ROWS, LANES), lambda s: (0, 0, 0))
    f_spec = pl.BlockSpec((SCAN_SLABS, tt * SCAN_ROWS, LANES), lambda s: (0, jnp.maximum(s - 1, 0), 0))
    b_spec = pl.BlockSpec((SCAN_SLABS, tt * SCAN_ROWS, LANES), lambda s: (0, nt - jnp.maximum(s, 1), 0))
    out_sds = jax.ShapeDtypeStruct((SCAN_SLABS, t_lat * SCAN_ROWS, LANES), F32)
    kern = functools.partial(_rg_scan_kernel, t_ctx=t_ctx, tt=tt)
    return pl.pallas_call(
        kern,
        grid=(nt + 1,),
        in_specs=[ctx_spec, ctx_spec, f_spec, b_spec],
        out_specs=[f_spec, b_spec],
        out_shape=[out_sds, out_sds],
        scratch_shapes=[pltpu.VMEM((2, SCAN_SLABS, SCAN_ROWS, LANES), F32)],
        compiler_params=_cparams(("arbitrary",)),
        name="rg_scan",
    )(coef_c[0], coef_c[1], coef_l[0], coef_l[1])


N_CTX_CHUNKS = 4
CONV_SLABS = 3 * D // LANES
CHAIN_GROUP = 16
NEUMANN_STEPS = 6


def _delta_prep(d, is_ctx, refs_l, refs_c, prevraw, cw_ref, alog_ref, dtb_ref, ybuf, gbuf, slot):
    q_l, k_l, v_l, ab_l = refs_l
    q_c, k_c, v_c, ab_c = refs_c

    srcs = ((q_c, q_l), (k_c, k_l), (v_c, v_l))
    base = 8 if d == 0 else 0
    pair_w = 2 * LANES
    slabs_per_src = D // LANES

    def conv_slab(j):
        rc, rl = srcs[j // slabs_per_src]
        ls = slice((j % slabs_per_src) * LANES, (j % slabs_per_src + 1) * LANES)
        raw = jnp.where(is_ctx, rc[0, :, ls], rl[0, :, ls]).astype(F32)
        prevraw[d, j, base:base + CHUNK, :] = raw
        wj = cw_ref[d, :, j * LANES:(j + 1) * LANES]
        acc = raw * wj[3:4]
        for sft in (1, 2, 3):
            off = base - sft if d == 0 else sft
            acc = acc + prevraw[d, j, off:off + CHUNK, :] * wj[3 - sft:4 - sft]
        if d == 0:
            prevraw[d, j, 0:8, :] = prevraw[d, j, CHUNK:CHUNK + 8, :]
        else:
            prevraw[d, j, CHUNK:CHUNK + 8, :] = prevraw[d, j, 0:8, :]
        return acc * _sigmoid(acc)

    def pair_piece(g):
        def run():
            y2 = jnp.concatenate([conv_slab(2 * g), conv_slab(2 * g + 1)], axis=1)
            if g < 2 * D // pair_w:
                bi = lax.broadcasted_iota(jnp.int32, (pair_w, pair_w), 0) // DK
                bj = lax.broadcasted_iota(jnp.int32, (pair_w, pair_w), 1) // DK
                ss = jnp.dot((y2 * y2).astype(BF16), (bi == bj).astype(BF16), preferred_element_type=F32)
                scale = (DK ** -0.5) if g < D // pair_w else 1.0
                y2 = y2 * lax.rsqrt(ss + EPS) * scale
            ybuf[slot, d, :, g * pair_w:(g + 1) * pair_w] = y2
        return run

    def gate_piece():
        abv = jnp.where(is_ctx, ab_c[...], ab_l[...])
        rp = lax.broadcasted_iota(jnp.int32, (2 * CHUNK, CHUNK), 0)
        cp = lax.broadcasted_iota(jnp.int32, (2 * CHUNK, CHUNK), 1)
        if d == 0:
            m_pad = ((rp >= cp) & (rp < CHUNK)).astype(F32)
        else:
            m_pad = ((rp <= cp) & (rp < CHUNK)).astype(F32)
        g_all = -jnp.exp(alog_ref[...]) * _softplus(abv + dtb_ref[...])
        gcum = jnp.dot(m_pad, g_all, preferred_element_type=F32, precision=HIGHEST)
        gbuf[slot, d, 0] = gcum
        gbuf[slot, d, 1] = gcum.T
        gbuf[slot, d, 2, 0:CHUNK, :] = _sigmoid(abv)

    return [gate_piece] + [pair_piece(g) for g in range(3 * D // pair_w)]


def _delta_heads(out_refs, state, ybuf, gbuf, slot):
    ri = lax.broadcasted_iota(jnp.int32, (CHUNK, CHUNK), 0)
    ci = lax.broadcasted_iota(jnp.int32, (CHUNK, CHUNK), 1)
    incl = (ri >= ci, ri <= ci)
    strict = (ri > ci, ri < ci)
    last = (CHUNK - 1, 0)
    eye = (ri == ci).astype(F32)
    keep = lax.broadcasted_iota(jnp.int32, (CHUNK, 2 * CHUNK), 1) >= CHUNK
    zeros_half = jnp.zeros((CHUNK, DK), F32)
    gcum = (gbuf[slot, 0, 0], gbuf[slot, 1, 0])
    gcum_t = (gbuf[slot, 0, 1], gbuf[slot, 1, 1])
    beta_all = (gbuf[slot, 0, 2, 0:CHUNK, :], gbuf[slot, 1, 2, 0:CHUNK, :])
    q_of = lambda c: ybuf[slot, c[0], :, c[1] * DK:(c[1] + 1) * DK]
    k_of = lambda c: ybuf[slot, c[0], :, D + c[1] * DK:D + (c[1] + 1) * DK]
    v_of = lambda c: ybuf[slot, c[0], :, 2 * D + c[1] * DK:2 * D + (c[1] + 1) * DK]
    lane_of = lambda c: c[0] * HEADS + c[1]
    gcol_of = lambda c: gcum[c[0]][0:CHUNK, lane_of(c):lane_of(c) + 1]
    bcol_of = lambda c: beta_all[c[0]][:, 2 * HEADS + lane_of(c):2 * HEADS + lane_of(c) + 1]

    chains = [(d, h) for h in range(HEADS) for d in (0, 1)]
    qmat, qkd, sol, wqh, v_new = {}, {}, {}, {}, {}

    def first(c):
        d = c[0]
        grow = gcum_t[d][lane_of(c):lane_of(c) + 1, 0:CHUNK]
        kn = k_of(c)
        knb = kn.astype(BF16)
        a_mat = _nt_dot((kn * bcol_of(c)).astype(BF16), knb)
        qk = _nt_dot(q_of(c).astype(BF16), knb)
        dec = jnp.where(incl[d], jnp.exp(jnp.where(incl[d], gcol_of(c) - grow, 0.0)), 0.0)
        s0 = jnp.where(strict[d], -a_mat * dec, 0.0)
        qmat[c] = jnp.concatenate([s0, eye], axis=1)
        qkd[c] = (qk * dec).astype(BF16)

    def neumann(c):
        qm = qmat[c]
        qmat[c] = jnp.dot(qm[:, 0:CHUNK].astype(BF16), qm.astype(BF16),
                          preferred_element_type=F32) + jnp.where(keep, qm, 0.0)

    def solve(c):
        bcol = bcol_of(c)
        rhs = jnp.concatenate([v_of(c) * bcol, k_of(c) * (bcol * jnp.exp(gcol_of(c)))], axis=1).astype(BF16)
        sol[c] = jnp.dot(qmat[c][:, CHUNK:2 * CHUNK].astype(BF16), rhs, preferred_element_type=F32)

    def apply_state(c):
        wq = jnp.concatenate([sol[c][:, DK:2 * DK], q_of(c) * jnp.exp(gcol_of(c))], axis=0)
        wqh[c] = jnp.dot(wq.astype(BF16), state[c[0], c[1]].astype(BF16), preferred_element_type=F32)

    def output(c):
        v_new[c] = sol[c][:, 0:DK] - wqh[c][0:CHUNK]
        o = wqh[c][CHUNK:2 * CHUNK] + jnp.dot(qkd[c], v_new[c].astype(BF16), preferred_element_type=F32)
        out_refs[c[0]][:, c[1] * DK:(c[1] + 1) * DK] = o.astype(BF16)

    def update_state(c):
        d = c[0]
        gl = gcum[d][last[d]:last[d] + 1, lane_of(c):lane_of(c) + 1]
        kd = k_of(c) * jnp.exp(gl - gcol_of(c))
        kd_t = jnp.concatenate([kd, zeros_half], axis=0).T
        v_pad = jnp.concatenate([v_new[c], zeros_half], axis=0)
        state[c[0], c[1]] = state[c[0], c[1]] * jnp.exp(gl) + jnp.dot(
            kd_t.astype(BF16), v_pad.astype(BF16), preferred_element_type=F32)

    stages = [first] + [neumann] * NEUMANN_STEPS + [solve, apply_state, output, update_state]
    groups = [chains[g:g + CHAIN_GROUP] for g in range(0, len(chains), CHAIN_GROUP)]
    return [functools.partial(stage, c) for group in groups for stage in stages for c in group]


def _delta_kernel(qf_l, kf_l, vf_l, abf_l, qb_l, kb_l, vb_l, abb_l,
                  qf_c, kf_c, vf_c, abf_c, qb_c, kb_c, vb_c, abb_c,
                  cw_ref, alog_ref, dtb_ref, of_ref, ob_ref, state, prevraw, ybuf, gbuf):
    s = pl.program_id(1)
    is_ctx = s < N_CTX_CHUNKS
    slot_p = s % 2
    slot_h = 1 - slot_p

    @pl.when(s == 0)
    def _():
        state[...] = jnp.zeros(state.shape, F32)
        ybuf[1] = jnp.zeros(ybuf.shape[1:], F32)
        gbuf[1] = jnp.zeros(gbuf.shape[1:], F32)
        gbuf[0] = jnp.zeros(gbuf.shape[1:], F32)

    @pl.when((s == 0) | (s == N_CTX_CHUNKS))
    def _():
        prevraw[...] = jnp.zeros(prevraw.shape, F32)

    prep = (_delta_prep(0, is_ctx, (qf_l, kf_l, vf_l, abf_l), (qf_c, kf_c, vf_c, abf_c), prevraw,
                        cw_ref, alog_ref, dtb_ref, ybuf, gbuf, slot_p)
            + _delta_prep(1, is_ctx, (qb_l, kb_l, vb_l, abb_l), (qb_c, kb_c, vb_c, abb_c), prevraw,
                          cw_ref, alog_ref, dtb_ref, ybuf, gbuf, slot_p))
    heads = _delta_heads((of_ref, ob_ref), state, ybuf, gbuf, slot_h)
    every = len(heads) // len(prep)
    for n, piece in enumerate(heads):
        piece()
        if n % every == every - 1 and prep:
            prep.pop(0)()
    for piece in prep:
        piece()


def _delta(qkv_col, ab_col, o_ctx, ab_ctx, conv_w, alog_l, dtb_l, *, b, s_lat, t_ctx):
    ncol = s_lat // CHUNK
    assert s_lat // GRID_W == CHUNK and t_ctx == N_CTX_CHUNKS * CHUNK

    lat_i = lambda s: jnp.clip(s - N_CTX_CHUNKS, 0, ncol - 1)
    col_f = lambda bb, s: bb * ncol + lat_i(s)
    col_b = lambda bb, s: bb * ncol + ncol - 1 - lat_i(s)
    cch_f = lambda bb, s: bb * N_CTX_CHUNKS + jnp.minimum(s, N_CTX_CHUNKS - 1)
    cch_b = lambda bb, s: bb * N_CTX_CHUNKS + N_CTX_CHUNKS - 1 - jnp.minimum(s, N_CTX_CHUNKS - 1)

    def tok_spec(j, fn):
        return pl.BlockSpec((1, CHUNK, D), lambda bb, s: (j, fn(bb, s), 0))

    ab_spec = lambda fn: pl.BlockSpec((CHUNK, LANES), lambda bb, s: (fn(bb, s), 0))
    full = lambda shape: pl.BlockSpec(shape, lambda bb, s: (0,) * len(shape))

    in_specs = ([tok_spec(0, col_f), tok_spec(1, col_f), tok_spec(2, col_f), ab_spec(col_f),
                 tok_spec(0, col_b), tok_spec(1, col_b), tok_spec(2, col_b), ab_spec(col_b),
                 tok_spec(2, cch_f), tok_spec(3, cch_f), tok_spec(4, cch_f), ab_spec(cch_f),
                 tok_spec(2, cch_b), tok_spec(3, cch_b), tok_spec(4, cch_b), ab_spec(cch_b),
                 full((2, 4, 3 * D)), full((1, LANES)), full((1, LANES))])
    out_sds = jax.ShapeDtypeStruct((b * s_lat, D), BF16)
    out_specs = [pl.BlockSpec((CHUNK, D), lambda bb, s: (col_f(bb, s - 1), 0)),
                 pl.BlockSpec((CHUNK, D), lambda bb, s: (col_b(bb, s - 1), 0))]
    return pl.pallas_call(
        _delta_kernel,
        grid=(b, N_CTX_CHUNKS + ncol + 1),
        in_specs=in_specs,
        out_specs=out_specs,
        out_shape=[out_sds, out_sds],
        scratch_shapes=[pltpu.VMEM((2, HEADS, DK, DK), F32), pltpu.VMEM((2, CONV_SLABS, CHUNK + 8, LANES), F32),
                        pltpu.VMEM((2, 2, CHUNK, 3 * D), F32), pltpu.VMEM((2, 2, 3, 2 * CHUNK, LANES), F32)],
        compiler_params=_cparams(("arbitrary", "arbitrary")),
        name="delta",
    )(qkv_col, qkv_col, qkv_col, ab_col, qkv_col, qkv_col, qkv_col, ab_col,
      o_ctx, o_ctx, o_ctx, ab_ctx, o_ctx, o_ctx, o_ctx, ab_ctx,
      conv_w, alog_l, dtb_l)


def _rms(x, g):
    return x * lax.rsqrt(jnp.mean(x * x, axis=-1, keepdims=True) + EPS) * g


def _store_row_tiles(ref, val, n):
    for s in range(ROW_SUB):
        ref[pl.ds(s, n, stride=ROW_SUB), :] = val[:, s * LANES:(s + 1) * LANES]


def _load_row_tiles(ref, n):
    return jnp.concatenate([ref[pl.ds(s, n, stride=ROW_SUB), :] for s in range(ROW_SUB)], axis=1)


def _scan_rows(ref, b, tm):
    return jnp.concatenate(
        [ref[q, pl.ds(b * 2 + half, tm, stride=SCAN_ROWS), :] for half in (0, 1) for q in range(SCAN_SLABS)],
        axis=1)


def _post_kernel(x_ref, gate_ref, z_ref, grg_ref, gdn_ref, hf_ref, hb_ref, of_ref, ob_ref, mod_ref,
                 wrg_ref, wdn_ref, wout_ref, dng_ref, postg_ref, preg_ref, rw_ref, rb_ref,
                 x1_ref, hf2_ref, topi_ref, topw_ref, rank_ref, cnt_ref, carry, *, tm):
    b = pl.program_id(1)

    @pl.when((pl.program_id(0) == 0) & (b == 0))
    def _():
        carry[...] = jnp.zeros(carry.shape, F32)

    gate = gate_ref[0].astype(F32)
    gelu = 0.5 * gate * (1.0 + jnp.tanh(0.7978845608028654 * (gate + 0.044715 * gate * gate * gate)))
    rg_h = _scan_rows(hf_ref, b, tm) + _scan_rows(hb_ref, b, tm)
    y_rg = jnp.dot((rg_h * gelu).astype(BF16), wrg_ref[...], preferred_element_type=F32)

    dn = of_ref[...].astype(F32) + ob_ref[...].astype(F32)
    z = z_ref[0].astype(F32)
    parts = []
    for h in range(HEADS):
        seg = dn[:, h * DK:(h + 1) * DK]
        parts.append(seg * lax.rsqrt(jnp.mean(seg * seg, axis=-1, keepdims=True) + EPS))
    o = jnp.concatenate(parts, axis=1) * dng_ref[...] * (z * _sigmoid(z))
    y_dn = jnp.dot(o.astype(BF16), wdn_ref[...], preferred_element_type=F32)

    merged = _sigmoid(grg_ref[0].astype(F32)) * y_rg + _sigmoid(gdn_ref[0].astype(F32)) * y_dn
    y = jnp.dot(merged.astype(BF16), wout_ref[...], preferred_element_type=F32)

    g1 = mod_ref[pl.ds(b, 1), 2 * D:3 * D]
    sh2 = mod_ref[pl.ds(b, 1), 3 * D:4 * D]
    sc2 = mod_ref[pl.ds(b, 1), 4 * D:5 * D]
    x1 = x_ref[...] + g1 * _rms(y, postg_ref[...])
    x1_ref[...] = x1
    hf2 = _rms(x1, preg_ref[...]) * (1.0 + sc2) + sh2
    _store_row_tiles(hf2_ref, hf2, tm)

    lane = lax.broadcasted_iota(jnp.int32, (tm, LANES), 1)
    lane_f = lane.astype(F32)
    h_hi = hf2.astype(BF16)
    h_lo = (hf2 - h_hi.astype(F32)).astype(BF16)
    rw2 = rw_ref[...]
    hh = jnp.dot(h_hi, rw2, preferred_element_type=F32)
    logits = (hh[:, 0:LANES] + hh[:, LANES:2 * LANES]
              + jnp.dot(h_lo, rw2[:, 0:LANES], preferred_element_type=F32) + rb_ref[...])
    neg = jnp.float32(-jnp.inf)
    lg = jnp.where(lane < N_EXPERTS, logits, neg)
    vals, sels = [], []
    topi = jnp.zeros((tm, LANES), F32)
    onehot = jnp.zeros((tm, LANES), F32)
    for k in range(TOP_K):
        m = jnp.max(lg, axis=-1, keepdims=True)
        idx = jnp.min(jnp.where(lg == m, lane_f, float(LANES)), axis=-1, keepdims=True)
        sel = lane_f == idx
        vals.append(m)
        sels.append(sel)
        lg = jnp.where(sel, neg, lg)
        onehot = onehot + sel.astype(F32)
        topi = topi + jnp.where(lane == k, idx, 0.0)
    es = [jnp.exp(v - vals[0]) for v in vals]
    den = es[0] + es[1] + es[2] + es[3]
    topw = jnp.zeros((tm, LANES), F32)
    for k in range(TOP_K):
        topw = topw + jnp.where(lane == k, es[k] / den, 0.0)

    ri = lax.broadcasted_iota(jnp.int32, (tm, tm), 0)
    ci = lax.broadcasted_iota(jnp.int32, (tm, tm), 1)
    tri = (ri > ci).astype(BF16)
    cum = jnp.dot(tri, onehot.astype(BF16), preferred_element_type=F32) + carry[...]
    rank = jnp.zeros((tm, LANES), F32)
    for k in range(TOP_K):
        rk = jnp.sum(jnp.where(sels[k], cum, 0.0), axis=-1, keepdims=True)
        rank = rank + jnp.where(lane == k, rk, 0.0)
    new_carry = carry[...] + jnp.sum(onehot, axis=0, keepdims=True)
    carry[...] = new_carry
    topi_ref[...] = topi.astype(jnp.int32)
    topw_ref[...] = topw
    rank_ref[...] = rank.astype(jnp.int32)
    cnt_ref[...] = jnp.broadcast_to(new_carry, (8, LANES)).astype(jnp.int32)


def _post(x2d, o_lat, hf, hb, of, ob, mod, wrg, wdn, wout, dng, postg, preg, rw, rb, *, tm, s_lat):
    n = x2d.shape[0]
    tpb = s_lat // tm
    nb = n // s_lat
    kern = functools.partial(_post_kernel, tm=tm)
    tok = pl.BlockSpec((tm, D), lambda t, b: (b * tpb + t, 0))
    oj = lambda j: pl.BlockSpec((1, tm, D), lambda t, b: (j, b * tpb + t, 0))
    hspec = pl.BlockSpec((SCAN_SLABS, tm * SCAN_ROWS, LANES), lambda t, b: (0, t, 0))
    full = lambda shape: pl.BlockSpec(shape, lambda t, b: (0,) * len(shape))
    lane_out = pl.BlockSpec((tm, LANES), lambda t, b: (b * tpb + t, 0))
    return pl.pallas_call(
        kern,
        grid=(tpb, nb),
        in_specs=[tok, oj(1), oj(2), oj(3), oj(4), hspec, hspec, tok, tok, full((8, 6 * D)),
                  full((D, D)), full((D, D)), full((D, D)), full((1, D)), full((1, D)), full((1, D)),
                  full((D, 2 * LANES)), full((1, LANES))],
        out_specs=[tok, pl.BlockSpec((tm * ROW_SUB, LANES), lambda t, b: (b * tpb + t, 0)), lane_out, lane_out,
                   lane_out, pl.BlockSpec((8, LANES), lambda t, b: (0, 0))],
        out_shape=[jax.ShapeDtypeStruct((n, D), F32), jax.ShapeDtypeStruct((n * ROW_SUB, LANES), F32),
                   jax.ShapeDtypeStruct((n, LANES), jnp.int32), jax.ShapeDtypeStruct((n, LANES), F32),
                   jax.ShapeDtypeStruct((n, LANES), jnp.int32), jax.ShapeDtypeStruct((8, LANES), jnp.int32)],
        scratch_shapes=[pltpu.VMEM((1, LANES), F32)],
        compiler_params=_cparams(("arbitrary", "arbitrary")),
        name="post_mix",
    )(x2d, o_lat, o_lat, o_lat, o_lat, hf, hb, of, ob, mod, wrg, wdn, wout, dng, postg, preg, rw, rb)


def _idx_copy(pos_hbm, idx, isem, tile, slot):
    n_idx = pos_hbm.shape[1]
    return pltpu.make_async_copy(pos_hbm.at[tile], idx.at[pl.ds(pl.multiple_of(slot * n_idx, n_idx), n_idx)],
                                 isem.at[slot])


def _row_dma_loop(idx, slot, tb, issue):
    base = slot * (tb * TOP_K)

    def body(rt, c):
        r0 = pl.multiple_of(rt * 8, 8)
        i0 = base + rt * (8 * TOP_K)
        for j in range(8):
            for k in range(TOP_K):
                issue(r0 + j, k, idx[i0 + j * TOP_K + k])
        return c

    lax.fori_loop(0, tb // 8, body, 0)


def _dispatch_kernel(starts_ref, ends_ref, pos_hbm, x_ref, xs_hbm, idx, zbuf, sem, isem, *, tb):
    i = pl.program_id(0)
    n = pl.num_programs(0)
    slot = i % 2

    @pl.when(i == 0)
    def _():
        _idx_copy(pos_hbm, idx, isem, 0, 0).start()
        zbuf[...] = jnp.zeros(zbuf.shape, F32)

        def zero_copy(e):
            r0 = pl.multiple_of(ends_ref[e] - MOE_TM, MOE_TM)
            return pltpu.make_async_copy(zbuf, xs_hbm.at[pl.ds(r0, MOE_TM)], sem)

        for e in range(N_EXPERTS):
            @pl.when(ends_ref[e] > starts_ref[e])
            def _():
                zero_copy(e).start()
        for e in range(N_EXPERTS):
            @pl.when(ends_ref[e] > starts_ref[e])
            def _():
                zero_copy(e).wait()

        def tail_copy(t):
            return pltpu.make_async_copy(zbuf, xs_hbm.at[pl.ds(pl.multiple_of(t * MOE_TM, MOE_TM), MOE_TM)], sem)

        n_used = ends_ref[N_EXPERTS - 1] // MOE_TM
        n_tiles = xs_hbm.shape[0] // MOE_TM

        def tail_start(t, c):
            tail_copy(t).start()
            return c

        def tail_wait(t, c):
            tail_copy(t).wait()
            return c

        lax.fori_loop(n_used, n_tiles, tail_start, 0)
        lax.fori_loop(n_used, n_tiles, tail_wait, 0)

    @pl.when(i + 1 < n)
    def _():
        _idx_copy(pos_hbm, idx, isem, i + 1, 1 - slot).start()

    _idx_copy(pos_hbm, idx, isem, i, slot).wait()

    def issue(r, k, p):
        pltpu.make_async_copy(x_ref.at[r], xs_hbm.at[p], sem).start(priority=k % 2)

    _row_dma_loop(idx, slot, tb, issue)
    for k in range(TOP_K):
        pltpu.make_async_copy(x_ref, xs_hbm.at[pl.ds(0, tb)], sem).wait()


def _dispatch(starts, ends, pos2d, xp, n_rows, *, tb):
    n = xp.shape[0]
    grid_spec = pltpu.PrefetchScalarGridSpec(
        num_scalar_prefetch=2,
        grid=(n // tb,),
        in_specs=[pl.BlockSpec(memory_space=pl.ANY),
                  pl.BlockSpec((tb, ROW_SUB, LANES), lambda i, st, en: (i, 0, 0))],
        out_specs=pl.BlockSpec(memory_space=pl.ANY),
        scratch_shapes=[pltpu.SMEM((2 * tb * TOP_K,), jnp.int32), pltpu.VMEM((MOE_TM, ROW_SUB, LANES), F32),
                        pltpu.SemaphoreType.DMA, pltpu.SemaphoreType.DMA((2,))],
    )
    return pl.pallas_call(
        functools.partial(_dispatch_kernel, tb=tb),
        grid_spec=grid_spec,
        out_shape=jax.ShapeDtypeStruct((n_rows, ROW_SUB, LANES), F32),
        compiler_params=_cparams(("arbitrary",)),
        name="dispatch",
    )(starts, ends, pos2d, xp)


def _expert_kernel(te_ref, nu_ref, xs_ref, w1_ref, b1_ref, w2_ref, b2_ref, ys_ref, w1b, w2b):
    i = pl.program_id(0)

    @pl.when(i < nu_ref[0])
    def _():
        changed = (i == 0) | (te_ref[i] != te_ref[jnp.maximum(i - 1, 0)])

        @pl.when(changed)
        def _():
            w1b[...] = w1_ref[0].astype(BF16)
            w2b[...] = w2_ref[0].astype(BF16)

        x = _load_row_tiles(xs_ref, MOE_TM).astype(BF16)
        hh = jnp.dot(x, w1b[...], preferred_element_type=F32) + b1_ref[0]
        glu = jnp.minimum(hh[:, 0:D], SWIGLU_LIMIT)
        lin = jnp.clip(hh[:, D:2 * D], -SWIGLU_LIMIT, SWIGLU_LIMIT)
        act = glu * _sigmoid(SWIGLU_ALPHA * glu) * (lin + 1.0)
        out = jnp.dot(act.astype(BF16), w2b[...], preferred_element_type=F32) + b2_ref[0]
        _store_row_tiles(ys_ref, out, MOE_TM)

    @pl.when(i >= nu_ref[0])
    def _():
        ys_ref[...] = jnp.zeros(ys_ref.shape, F32)


def _experts(tile_expert, n_used, xs, w1, b1, w2, b2):
    n_rows = xs.shape[0] // ROW_SUB
    n_tiles = n_rows // MOE_TM
    row_map = lambda i, te, nu: (jnp.minimum(i, nu[0] - 1), 0)
    grid_spec = pltpu.PrefetchScalarGridSpec(
        num_scalar_prefetch=2,
        grid=(n_tiles,),
        in_specs=[pl.BlockSpec((MOE_TM * ROW_SUB, LANES), row_map),
                  pl.BlockSpec((1, D, 2 * D), lambda i, te, nu: (te[i], 0, 0)),
                  pl.BlockSpec((1, 1, 2 * D), lambda i, te, nu: (te[i], 0, 0)),
                  pl.BlockSpec((1, D, D), lambda i, te, nu: (te[i], 0, 0)),
                  pl.BlockSpec((1, 1, D), lambda i, te, nu: (te[i], 0, 0))],
        out_specs=pl.BlockSpec((MOE_TM * ROW_SUB, LANES), lambda i, te, nu: (i, 0)),
        scratch_shapes=[pltpu.VMEM((D, 2 * D), BF16), pltpu.VMEM((D, D), BF16)],
    )
    return pl.pallas_call(
        _expert_kernel,
        grid_spec=grid_spec,
        out_shape=jax.ShapeDtypeStruct((n_rows * ROW_SUB, LANES), F32),
        compiler_params=_cparams(("arbitrary",)),
        name="experts",
    )(tile_expert, n_used, xs, w1, b1, w2, b2)


def _final_kernel(pos_hbm, ys_hbm, topw_ref, x1_ref, mod_ref, postg_ref, o_ref, idx, buf, sem, isem,
                  *, tb, tiles_per_batch):
    i = pl.program_id(0)
    n = pl.num_programs(0)
    slot = i % 2

    @pl.when(i == 0)
    def _():
        _idx_copy(pos_hbm, idx, isem, 0, 0).start()

    @pl.when(i + 1 < n)
    def _():
        _idx_copy(pos_hbm, idx, isem, i + 1, 1 - slot).start()

    _idx_copy(pos_hbm, idx, isem, i, slot).wait()

    def issue(r, k, p):
        src = ys_hbm.at[pl.ds(pl.multiple_of(p * ROW_SUB, ROW_SUB), ROW_SUB)]
        dst = buf.at[k, pl.ds(pl.multiple_of(r * ROW_SUB, ROW_SUB), ROW_SUB)]
        pltpu.make_async_copy(src, dst, sem).start(priority=k % 2)

    _row_dma_loop(idx, slot, tb, issue)
    for k in range(TOP_K):
        pltpu.make_async_copy(ys_hbm.at[pl.ds(0, tb * ROW_SUB)], buf.at[k], sem).wait()

    row = i // tiles_per_batch
    tw = topw_ref[...]
    f = _load_row_tiles(buf.at[0], tb) * tw[:, 0:1]
    for k in range(1, TOP_K):
        f = f + _load_row_tiles(buf.at[k], tb) * tw[:, k:k + 1]
    g2 = mod_ref[pl.ds(row, 1), 5 * D:6 * D]
    o_ref[...] = x1_ref[...] + g2 * _rms(f, postg_ref[...])


def _final(pos2d, ys, topw, x1, mod, postg, *, tb, s_lat):
    n = x1.shape[0]
    kern = functools.partial(_final_kernel, tb=tb, tiles_per_batch=s_lat // tb)
    return pl.pallas_call(
        kern,
        grid=(n // tb,),
        in_specs=[pl.BlockSpec(memory_space=pl.ANY),
                  pl.BlockSpec(memory_space=pl.ANY),
                  pl.BlockSpec((tb, LANES), lambda i: (i, 0)),
                  pl.BlockSpec((tb, D), lambda i: (i, 0)),
                  pl.BlockSpec((8, 6 * D), lambda i: (0, 0)),
                  pl.BlockSpec((1, D), lambda i: (0, 0))],
        out_specs=pl.BlockSpec((tb, D), lambda i: (i, 0)),
        out_shape=jax.ShapeDtypeStruct((n, D), F32),
        scratch_shapes=[pltpu.SMEM((2 * tb * TOP_K,), jnp.int32), pltpu.VMEM((TOP_K, tb * ROW_SUB, LANES), F32),
                        pltpu.SemaphoreType.DMA, pltpu.SemaphoreType.DMA((2,))],
        compiler_params=_cparams(("arbitrary",)),
        name="final",
    )(pos2d, ys, topw, x1, mod, postg)


def _block_diag(w):
    w = w.reshape(2, 4, 4, RG_BW, RG_BW)
    eye = jnp.eye(4, dtype=w.dtype)
    return jnp.einsum('dgiab,ij->dgiajb', w, eye).reshape(2, 4, 4 * RG_BW, 4 * RG_BW)


def _grid_transpose(t, b):
    return t.reshape(b, GRID_W, GRID_W, -1).transpose(0, 2, 1, 3).reshape(t.shape)


def kernel(x, c, ctx, c_ctx, ada_w, ada_b, mix_pre_g, mix_post_g, w_in, rg_conv_w, rg_conv_b, rg_wa, rg_ba,
           rg_wi, rg_bi, rg_lam, rg_w_o, dn_conv_w, dn_a_log, dn_dt_bias, dn_norm_g, dn_w_o, w_out,
           ffn_pre_g, ffn_post_g, router_w, router_b, e_w1, e_b1, e_w2, e_b2):
    b, s_lat, _ = x.shape
    t_ctx = ctx.shape[1]
    depth = ada_w.shape[0]
    assert depth == 1 and b * 2 == SCAN_ROWS and s_lat == GRID_W * GRID_W
    n_lat = b * s_lat
    l = 0

    c8 = jnp.zeros((8, D), F32).at[0:b].set(c).at[b].set(c_ctx)
    mod = _ada_mod(c8, ada_w[l], ada_b[l].reshape(1, -1))

    wi = w_in[l]
    blk = lambda j0: wi[:, j0:j0 + D]
    w_raster = jnp.stack([blk(0), blk(D), blk(5 * D), blk(6 * D + 32), blk(7 * D + 32)]).astype(BF16)
    w_qkv = jnp.stack([blk(2 * D), blk(3 * D), blk(4 * D)]).astype(BF16)
    w_all = jnp.stack([blk(0), blk(D), blk(2 * D), blk(3 * D), blk(4 * D)]).astype(BF16)
    w_ab = jnp.pad(wi[:, 6 * D:6 * D + 32], ((0, 0), (0, LANES - 32))).astype(BF16)
    pre_g = mix_pre_g[l].reshape(1, D)
    x2d = x.reshape(n_lat, D)
    x_col = _grid_transpose(x2d, b)
    tpb = s_lat // INPROJ_TM
    o_lat, _ = _inproj(x2d, pre_g, mod, w_raster, w_ab, tm=INPROJ_TM, tiles_per_batch=tpb, row_offset=0)
    qkv_col, ab_col = _inproj(x_col, pre_g, mod, w_qkv, w_ab, tm=INPROJ_TM, tiles_per_batch=tpb, row_offset=0)
    o_ctx, ab_ctx = _inproj(ctx.reshape(b * t_ctx, D), pre_g, mod, w_all, w_ab, tm=b * t_ctx,
                            tiles_per_batch=1, row_offset=b)

    wblk = jnp.stack([_block_diag(rg_wa[l]), _block_diag(rg_wi[l])], axis=1).astype(BF16)
    rg_args = (rg_conv_w[l], rg_conv_b[l], wblk, rg_ba[l], rg_bi[l], rg_lam[l])
    coef_c = _rg_coef(o_ctx[0].reshape(b, t_ctx, D), *rg_args, tt=t_ctx, reset_first=True)
    coef_l = _rg_coef(o_lat[0].reshape(b, s_lat, D), *rg_args, tt=256, reset_first=False)
    hf, hb = _rg_scan(coef_c, coef_l, t_ctx=t_ctx, t_lat=s_lat, tt=256)

    alog_l = jnp.zeros((1, LANES), F32).at[0, 0:2 * HEADS].set(dn_a_log[l].reshape(-1))
    dtb_l = jnp.zeros((1, LANES), F32).at[0, 0:2 * HEADS].set(dn_dt_bias[l].reshape(-1))
    of, ob = _delta(qkv_col, ab_col, o_ctx, ab_ctx, dn_conv_w[l], alog_l, dtb_l, b=b, s_lat=s_lat, t_ctx=t_ctx)
    of = _grid_transpose(of, b)
    ob = _grid_transpose(ob, b)

    rw = jnp.pad(router_w[l], ((0, 0), (0, LANES - N_EXPERTS)))
    rw_hi = rw.astype(BF16)
    rw = jnp.concatenate([rw_hi, (rw - rw_hi.astype(F32)).astype(BF16)], axis=1)
    rb = jnp.pad(router_b[l].reshape(1, -1), ((0, 0), (0, LANES - N_EXPERTS)))
    x1, hf2, topi, topw, rank, cnt = _post(
        x2d, o_lat, hf, hb, of, ob, mod,
        rg_w_o[l].astype(BF16), dn_w_o[l].astype(BF16), w_out[l].astype(BF16),
        jnp.tile(dn_norm_g[l], HEADS).reshape(1, D), mix_post_g[l].reshape(1, D), ffn_pre_g[l].reshape(1, D),
        rw, rb, tm=256, s_lat=s_lat)

    counts = cnt[0, 0:N_EXPERTS]
    padded = ((counts + MOE_TM - 1) // MOE_TM) * MOE_TM
    ends = jnp.cumsum(padded)
    starts = ends - padded
    top_i = topi[:, 0:TOP_K]
    pos2d = (starts[top_i] + rank[:, 0:TOP_K]).reshape(n_lat // MOE_TB, MOE_TB * TOP_K)
    n_tiles = n_lat * TOP_K // MOE_TM + N_EXPERTS
    tile_start = jnp.arange(n_tiles, dtype=jnp.int32) * MOE_TM
    tile_expert = jnp.minimum(jnp.sum(tile_start[:, None] >= ends[None, :], axis=1), N_EXPERTS - 1).astype(jnp.int32)
    n_used = (ends[-1] // MOE_TM).astype(jnp.int32).reshape(1)

    xs = _dispatch(starts, ends, pos2d, hf2.reshape(n_lat, ROW_SUB, LANES), n_tiles * MOE_TM, tb=MOE_TB)
    xs = xs.reshape(n_tiles * MOE_TM * ROW_SUB, LANES)
    ys = _experts(tile_expert, n_used, xs, e_w1[l], e_b1[l].reshape(N_EXPERTS, 1, -1),
                  e_w2[l], e_b2[l].reshape(N_EXPERTS, 1, -1))
    out = _final(pos2d, ys, topw, x1, mod, ffn_post_g[l].reshape(1, D), tb=MOE_TB, s_lat=s_lat)
    return out.reshape(b, s_lat, D)
```

```python
import functools

import jax
import jax.numpy as jnp
from jax import lax
from jax.experimental import pallas as pl
from jax.experimental.pallas import tpu as pltpu

F32 = jnp.float32
BF16 = jnp.bfloat16
U32 = jnp.uint32
HIGHEST = lax.Precision.HIGHEST

D = 1024
EPS = 1e-6
GRID_W = 64
CHUNK = 64
HEADS = 8
DK = 128
RG_C = 8.0
RG_BW = 64
N_EXPERTS = 32
TOP_K = 4
SWIGLU_LIMIT = 7.0
SWIGLU_ALPHA = 1.702
LANES = 128
INPROJ_TM = 2048
MOE_TM = 512
MOE_TB = 512
ROW_SUB = D // LANES
VMEM_LIMIT = 56 * 1024 * 1024


def _cparams(sem):
    return pltpu.CompilerParams(dimension_semantics=sem, vmem_limit_bytes=VMEM_LIMIT)


def _sigmoid(x):
    return 1.0 / (1.0 + jnp.exp(-x))


def _softplus(y):
    return jnp.maximum(y, 0.0) + jnp.log1p(jnp.exp(-jnp.abs(y)))


def _nt_dot(a, b):
    return lax.dot_general(a, b, (((1,), (1,)), ((), ())), preferred_element_type=F32)


def _ada_kernel(c_ref, w_ref, b_ref, o_ref):
    c = c_ref[...]
    a = c * _sigmoid(c)
    o_ref[...] = jnp.dot(a, w_ref[...], preferred_element_type=F32, precision=HIGHEST) + b_ref[...]


def _ada_mod(c8, ada_w, ada_b):
    n = ada_w.shape[1]
    tn = 1024
    return pl.pallas_call(
        _ada_kernel,
        grid=(n // tn,),
        in_specs=[pl.BlockSpec((8, D), lambda j: (0, 0)),
                  pl.BlockSpec((D, tn), lambda j: (0, j)),
                  pl.BlockSpec((1, tn), lambda j: (0, j))],
        out_specs=pl.BlockSpec((8, tn), lambda j: (0, j)),
        out_shape=jax.ShapeDtypeStruct((8, n), F32),
        compiler_params=_cparams(("arbitrary",)),
        name="ada_mod",
    )(c8, ada_w, ada_b)


def _inproj_kernel(x_ref, g_ref, mod_ref, w_ref, wab_ref, o_ref, ab_ref, h_scr, *, tiles_per_batch, row_offset):
    i = pl.program_id(0)
    j = pl.program_id(1)

    @pl.when(j == 0)
    def _():
        x = x_ref[...]
        y = x * lax.rsqrt(jnp.mean(x * x, axis=-1, keepdims=True) + EPS) * g_ref[...]
        row = row_offset + i // tiles_per_batch
        sh = mod_ref[pl.ds(row, 1), 0:D]
        sc = mod_ref[pl.ds(row, 1), D:2 * D]
        h = (y * (1.0 + sc) + sh).astype(BF16)
        h_scr[...] = h
        ab_ref[...] = jnp.dot(h, wab_ref[...], preferred_element_type=F32)

    o_ref[0] = jnp.dot(h_scr[...], w_ref[0], preferred_element_type=F32).astype(BF16)


def _inproj(x2d, g, mod, w_main, w_ab, *, tm, tiles_per_batch, row_offset):
    n = x2d.shape[0]
    nj = w_main.shape[0]
    kern = functools.partial(_inproj_kernel, tiles_per_batch=tiles_per_batch, row_offset=row_offset)
    return pl.pallas_call(
        kern,
        grid=(n // tm, nj),
        in_specs=[pl.BlockSpec((tm, D), lambda i, j: (i, 0)),
                  pl.BlockSpec((1, D), lambda i, j: (0, 0)),
                  pl.BlockSpec((8, 6 * D), lambda i, j: (0, 0)),
                  pl.BlockSpec((1, D, D), lambda i, j: (j, 0, 0)),
                  pl.BlockSpec((D, LANES), lambda i, j: (0, 0))],
        out_specs=[pl.BlockSpec((1, tm, D), lambda i, j: (j, i, 0)),
                   pl.BlockSpec((tm, LANES), lambda i, j: (i, 0))],
        out_shape=[jax.ShapeDtypeStruct((nj, n, D), BF16),
                   jax.ShapeDtypeStruct((n, LANES), F32)],
        scratch_shapes=[pltpu.VMEM((tm, D), BF16)],
        compiler_params=_cparams(("arbitrary", "arbitrary")),
        name="inproj",
    )(x2d, g, mod, w_main, w_ab)


RG_HALO = 16
SCAN_ROWS = 8
SCAN_SLABS = 4


def _pack_bf16_pair(hi, lo):
    hi_bits = lax.bitcast_convert_type(hi.astype(BF16).astype(F32), U32)
    lo_bits = lax.bitcast_convert_type(lo.astype(BF16).astype(F32), U32)
    return hi_bits | (lo_bits >> 16)


def _rg_coef_kernel(cur_ref, prev_ref, next_ref, cw_ref, cb_ref, wblk_ref, ba_ref, bi_ref, lam_ref,
                    wf_ref, wb_ref, xbuf, *, tt, nt, reset_first):
    i = pl.program_id(0)
    b = pl.program_id(1)
    x = cur_ref[0].astype(F32)
    xp = jnp.where(i > 0, prev_ref[0].astype(F32), 0.0)
    xn = jnp.where(i < nt - 1, next_ref[0].astype(F32), 0.0)
    n_slab = D // LANES
    for j in range(n_slab):
        ls = slice(j * LANES, (j + 1) * LANES)
        xbuf[j, 0:RG_HALO, :] = xp[:, ls]
        xbuf[j, RG_HALO:RG_HALO + tt, :] = x[:, ls]
        xbuf[j, RG_HALO + tt:2 * RG_HALO + tt, :] = xn[:, ls]
    row = lax.broadcasted_iota(jnp.int32, (tt, 1), 0)
    outs = (wf_ref, wb_ref)
    for d in (0, 1):
        w = cw_ref[d]
        cols = []
        for j in range(n_slab):
            ls = slice(j * LANES, (j + 1) * LANES)
            acc = x[:, ls] * w[3:4, ls]
            for sft in (1, 2, 3):
                off = RG_HALO - sft if d == 0 else RG_HALO + sft
                acc = acc + xbuf[j, off:off + tt, :] * w[3 - sft:4 - sft, ls]
            cols.append(acc)
        xc = jnp.concatenate(cols, axis=1) + cb_ref[d:d + 1]
        xcb = xc.astype(BF16)
        pre_a = jnp.concatenate(
            [jnp.dot(xcb[:, g * 256:(g + 1) * 256], wblk_ref[d, 0, g], preferred_element_type=F32)
             for g in range(4)], axis=1) + ba_ref[d:d + 1]
        pre_i = jnp.concatenate(
            [jnp.dot(xcb[:, g * 256:(g + 1) * 256], wblk_ref[d, 1, g], preferred_element_type=F32)
             for g in range(4)], axis=1) + bi_ref[d:d + 1]
        r = _sigmoid(pre_a)
        ig = _sigmoid(pre_i)
        la = r * (-RG_C * _softplus(-lam_ref[d:d + 1]))
        a = jnp.exp(la)
        om = 1.0 - a * a
        mult = jnp.where(om > 0.0, om * lax.rsqrt(om), 0.0)
        if reset_first:
            if d == 0:
                first = (row == 0) & (i == 0)
            else:
                first = (row == tt - 1) & (i == nt - 1)
            mult = jnp.where(first, 1.0, mult)
        word = _pack_bf16_pair(la, mult * ig * xc)
        for half in (0, 1):
            for q in range(SCAN_SLABS):
                c0 = (half * SCAN_SLABS + q) * LANES
                outs[d][q, pl.ds(b * 2 + half, tt, stride=SCAN_ROWS), :] = word[:, c0:c0 + LANES]


def _rg_coef(xr, conv_w, conv_b, wblk, ba, bi, lam, *, tt, reset_first):
    b, t, _ = xr.shape
    nt = t // tt
    hb = tt // RG_HALO
    kern = functools.partial(_rg_coef_kernel, tt=tt, nt=nt, reset_first=reset_first)
    full = lambda shape: pl.BlockSpec(shape, lambda i, bb: (0,) * len(shape))
    out_sds = jax.ShapeDtypeStruct((SCAN_SLABS, t * SCAN_ROWS, LANES), U32)
    out_spec = pl.BlockSpec((SCAN_SLABS, tt * SCAN_ROWS, LANES), lambda i, bb: (0, i, 0))
    return pl.pallas_call(
        kern,
        grid=(nt, b),
        in_specs=[pl.BlockSpec((1, tt, D), lambda i, bb: (bb, i, 0)),
                  pl.BlockSpec((1, RG_HALO, D), lambda i, bb: (bb, jnp.maximum(i * hb - 1, 0), 0)),
                  pl.BlockSpec((1, RG_HALO, D), lambda i, bb: (bb, jnp.minimum((i + 1) * hb, t // RG_HALO - 1), 0)),
                  full((2, 4, D)), full((2, D)), full((2, 2, 4, 256, 256)),
                  full((2, D)), full((2, D)), full((2, D))],
        out_specs=[out_spec, out_spec],
        out_shape=[out_sds, out_sds],
        scratch_shapes=[pltpu.VMEM((D // LANES, tt + 2 * RG_HALO, LANES), F32)],
        compiler_params=_cparams(("arbitrary", "arbitrary")),
        name="rg_coef",
    )(xr, xr, xr, conv_w, conv_b, wblk, ba, bi, lam)


def _scan_step(word, h):
    la = lax.bitcast_convert_type(word & jnp.uint32(0xFFFF0000), F32)
    bb = lax.bitcast_convert_type(word << 16, F32)
    return jnp.exp(la) * h + bb


def _rg_scan_run(wf, wb, hf, hb, carry, n):
    def body(t, c):
        h_f, h_b = c
        r0 = pl.multiple_of(t * SCAN_ROWS, SCAN_ROWS)
        r1 = pl.multiple_of((n - 1 - t) * SCAN_ROWS, SCAN_ROWS)
        new_f, new_b = [], []
        for q in range(SCAN_SLABS):
            f = _scan_step(wf[q, pl.ds(r0, SCAN_ROWS), :], h_f[q])
            g = _scan_step(wb[q, pl.ds(r1, SCAN_ROWS), :], h_b[q])
            if hf is not None:
                hf[q, pl.ds(r0, SCAN_ROWS), :] = f
                hb[q, pl.ds(r1, SCAN_ROWS), :] = g
            new_f.append(f)
            new_b.append(g)
        return tuple(new_f), tuple(new_b)

    init = (tuple(carry[0, q] for q in range(SCAN_SLABS)), tuple(carry[1, q] for q in range(SCAN_SLABS)))
    h_f, h_b = lax.fori_loop(0, n, body, init, unroll=8)
    for q in range(SCAN_SLABS):
        carry[0, q] = h_f[q]
        carry[1, q] = h_b[q]


def _rg_scan_kernel(wf_c, wb_c, wf_l, wb_l, hf_ref, hb_ref, carry, *, t_ctx, tt):
    s = pl.program_id(0)

    @pl.when(s == 0)
    def _():
        carry[...] = jnp.zeros(carry.shape, F32)
        _rg_scan_run(wf_c, wb_c, None, None, carry, t_ctx)

    @pl.when(s > 0)
    def _():
        _rg_scan_run(wf_l, wb_l, hf_ref, hb_ref, carry, tt)


def _rg_scan(coef_c, coef_l, *, t_ctx, t_lat, tt):
    nt = t_lat // tt
    ctx_spec = pl.BlockSpec((SCAN_SLABS, t_ctx * SCAN_ROWS, LANES), lambda s: (0, 0, 0))
    f_spec = pl.BlockSpec((SCAN_SLABS, tt * SCAN_ROWS, LANES), lambda s: (0, jnp.maximum(s - 1, 0), 0))
    b_spec = pl.BlockSpec((SCAN_SLABS, tt * SCAN_ROWS, LANES), lambda s: (0, nt - jnp.maximum(s, 1), 0))
    out_sds = jax.ShapeDtypeStruct((SCAN_SLABS, t_lat * SCAN_ROWS, LANES), F32)
    kern = functools.partial(_rg_scan_kernel, t_ctx=t_ctx, tt=tt)
    return pl.pallas_call(
        kern,
        grid=(nt + 1,),
        in_specs=[ctx_spec, ctx_spec, f_spec, b_spec],
        out_specs=[f_spec, b_spec],
        out_shape=[out_sds, out_sds],
        scratch_shapes=[pltpu.VMEM((2, SCAN_SLABS, SCAN_ROWS, LANES), F32)],
        compiler_params=_cparams(("arbitrary",)),
        name="rg_scan",
    )(coef_c[0], coef_c[1], coef_l[0], coef_l[1])


N_CTX_CHUNKS = 4
CONV_SLABS = 3 * D // LANES
CHAIN_GROUP = 16
NEUMANN_STEPS = 6


def _delta_prep(d, is_ctx, refs_l, refs_c, prevraw, cw_ref, alog_ref, dtb_ref, ybuf, gbuf, slot):
    q_l, k_l, v_l, ab_l = refs_l
    q_c, k_c, v_c, ab_c = refs_c

    srcs = ((q_c, q_l), (k_c, k_l), (v_c, v_l))
    base = 8 if d == 0 else 0
    pair_w = 2 * LANES
    slabs_per_src = D // LANES

    def conv_slab(j):
        rc, rl = srcs[j // slabs_per_src]
        ls = slice((j % slabs_per_src) * LANES, (j % slabs_per_src + 1) * LANES)
        raw = jnp.where(is_ctx, rc[0, :, ls], rl[0, :, ls]).astype(F32)
        prevraw[d, j, base:base + CHUNK, :] = raw
        wj = cw_ref[d, :, j * LANES:(j + 1) * LANES]
        acc = raw * wj[3:4]
        for sft in (1, 2, 3):
            off = base - sft if d == 0 else sft
            acc = acc + prevraw[d, j, off:off + CHUNK, :] * wj[3 - sft:4 - sft]
        if d == 0:
            prevraw[d, j, 0:8, :] = prevraw[d, j, CHUNK:CHUNK + 8, :]
        else:
            prevraw[d, j, CHUNK:CHUNK + 8, :] = prevraw[d, j, 0:8, :]
        return acc * _sigmoid(acc)

    def pair_piece(g):
        def run():
            y2 = jnp.concatenate([conv_slab(2 * g), conv_slab(2 * g + 1)], axis=1)
            if g < 2 * D // pair_w:
                bi = lax.broadcasted_iota(jnp.int32, (pair_w, pair_w), 0) // DK
                bj = lax.broadcasted_iota(jnp.int32, (pair_w, pair_w), 1) // DK
                ss = jnp.dot((y2 * y2).astype(BF16), (bi == bj).astype(BF16), preferred_element_type=F32)
                scale = (DK ** -0.5) if g < D // pair_w else 1.0
                y2 = y2 * lax.rsqrt(ss + EPS) * scale
            ybuf[slot, d, :, g * pair_w:(g + 1) * pair_w] = y2
        return run

    def gate_piece():
        abv = jnp.where(is_ctx, ab_c[...], ab_l[...])
        rp = lax.broadcasted_iota(jnp.int32, (2 * CHUNK, CHUNK), 0)
        cp = lax.broadcasted_iota(jnp.int32, (2 * CHUNK, CHUNK), 1)
        if d == 0:
            m_pad = ((rp >= cp) & (rp < CHUNK)).astype(F32)
        else:
            m_pad = ((rp <= cp) & (rp < CHUNK)).astype(F32)
        g_all = -jnp.exp(alog_ref[...]) * _softplus(abv + dtb_ref[...])
        gcum = jnp.dot(m_pad, g_all, preferred_element_type=F32, precision=HIGHEST)
        gbuf[slot, d, 0] = gcum
        gbuf[slot, d, 1] = gcum.T
        gbuf[slot, d, 2, 0:CHUNK, :] = _sigmoid(abv)

    return [gate_piece] + [pair_piece(g) for g in range(3 * D // pair_w)]


def _delta_heads(out_refs, state, ybuf, gbuf, slot):
    ri = lax.broadcasted_iota(jnp.int32, (CHUNK, CHUNK), 0)
    ci = lax.broadcasted_iota(jnp.int32, (CHUNK, CHUNK), 1)
    incl = (ri >= ci, ri <= ci)
    strict = (ri > ci, ri < ci)
    last = (CHUNK - 1, 0)
    eye = (ri == ci).astype(F32)
    keep = lax.broadcasted_iota(jnp.int32, (CHUNK, 2 * CHUNK), 1) >= CHUNK
    zeros_half = jnp.zeros((CHUNK, DK), F32)
    gcum = (gbuf[slot, 0, 0], gbuf[slot, 1, 0])
    gcum_t = (gbuf[slot, 0, 1], gbuf[slot, 1, 1])
    beta_all = (gbuf[slot, 0, 2, 0:CHUNK, :], gbuf[slot, 1, 2, 0:CHUNK, :])
    q_of = lambda c: ybuf[slot, c[0], :, c[1] * DK:(c[1] + 1) * DK]
    k_of = lambda c: ybuf[slot, c[0], :, D + c[1] * DK:D + (c[1] + 1) * DK]
    v_of = lambda c: ybuf[slot, c[0], :, 2 * D + c[1] * DK:2 * D + (c[1] + 1) * DK]
    lane_of = lambda c: c[0] * HEADS + c[1]
    gcol_of = lambda c: gcum[c[0]][0:CHUNK, lane_of(c):lane_of(c) + 1]
    bcol_of = lambda c: beta_all[c[0]][:, 2 * HEADS + lane_of(c):2 * HEADS + lane_of(c) + 1]

    chains = [(d, h) for h in range(HEADS) for d in (0, 1)]
    qmat, qkd, sol, wqh, v_new = {}, {}, {}, {}, {}

    def first(c):
        d = c[0]
        grow = gcum_t[d][lane_of(c):lane_of(c) + 1, 0:CHUNK]
        kn = k_of(c)
        knb = kn.astype(BF16)
        a_mat = _nt_dot((kn * bcol_of(c)).astype(BF16), knb)
        qk = _nt_dot(q_of(c).astype(BF16), knb)
        dec = jnp.where(incl[d], jnp.exp(jnp.where(incl[d], gcol_of(c) - grow, 0.0)), 0.0)
        s0 = jnp.where(strict[d], -a_mat * dec, 0.0)
        qmat[c] = jnp.concatenate([s0, eye], axis=1)
        qkd[c] = (qk * dec).astype(BF16)

    def neumann(c):
        qm = qmat[c]
        qmat[c] = jnp.dot(qm[:, 0:CHUNK].astype(BF16), qm.astype(BF16),
                          preferred_element_type=F32) + jnp.where(keep, qm, 0.0)

    def solve(c):
        bcol = bcol_of(c)
        rhs = jnp.concatenate([v_of(c) * bcol, k_of(c) * (bcol * jnp.exp(gcol_of(c)))], axis=1).astype(BF16)
        sol[c] = jnp.dot(qmat[c][:, CHUNK:2 * CHUNK].astype(BF16), rhs, preferred_element_type=F32)

    def apply_state(c):
        wq = jnp.concatenate([sol[c][:, DK:2 * DK], q_of(c) * jnp.exp(gcol_of(c))], axis=0)
        wqh[c] = jnp.dot(wq.astype(BF16), state[c[0], c[1]].astype(BF16), preferred_element_type=F32)

    def output(c):
        v_new[c] = sol[c][:, 0:DK] - wqh[c][0:CHUNK]
        o = wqh[c][CHUNK:2 * CHUNK] + jnp.dot(qkd[c], v_new[c].astype(BF16), preferred_element_type=F32)
        out_refs[c[0]][:, c[1] * DK:(c[1] + 1) * DK] = o.astype(BF16)

    def update_state(c):
        d = c[0]
        gl = gcum[d][last[d]:last[d] + 1, lane_of(c):lane_of(c) + 1]
        kd = k_of(c) * jnp.exp(gl - gcol_of(c))
        kd_t = jnp.concatenate([kd, zeros_half], axis=0).T
        v_pad = jnp.concatenate([v_new[c], zeros_half], axis=0)
        state[c[0], c[1]] = state[c[0], c[1]] * jnp.exp(gl) + jnp.dot(
            kd_t.astype(BF16), v_pad.astype(BF16), preferred_element_type=F32)

    stages = [first] + [neumann] * NEUMANN_STEPS + [solve, apply_state, output, update_state]
    groups = [chains[g:g + CHAIN_GROUP] for g in range(0, len(chains), CHAIN_GROUP)]
    return [functools.partial(stage, c) for group in groups for stage in stages for c in group]


def _delta_kernel(qf_l, kf_l, vf_l, abf_l, qb_l, kb_l, vb_l, abb_l,
                  qf_c, kf_c, vf_c, abf_c, qb_c, kb_c, vb_c, abb_c,
                  cw_ref, alog_ref, dtb_ref, of_ref, ob_ref, state, prevraw, ybuf, gbuf):
    s = pl.program_id(1)
    is_ctx = s < N_CTX_CHUNKS
    slot_p = s % 2
    slot_h = 1 - slot_p

    @pl.when(s == 0)
    def _():
        state[...] = jnp.zeros(state.shape, F32)
        ybuf[1] = jnp.zeros(ybuf.shape[1:], F32)
        gbuf[1] = jnp.zeros(gbuf.shape[1:], F32)
        gbuf[0] = jnp.zeros(gbuf.shape[1:], F32)

    @pl.when((s == 0) | (s == N_CTX_CHUNKS))
    def _():
        prevraw[...] = jnp.zeros(prevraw.shape, F32)

    prep = (_delta_prep(0, is_ctx, (qf_l, kf_l, vf_l, abf_l), (qf_c, kf_c, vf_c, abf_c), prevraw,
                        cw_ref, alog_ref, dtb_ref, ybuf, gbuf, slot_p)
            + _delta_prep(1, is_ctx, (qb_l, kb_l, vb_l, abb_l), (qb_c, kb_c, vb_c, abb_c), prevraw,
                          cw_ref, alog_ref, dtb_ref, ybuf, gbuf, slot_p))
    heads = _delta_heads((of_ref, ob_ref), state, ybuf, gbuf, slot_h)
    every = len(heads) // len(prep)
    for n, piece in enumerate(heads):
        piece()
        if n % every == every - 1 and prep:
            prep.pop(0)()
    for piece in prep:
        piece()


def _delta(qkv_col, ab_col, o_ctx, ab_ctx, conv_w, alog_l, dtb_l, *, b, s_lat, t_ctx):
    ncol = s_lat // CHUNK
    assert s_lat // GRID_W == CHUNK and t_ctx == N_CTX_CHUNKS * CHUNK

    lat_i = lambda s: jnp.clip(s - N_CTX_CHUNKS, 0, ncol - 1)
    col_f = lambda bb, s: bb * ncol + lat_i(s)
    col_b = lambda bb, s: bb * ncol + ncol - 1 - lat_i(s)
    cch_f = lambda bb, s: bb * N_CTX_CHUNKS + jnp.minimum(s, N_CTX_CHUNKS - 1)
    cch_b = lambda bb, s: bb * N_CTX_CHUNKS + N_CTX_CHUNKS - 1 - jnp.minimum(s, N_CTX_CHUNKS - 1)

    def tok_spec(j, fn):
        return pl.BlockSpec((1, CHUNK, D), lambda bb, s: (j, fn(bb, s), 0))

    ab_spec = lambda fn: pl.BlockSpec((CHUNK, LANES), lambda bb, s: (fn(bb, s), 0))
    full = lambda shape: pl.BlockSpec(shape, lambda bb, s: (0,) * len(shape))

    in_specs = ([tok_spec(0, col_f), tok_spec(1, col_f), tok_spec(2, col_f), ab_spec(col_f),
                 tok_spec(0, col_b), tok_spec(1, col_b), tok_spec(2, col_b), ab_spec(col_b),
                 tok_spec(2, cch_f), tok_spec(3, cch_f), tok_spec(4, cch_f), ab_spec(cch_f),
                 tok_spec(2, cch_b), tok_spec(3, cch_b), tok_spec(4, cch_b), ab_spec(cch_b),
                 full((2, 4, 3 * D)), full((1, LANES)), full((1, LANES))])
    out_sds = jax.ShapeDtypeStruct((b * s_lat, D), BF16)
    out_specs = [pl.BlockSpec((CHUNK, D), lambda bb, s: (col_f(bb, s - 1), 0)),
                 pl.BlockSpec((CHUNK, D), lambda bb, s: (col_b(bb, s - 1), 0))]
    return pl.pallas_call(
        _delta_kernel,
        grid=(b, N_CTX_CHUNKS + ncol + 1),
        in_specs=in_specs,
        out_specs=out_specs,
        out_shape=[out_sds, out_sds],
        scratch_shapes=[pltpu.VMEM((2, HEADS, DK, DK), F32), pltpu.VMEM((2, CONV_SLABS, CHUNK + 8, LANES), F32),
                        pltpu.VMEM((2, 2, CHUNK, 3 * D), F32), pltpu.VMEM((2, 2, 3, 2 * CHUNK, LANES), F32)],
        compiler_params=_cparams(("arbitrary", "arbitrary")),
        name="delta",
    )(qkv_col, qkv_col, qkv_col, ab_col, qkv_col, qkv_col, qkv_col, ab_col,
      o_ctx, o_ctx, o_ctx, ab_ctx, o_ctx, o_ctx, o_ctx, ab_ctx,
      conv_w, alog_l, dtb_l)


def _rms(x, g):
    return x * lax.rsqrt(jnp.mean(x * x, axis=-1, keepdims=True) + EPS) * g


def _store_row_tiles(ref, val, n):
    for s in range(ROW_SUB):
        ref[pl.ds(s, n, stride=ROW_SUB), :] = val[:, s * LANES:(s + 1) * LANES]


def _load_row_tiles(ref, n):
    return jnp.concatenate([ref[pl.ds(s, n, stride=ROW_SUB), :] for s in range(ROW_SUB)], axis=1)


def _scan_rows(ref, b, tm):
    return jnp.concatenate(
        [ref[q, pl.ds(b * 2 + half, tm, stride=SCAN_ROWS), :] for half in (0, 1) for q in range(SCAN_SLABS)],
        axis=1)


def _post_kernel(x_ref, gate_ref, z_ref, grg_ref, gdn_ref, hf_ref, hb_ref, of_ref, ob_ref, mod_ref,
                 wrg_ref, wdn_ref, wout_ref, dng_ref, postg_ref, preg_ref, rw_ref, rb_ref,
                 x1_ref, hf2_ref, topw_ref, route_ref, cnt_ref, carry, *, tm):
    b = pl.program_id(1)

    @pl.when((pl.program_id(0) == 0) & (b == 0))
    def _():
        carry[...] = jnp.zeros(carry.shape, F32)

    gate = gate_ref[0].astype(F32)
    gelu = 0.5 * gate * (1.0 + jnp.tanh(0.7978845608028654 * (gate + 0.044715 * gate * gate * gate)))
    rg_h = _scan_rows(hf_ref, b, tm) + _scan_rows(hb_ref, b, tm)
    y_rg = jnp.dot((rg_h * gelu).astype(BF16), wrg_ref[...], preferred_element_type=F32)

    dn = of_ref[...].astype(F32) + ob_ref[...].astype(F32)
    z = z_ref[0].astype(F32)
    parts = []
    for h in range(HEADS):
        seg = dn[:, h * DK:(h + 1) * DK]
        parts.append(seg * lax.rsqrt(jnp.mean(seg * seg, axis=-1, keepdims=True) + EPS))
    o = jnp.concatenate(parts, axis=1) * dng_ref[...] * (z * _sigmoid(z))
    y_dn = jnp.dot(o.astype(BF16), wdn_ref[...], preferred_element_type=F32)

    merged = _sigmoid(grg_ref[0].astype(F32)) * y_rg + _sigmoid(gdn_ref[0].astype(F32)) * y_dn
    y = jnp.dot(merged.astype(BF16), wout_ref[...], preferred_element_type=F32)

    g1 = mod_ref[pl.ds(b, 1), 2 * D:3 * D]
    sh2 = mod_ref[pl.ds(b, 1), 3 * D:4 * D]
    sc2 = mod_ref[pl.ds(b, 1), 4 * D:5 * D]
    x1 = x_ref[...] + g1 * _rms(y, postg_ref[...])
    x1_ref[...] = x1
    hf2 = _rms(x1, preg_ref[...]) * (1.0 + sc2) + sh2
    _store_row_tiles(hf2_ref, hf2, tm)

    lane = lax.broadcasted_iota(jnp.int32, (tm, LANES), 1)
    lane_f = lane.astype(F32)
    h_hi = hf2.astype(BF16)
    h_lo = (hf2 - h_hi.astype(F32)).astype(BF16)
    rw2 = rw_ref[...]
    hh = jnp.dot(h_hi, rw2, preferred_element_type=F32)
    logits = (hh[:, 0:LANES] + hh[:, LANES:2 * LANES]
              + jnp.dot(h_lo, rw2[:, 0:LANES], preferred_element_type=F32) + rb_ref[...])
    neg = jnp.float32(-jnp.inf)
    lg = jnp.where(lane < N_EXPERTS, logits, neg)
    vals, sels = [], []
    topi = jnp.zeros((tm, LANES), F32)
    onehot = jnp.zeros((tm, LANES), F32)
    for k in range(TOP_K):
        m = jnp.max(lg, axis=-1, keepdims=True)
        idx = jnp.min(jnp.where(lg == m, lane_f, float(LANES)), axis=-1, keepdims=True)
        sel = lane_f == idx
        vals.append(m)
        sels.append(sel)
        lg = jnp.where(sel, neg, lg)
        onehot = onehot + sel.astype(F32)
        topi = topi + jnp.where(lane == k, idx, 0.0)
    es = [jnp.exp(v - vals[0]) for v in vals]
    den = es[0] + es[1] + es[2] + es[3]
    topw = jnp.zeros((tm, LANES), F32)
    for k in range(TOP_K):
        topw = topw + jnp.where(lane == k, es[k] / den, 0.0)

    ri = lax.broadcasted_iota(jnp.int32, (tm, tm), 0)
    ci = lax.broadcasted_iota(jnp.int32, (tm, tm), 1)
    tri = (ri > ci).astype(BF16)
    cum = jnp.dot(tri, onehot.astype(BF16), preferred_element_type=F32) + carry[...]
    rank = jnp.zeros((tm, LANES), F32)
    for k in range(TOP_K):
        rk = jnp.sum(jnp.where(sels[k], cum, 0.0), axis=-1, keepdims=True)
        rank = rank + jnp.where(lane == TOP_K + k, rk, 0.0)
    new_carry = carry[...] + jnp.sum(onehot, axis=0, keepdims=True)
    carry[...] = new_carry
    topw_ref[...] = topw
    route_ref[...] = (topi + rank).T[0:2 * TOP_K, :].astype(jnp.int32)
    cnt_ref[...] = jnp.broadcast_to(new_carry, (8, LANES)).astype(jnp.int32)


def _post(x2d, o_lat, hf, hb, of, ob, mod, wrg, wdn, wout, dng, postg, preg, rw, rb, *, tm, s_lat):
    n = x2d.shape[0]
    tpb = s_lat // tm
    nb = n // s_lat
    kern = functools.partial(_post_kernel, tm=tm)
    tok = pl.BlockSpec((tm, D), lambda t, b: (b * tpb + t, 0))
    oj = lambda j: pl.BlockSpec((1, tm, D), lambda t, b: (j, b * tpb + t, 0))
    hspec = pl.BlockSpec((SCAN_SLABS, tm * SCAN_ROWS, LANES), lambda t, b: (0, t, 0))
    full = lambda shape: pl.BlockSpec(shape, lambda t, b: (0,) * len(shape))
    lane_out = pl.BlockSpec((tm, LANES), lambda t, b: (b * tpb + t, 0))
    return pl.pallas_call(
        kern,
        grid=(tpb, nb),
        in_specs=[tok, oj(1), oj(2), oj(3), oj(4), hspec, hspec, tok, tok, full((8, 6 * D)),
                  full((D, D)), full((D, D)), full((D, D)), full((1, D)), full((1, D)), full((1, D)),
                  full((D, 2 * LANES)), full((1, LANES))],
        out_specs=[tok, pl.BlockSpec((tm * ROW_SUB, LANES), lambda t, b: (b * tpb + t, 0)), lane_out,
                   pl.BlockSpec((2 * TOP_K, tm), lambda t, b: (0, b * tpb + t)),
                   pl.BlockSpec((8, LANES), lambda t, b: (0, 0))],
        out_shape=[jax.ShapeDtypeStruct((n, D), F32), jax.ShapeDtypeStruct((n * ROW_SUB, LANES), F32),
                   jax.ShapeDtypeStruct((n, LANES), F32), jax.ShapeDtypeStruct((2 * TOP_K, n), jnp.int32),
                   jax.ShapeDtypeStruct((8, LANES), jnp.int32)],
        scratch_shapes=[pltpu.VMEM((1, LANES), F32)],
        compiler_params=_cparams(("arbitrary", "arbitrary")),
        name="post_mix",
    )(x2d, o_lat, o_lat, o_lat, o_lat, hf, hb, of, ob, mod, wrg, wdn, wout, dng, postg, preg, rw, rb)


def _idx_copy(pos_hbm, idx, isem, tile, slot):
    n_idx = pos_hbm.shape[1]
    return pltpu.make_async_copy(pos_hbm.at[tile], idx.at[pl.ds(pl.multiple_of(slot * n_idx, n_idx), n_idx)],
                                 isem.at[slot])


def _row_dma_loop(idx, slot, tb, issue):
    base = slot * (tb * TOP_K)

    def body(rt, c):
        r0 = pl.multiple_of(rt * 8, 8)
        for j in range(8):
            for k in range(TOP_K):
                issue(r0 + j, k, idx[base + k * tb + r0 + j])
        return c

    lax.fori_loop(0, tb // 8, body, 0)


def _dispatch_kernel(starts_ref, ends_ref, pos_hbm, x_ref, xs_hbm, idx, zbuf, sem, isem, *, tb):
    i = pl.program_id(0)
    n = pl.num_programs(0)
    slot = i % 2

    @pl.when(i == 0)
    def _():
        _idx_copy(pos_hbm, idx, isem, 0, 0).start()
        zbuf[...] = jnp.zeros(zbuf.shape, F32)

        def zero_copy(e):
            r0 = pl.multiple_of(ends_ref[e] - MOE_TM, MOE_TM)
            return pltpu.make_async_copy(zbuf, xs_hbm.at[pl.ds(r0, MOE_TM)], sem)

        for e in range(N_EXPERTS):
            @pl.when(ends_ref[e] > starts_ref[e])
            def _():
                zero_copy(e).start()
        for e in range(N_EXPERTS):
            @pl.when(ends_ref[e] > starts_ref[e])
            def _():
                zero_copy(e).wait()

        def tail_copy(t):
            return pltpu.make_async_copy(zbuf, xs_hbm.at[pl.ds(pl.multiple_of(t * MOE_TM, MOE_TM), MOE_TM)], sem)

        n_used = ends_ref[N_EXPERTS - 1] // MOE_TM
        n_tiles = xs_hbm.shape[0] // MOE_TM

        def tail_start(t, c):
            tail_copy(t).start()
            return c

        def tail_wait(t, c):
            tail_copy(t).wait()
            return c

        lax.fori_loop(n_used, n_tiles, tail_start, 0)
        lax.fori_loop(n_used, n_tiles, tail_wait, 0)

    @pl.when(i + 1 < n)
    def _():
        _idx_copy(pos_hbm, idx, isem, i + 1, 1 - slot).start()

    _idx_copy(pos_hbm, idx, isem, i, slot).wait()

    def issue(r, k, p):
        pltpu.make_async_copy(x_ref.at[r], xs_hbm.at[p], sem).start(priority=k % 2)

    _row_dma_loop(idx, slot, tb, issue)
    for k in range(TOP_K):
        pltpu.make_async_copy(x_ref, xs_hbm.at[pl.ds(0, tb)], sem).wait()


def _dispatch(starts, ends, pos2d, xp, n_rows, *, tb):
    n = xp.shape[0]
    grid_spec = pltpu.PrefetchScalarGridSpec(
        num_scalar_prefetch=2,
        grid=(n // tb,),
        in_specs=[pl.BlockSpec(memory_space=pl.ANY),
                  pl.BlockSpec((tb, ROW_SUB, LANES), lambda i, st, en: (i, 0, 0))],
        out_specs=pl.BlockSpec(memory_space=pl.ANY),
        scratch_shapes=[pltpu.SMEM((2 * tb * TOP_K,), jnp.int32), pltpu.VMEM((MOE_TM, ROW_SUB, LANES), F32),
                        pltpu.SemaphoreType.DMA, pltpu.SemaphoreType.DMA((2,))],
    )
    return pl.pallas_call(
        functools.partial(_dispatch_kernel, tb=tb),
        grid_spec=grid_spec,
        out_shape=jax.ShapeDtypeStruct((n_rows, ROW_SUB, LANES), F32),
        compiler_params=_cparams(("arbitrary",)),
        name="dispatch",
    )(starts, ends, pos2d, xp)


def _expert_kernel(te_ref, nu_ref, xs_ref, w1_ref, b1_ref, w2_ref, b2_ref, ys_ref, w1b, w2b):
    i = pl.program_id(0)

    @pl.when(i < nu_ref[0])
    def _():
        changed = (i == 0) | (te_ref[i] != te_ref[jnp.maximum(i - 1, 0)])

        @pl.when(changed)
        def _():
            w1b[...] = w1_ref[0].astype(BF16)
            w2b[...] = w2_ref[0].astype(BF16)

        x = _load_row_tiles(xs_ref, MOE_TM).astype(BF16)
        hh = jnp.dot(x, w1b[...], preferred_element_type=F32) + b1_ref[0]
        glu = jnp.minimum(hh[:, 0:D], SWIGLU_LIMIT)
        lin = jnp.clip(hh[:, D:2 * D], -SWIGLU_LIMIT, SWIGLU_LIMIT)
        act = glu * _sigmoid(SWIGLU_ALPHA * glu) * (lin + 1.0)
        out = jnp.dot(act.astype(BF16), w2b[...], preferred_element_type=F32) + b2_ref[0]
        _store_row_tiles(ys_ref, out, MOE_TM)

    @pl.when(i >= nu_ref[0])
    def _():
        ys_ref[...] = jnp.zeros(ys_ref.shape, F32)


def _experts(tile_expert, n_used, xs, w1, b1, w2, b2):
    n_rows = xs.shape[0] // ROW_SUB
    n_tiles = n_rows // MOE_TM
    row_map = lambda i, te, nu: (jnp.minimum(i, nu[0] - 1), 0)
    grid_spec = pltpu.PrefetchScalarGridSpec(
        num_scalar_prefetch=2,
        grid=(n_tiles,),
        in_specs=[pl.BlockSpec((MOE_TM * ROW_SUB, LANES), row_map),
                  pl.BlockSpec((1, D, 2 * D), lambda i, te, nu: (te[i], 0, 0)),
                  pl.BlockSpec((1, 1, 2 * D), lambda i, te, nu: (te[i], 0, 0)),
                  pl.BlockSpec((1, D, D), lambda i, te, nu: (te[i], 0, 0)),
                  pl.BlockSpec((1, 1, D), lambda i, te, nu: (te[i], 0, 0))],
        out_specs=pl.BlockSpec((MOE_TM * ROW_SUB, LANES), lambda i, te, nu: (i, 0)),
        scratch_shapes=[pltpu.VMEM((D, 2 * D), BF16), pltpu.VMEM((D, D), BF16)],
    )
    return pl.pallas_call(
        _expert_kernel,
        grid_spec=grid_spec,
        out_shape=jax.ShapeDtypeStruct((n_rows * ROW_SUB, LANES), F32),
        compiler_params=_cparams(("arbitrary",)),
        name="experts",
    )(tile_expert, n_used, xs, w1, b1, w2, b2)


def _final_kernel(pos_hbm, ys_hbm, topw_ref, x1_ref, mod_ref, postg_ref, o_ref, idx, buf, sem, isem,
                  *, tb, tiles_per_batch):
    i = pl.program_id(0)
    n = pl.num_programs(0)
    slot = i % 2

    @pl.when(i == 0)
    def _():
        _idx_copy(pos_hbm, idx, isem, 0, 0).start()

    @pl.when(i + 1 < n)
    def _():
        _idx_copy(pos_hbm, idx, isem, i + 1, 1 - slot).start()

    _idx_copy(pos_hbm, idx, isem, i, slot).wait()

    def issue(r, k, p):
        src = ys_hbm.at[pl.ds(pl.multiple_of(p * ROW_SUB, ROW_SUB), ROW_SUB)]
        dst = buf.at[k, pl.ds(pl.multiple_of(r * ROW_SUB, ROW_SUB), ROW_SUB)]
        pltpu.make_async_copy(src, dst, sem).start(priority=k % 2)

    _row_dma_loop(idx, slot, tb, issue)
    for k in range(TOP_K):
        pltpu.make_async_copy(ys_hbm.at[pl.ds(0, tb * ROW_SUB)], buf.at[k], sem).wait()

    row = i // tiles_per_batch
    tw = topw_ref[...]
    f = _load_row_tiles(buf.at[0], tb) * tw[:, 0:1]
    for k in range(1, TOP_K):
        f = f + _load_row_tiles(buf.at[k], tb) * tw[:, k:k + 1]
    g2 = mod_ref[pl.ds(row, 1), 5 * D:6 * D]
    o_ref[...] = x1_ref[...] + g2 * _rms(f, postg_ref[...])


def _final(pos2d, ys, topw, x1, mod, postg, *, tb, s_lat):
    n = x1.shape[0]
    kern = functools.partial(_final_kernel, tb=tb, tiles_per_batch=s_lat // tb)
    return pl.pallas_call(
        kern,
        grid=(n // tb,),
        in_specs=[pl.BlockSpec(memory_space=pl.ANY),
                  pl.BlockSpec(memory_space=pl.ANY),
                  pl.BlockSpec((tb, LANES), lambda i: (i, 0)),
                  pl.BlockSpec((tb, D), lambda i: (i, 0)),
                  pl.BlockSpec((8, 6 * D), lambda i: (0, 0)),
                  pl.BlockSpec((1, D), lambda i: (0, 0))],
        out_specs=pl.BlockSpec((tb, D), lambda i: (i, 0)),
        out_shape=jax.ShapeDtypeStruct((n, D), F32),
        scratch_shapes=[pltpu.SMEM((2 * tb * TOP_K,), jnp.int32), pltpu.VMEM((TOP_K, tb * ROW_SUB, LANES), F32),
                        pltpu.SemaphoreType.DMA, pltpu.SemaphoreType.DMA((2,))],
        compiler_params=_cparams(("arbitrary",)),
        name="final",
    )(pos2d, ys, topw, x1, mod, postg)


def _block_diag(w):
    w = w.reshape(2, 4, 4, RG_BW, RG_BW)
    eye = jnp.eye(4, dtype=w.dtype)
    return jnp.einsum('dgiab,ij->dgiajb', w, eye).reshape(2, 4, 4 * RG_BW, 4 * RG_BW)


def _grid_transpose(t, b):
    return t.reshape(b, GRID_W, GRID_W, -1).transpose(0, 2, 1, 3).reshape(t.shape)


def kernel(x, c, ctx, c_ctx, ada_w, ada_b, mix_pre_g, mix_post_g, w_in, rg_conv_w, rg_conv_b, rg_wa, rg_ba,
           rg_wi, rg_bi, rg_lam, rg_w_o, dn_conv_w, dn_a_log, dn_dt_bias, dn_norm_g, dn_w_o, w_out,
           ffn_pre_g, ffn_post_g, router_w, router_b, e_w1, e_b1, e_w2, e_b2):
    b, s_lat, _ = x.shape
    t_ctx = ctx.shape[1]
    depth = ada_w.shape[0]
    assert depth == 1 and b * 2 == SCAN_ROWS and s_lat == GRID_W * GRID_W
    n_lat = b * s_lat
    l = 0

    c8 = jnp.zeros((8, D), F32).at[0:b].set(c).at[b].set(c_ctx)
    mod = _ada_mod(c8, ada_w[l], ada_b[l].reshape(1, -1))

    wi = w_in[l]
    blk = lambda j0: wi[:, j0:j0 + D]
    w_raster = jnp.stack([blk(0), blk(D), blk(5 * D), blk(6 * D + 32), blk(7 * D + 32)]).astype(BF16)
    w_qkv = jnp.stack([blk(2 * D), blk(3 * D), blk(4 * D)]).astype(BF16)
    w_all = jnp.stack([blk(0), blk(D), blk(2 * D), blk(3 * D), blk(4 * D)]).astype(BF16)
    w_ab = jnp.pad(wi[:, 6 * D:6 * D + 32], ((0, 0), (0, LANES - 32))).astype(BF16)
    pre_g = mix_pre_g[l].reshape(1, D)
    x2d = x.reshape(n_lat, D)
    x_col = _grid_transpose(x2d, b)
    tpb = s_lat // INPROJ_TM
    o_lat, _ = _inproj(x2d, pre_g, mod, w_raster, w_ab, tm=INPROJ_TM, tiles_per_batch=tpb, row_offset=0)
    qkv_col, ab_col = _inproj(x_col, pre_g, mod, w_qkv, w_ab, tm=INPROJ_TM, tiles_per_batch=tpb, row_offset=0)
    o_ctx, ab_ctx = _inproj(ctx.reshape(b * t_ctx, D), pre_g, mod, w_all, w_ab, tm=b * t_ctx,
                            tiles_per_batch=1, row_offset=b)

    wblk = jnp.stack([_block_diag(rg_wa[l]), _block_diag(rg_wi[l])], axis=1).astype(BF16)
    rg_args = (rg_conv_w[l], rg_conv_b[l], wblk, rg_ba[l], rg_bi[l], rg_lam[l])
    coef_c = _rg_coef(o_ctx[0].reshape(b, t_ctx, D), *rg_args, tt=t_ctx, reset_first=True)
    coef_l = _rg_coef(o_lat[0].reshape(b, s_lat, D), *rg_args, tt=256, reset_first=False)
    hf, hb = _rg_scan(coef_c, coef_l, t_ctx=t_ctx, t_lat=s_lat, tt=256)

    alog_l = jnp.zeros((1, LANES), F32).at[0, 0:2 * HEADS].set(dn_a_log[l].reshape(-1))
    dtb_l = jnp.zeros((1, LANES), F32).at[0, 0:2 * HEADS].set(dn_dt_bias[l].reshape(-1))
    of, ob = _delta(qkv_col, ab_col, o_ctx, ab_ctx, dn_conv_w[l], alog_l, dtb_l, b=b, s_lat=s_lat, t_ctx=t_ctx)
    of = _grid_transpose(of, b)
    ob = _grid_transpose(ob, b)

    rw = jnp.pad(router_w[l], ((0, 0), (0, LANES - N_EXPERTS)))
    rw_hi = rw.astype(BF16)
    rw = jnp.concatenate([rw_hi, (rw - rw_hi.astype(F32)).astype(BF16)], axis=1)
    rb = jnp.pad(router_b[l].reshape(1, -1), ((0, 0), (0, LANES - N_EXPERTS)))
    x1, hf2, topw, route, cnt = _post(
        x2d, o_lat, hf, hb, of, ob, mod,
        rg_w_o[l].astype(BF16), dn_w_o[l].astype(BF16), w_out[l].astype(BF16),
        jnp.tile(dn_norm_g[l], HEADS).reshape(1, D), mix_post_g[l].reshape(1, D), ffn_pre_g[l].reshape(1, D),
        rw, rb, tm=256, s_lat=s_lat)

    counts = cnt[0, 0:N_EXPERTS]
    padded = ((counts + MOE_TM - 1) // MOE_TM) * MOE_TM
    ends = jnp.cumsum(padded)
    starts = ends - padded
    pos = starts[route[0:TOP_K]] + route[TOP_K:2 * TOP_K]
    pos2d = pos.reshape(TOP_K, n_lat // MOE_TB, MOE_TB).transpose(1, 0, 2).reshape(n_lat // MOE_TB, TOP_K * MOE_TB)
    n_tiles = n_lat * TOP_K // MOE_TM + N_EXPERTS
    tile_start = jnp.arange(n_tiles, dtype=jnp.int32) * MOE_TM
    tile_expert = jnp.minimum(jnp.sum(tile_start[:, None] >= ends[None, :], axis=1), N_EXPERTS - 1).astype(jnp.int32)
    n_used = (ends[-1] // MOE_TM).astype(jnp.int32).reshape(1)

    xs = _dispatch(starts, ends, pos2d, hf2.reshape(n_lat, ROW_SUB, LANES), n_tiles * MOE_TM, tb=MOE_TB)
    xs = xs.reshape(n_tiles * MOE_TM * ROW_SUB, LANES)
    ys = _experts(tile_expert, n_used, xs, e_w1[l], e_b1[l].reshape(N_EXPERTS, 1, -1),
                  e_w2[l], e_b2[l].reshape(N_EXPERTS, 1, -1))
    out = _final(pos2d, ys, topw, x1, mod, ffn_post_g[l].reshape(1, D), tb=MOE_TB, s_lat=s_lat)
    return out.reshape(b, s_lat, D)
```

```python
import functools

import jax
import jax.numpy as jnp
from jax import lax
from jax.experimental import pallas as pl
from jax.experimental.pallas import tpu as pltpu

F32 = jnp.float32
BF16 = jnp.bfloat16
U32 = jnp.uint32
HIGHEST = lax.Precision.HIGHEST

D = 1024
EPS = 1e-6
GRID_W = 64
CHUNK = 64
HEADS = 8
DK = 128
RG_C = 8.0
RG_BW = 64
N_EXPERTS = 32
TOP_K = 4
SWIGLU_LIMIT = 7.0
SWIGLU_ALPHA = 1.702
LANES = 128
INPROJ_TM = 2048
MOE_TM = 512
MOE_TB = 512
ROW_SUB = D // LANES
VMEM_LIMIT = 56 * 1024 * 1024


def _cparams(sem):
    return pltpu.CompilerParams(dimension_semantics=sem, vmem_limit_bytes=VMEM_LIMIT)


def _sigmoid(x):
    return 1.0 / (1.0 + jnp.exp(-x))


def _softplus(y):
    return jnp.maximum(y, 0.0) + jnp.log1p(jnp.exp(-jnp.abs(y)))


def _nt_dot(a, b):
    return lax.dot_general(a, b, (((1,), (1,)), ((), ())), preferred_element_type=F32)


def _ada_kernel(c_ref, w_ref, b_ref, o_ref):
    c = c_ref[...]
    a = c * _sigmoid(c)
    o_ref[...] = jnp.dot(a, w_ref[...], preferred_element_type=F32, precision=HIGHEST) + b_ref[...]


def _ada_mod(c8, ada_w, ada_b):
    n = ada_w.shape[1]
    tn = 1024
    return pl.pallas_call(
        _ada_kernel,
        grid=(n // tn,),
        in_specs=[pl.BlockSpec((8, D), lambda j: (0, 0)),
                  pl.BlockSpec((D, tn), lambda j: (0, j)),
                  pl.BlockSpec((1, tn), lambda j: (0, j))],
        out_specs=pl.BlockSpec((8, tn), lambda j: (0, j)),
        out_shape=jax.ShapeDtypeStruct((8, n), F32),
        compiler_params=_cparams(("arbitrary",)),
        name="ada_mod",
    )(c8, ada_w, ada_b)


def _inproj_kernel(x_ref, g_ref, mod_ref, w_ref, wab_ref, o_ref, ab_ref, h_scr, *, tiles_per_batch, row_offset):
    i = pl.program_id(0)
    j = pl.program_id(1)

    @pl.when(j == 0)
    def _():
        x = x_ref[...]
        y = x * lax.rsqrt(jnp.mean(x * x, axis=-1, keepdims=True) + EPS) * g_ref[...]
        row = row_offset + i // tiles_per_batch
        sh = mod_ref[pl.ds(row, 1), 0:D]
        sc = mod_ref[pl.ds(row, 1), D:2 * D]
        h = (y * (1.0 + sc) + sh).astype(BF16)
        h_scr[...] = h
        ab_ref[...] = jnp.dot(h, wab_ref[...], preferred_element_type=F32)

    o_ref[0] = jnp.dot(h_scr[...], w_ref[0], preferred_element_type=F32).astype(BF16)


def _inproj(x2d, g, mod, w_main, w_ab, *, tm, tiles_per_batch, row_offset):
    n = x2d.shape[0]
    nj = w_main.shape[0]
    kern = functools.partial(_inproj_kernel, tiles_per_batch=tiles_per_batch, row_offset=row_offset)
    return pl.pallas_call(
        kern,
        grid=(n // tm, nj),
        in_specs=[pl.BlockSpec((tm, D), lambda i, j: (i, 0)),
                  pl.BlockSpec((1, D), lambda i, j: (0, 0)),
                  pl.BlockSpec((8, 6 * D), lambda i, j: (0, 0)),
                  pl.BlockSpec((1, D, D), lambda i, j: (j, 0, 0)),
                  pl.BlockSpec((D, LANES), lambda i, j: (0, 0))],
        out_specs=[pl.BlockSpec((1, tm, D), lambda i, j: (j, i, 0)),
                   pl.BlockSpec((tm, LANES), lambda i, j: (i, 0))],
        out_shape=[jax.ShapeDtypeStruct((nj, n, D), BF16),
                   jax.ShapeDtypeStruct((n, LANES), F32)],
        scratch_shapes=[pltpu.VMEM((tm, D), BF16)],
        compiler_params=_cparams(("arbitrary", "arbitrary")),
        name="inproj",
    )(x2d, g, mod, w_main, w_ab)


RG_HALO = 16
SCAN_ROWS = 8
SCAN_SLABS = 4


def _pack_bf16_pair(hi, lo):
    hi_bits = lax.bitcast_convert_type(hi.astype(BF16).astype(F32), U32)
    lo_bits = lax.bitcast_convert_type(lo.astype(BF16).astype(F32), U32)
    return hi_bits | (lo_bits >> 16)


def _rg_coef_kernel(cur_ref, prev_ref, next_ref, cw_ref, cb_ref, wblk_ref, ba_ref, bi_ref, lam_ref,
                    wf_ref, wb_ref, xbuf, *, tt, nt, reset_first):
    i = pl.program_id(0)
    b = pl.program_id(1)
    x = cur_ref[0].astype(F32)
    xp = jnp.where(i > 0, prev_ref[0].astype(F32), 0.0)
    xn = jnp.where(i < nt - 1, next_ref[0].astype(F32), 0.0)
    n_slab = D // LANES
    for j in range(n_slab):
        ls = slice(j * LANES, (j + 1) * LANES)
        xbuf[j, 0:RG_HALO, :] = xp[:, ls]
        xbuf[j, RG_HALO:RG_HALO + tt, :] = x[:, ls]
        xbuf[j, RG_HALO + tt:2 * RG_HALO + tt, :] = xn[:, ls]
    row = lax.broadcasted_iota(jnp.int32, (tt, 1), 0)
    outs = (wf_ref, wb_ref)
    for d in (0, 1):
        w = cw_ref[d]
        cols = []
        for j in range(n_slab):
            ls = slice(j * LANES, (j + 1) * LANES)
            acc = x[:, ls] * w[3:4, ls]
            for sft in (1, 2, 3):
                off = RG_HALO - sft if d == 0 else RG_HALO + sft
                acc = acc + xbuf[j, off:off + tt, :] * w[3 - sft:4 - sft, ls]
            cols.append(acc)
        xc = jnp.concatenate(cols, axis=1) + cb_ref[d:d + 1]
        xcb = xc.astype(BF16)
        pre_a = jnp.concatenate(
            [jnp.dot(xcb[:, g * 256:(g + 1) * 256], wblk_ref[d, 0, g], preferred_element_type=F32)
             for g in range(4)], axis=1) + ba_ref[d:d + 1]
        pre_i = jnp.concatenate(
            [jnp.dot(xcb[:, g * 256:(g + 1) * 256], wblk_ref[d, 1, g], preferred_element_type=F32)
             for g in range(4)], axis=1) + bi_ref[d:d + 1]
        r = _sigmoid(pre_a)
        ig = _sigmoid(pre_i)
        la = r * (-RG_C * _softplus(-lam_ref[d:d + 1]))
        a = jnp.exp(la)
        om = 1.0 - a * a
        mult = jnp.where(om > 0.0, om * lax.rsqrt(om), 0.0)
        if reset_first:
            if d == 0:
                first = (row == 0) & (i == 0)
            else:
                first = (row == tt - 1) & (i == nt - 1)
            mult = jnp.where(first, 1.0, mult)
        word = _pack_bf16_pair(la, mult * ig * xc)
        for half in (0, 1):
            for q in range(SCAN_SLABS):
                c0 = (half * SCAN_SLABS + q) * LANES
                outs[d][q, pl.ds(b * 2 + half, tt, stride=SCAN_ROWS), :] = word[:, c0:c0 + LANES]


def _rg_coef(xr, conv_w, conv_b, wblk, ba, bi, lam, *, tt, reset_first):
    b, t, _ = xr.shape
    nt = t // tt
    hb = tt // RG_HALO
    kern = functools.partial(_rg_coef_kernel, tt=tt, nt=nt, reset_first=reset_first)
    full = lambda shape: pl.BlockSpec(shape, lambda i, bb: (0,) * len(shape))
    out_sds = jax.ShapeDtypeStruct((SCAN_SLABS, t * SCAN_ROWS, LANES), U32)
    out_spec = pl.BlockSpec((SCAN_SLABS, tt * SCAN_ROWS, LANES), lambda i, bb: (0, i, 0))
    return pl.pallas_call(
        kern,
        grid=(nt, b),
        in_specs=[pl.BlockSpec((1, tt, D), lambda i, bb: (bb, i, 0)),
                  pl.BlockSpec((1, RG_HALO, D), lambda i, bb: (bb, jnp.maximum(i * hb - 1, 0), 0)),
                  pl.BlockSpec((1, RG_HALO, D), lambda i, bb: (bb, jnp.minimum((i + 1) * hb, t // RG_HALO - 1), 0)),
                  full((2, 4, D)), full((2, D)), full((2, 2, 4, 256, 256)),
                  full((2, D)), full((2, D)), full((2, D))],
        out_specs=[out_spec, out_spec],
        out_shape=[out_sds, out_sds],
        scratch_shapes=[pltpu.VMEM((D // LANES, tt + 2 * RG_HALO, LANES), F32)],
        compiler_params=_cparams(("arbitrary", "arbitrary")),
        name="rg_coef",
    )(xr, xr, xr, conv_w, conv_b, wblk, ba, bi, lam)


def _scan_step(word, h):
    la = lax.bitcast_convert_type(word & jnp.uint32(0xFFFF0000), F32)
    bb = lax.bitcast_convert_type(word << 16, F32)
    return jnp.exp(la) * h + bb


def _rg_scan_run(wf, wb, hf, hb, carry, n):
    def body(t, c):
        h_f, h_b = c
        r0 = pl.multiple_of(t * SCAN_ROWS, SCAN_ROWS)
        r1 = pl.multiple_of((n - 1 - t) * SCAN_ROWS, SCAN_ROWS)
        new_f, new_b = [], []
        for q in range(SCAN_SLABS):
            f = _scan_step(wf[q, pl.ds(r0, SCAN_ROWS), :], h_f[q])
            g = _scan_step(wb[q, pl.ds(r1, SCAN_ROWS), :], h_b[q])
            if hf is not None:
                hf[q, pl.ds(r0, SCAN_ROWS), :] = f
                hb[q, pl.ds(r1, SCAN_ROWS), :] = g
            new_f.append(f)
            new_b.append(g)
        return tuple(new_f), tuple(new_b)

    init = (tuple(carry[0, q] for q in range(SCAN_SLABS)), tuple(carry[1, q] for q in range(SCAN_SLABS)))
    h_f, h_b = lax.fori_loop(0, n, body, init, unroll=8)
    for q in range(SCAN_SLABS):
        carry[0, q] = h_f[q]
        carry[1, q] = h_b[q]


def _rg_scan_kernel(wf_c, wb_c, wf_l, wb_l, hf_ref, hb_ref, carry, *, t_ctx, tt):
    s = pl.program_id(0)

    @pl.when(s == 0)
    def _():
        carry[...] = jnp.zeros(carry.shape, F32)
        _rg_scan_run(wf_c, wb_c, None, None, carry, t_ctx)

    @pl.when(s > 0)
    def _():
        _rg_scan_run(wf_l, wb_l, hf_ref, hb_ref, carry, tt)


def _rg_scan(coef_c, coef_l, *, t_ctx, t_lat, tt):
    nt = t_lat // tt
    ctx_spec = pl.BlockSpec((SCAN_SLABS, t_ctx * SCAN_ROWS, LANES), lambda s: (0, 0, 0))
    f_spec = pl.BlockSpec((SCAN_SLABS, tt * SCAN_ROWS, LANES), lambda s: (0, jnp.maximum(s - 1, 0), 0))
    b_spec = pl.BlockSpec((SCAN_SLABS, tt * SCAN_ROWS, LANES), lambda s: (0, nt - jnp.maximum(s, 1), 0))
    out_sds = jax.ShapeDtypeStruct((SCAN_SLABS, t_lat * SCAN_ROWS, LANES), F32)
    kern = functools.partial(_rg_scan_kernel, t_ctx=t_ctx, tt=tt)
    return pl.pallas_call(
        kern,
        grid=(nt + 1,),
        in_specs=[ctx_spec, ctx_spec, f_spec, b_spec],
        out_specs=[f_spec, b_spec],
        out_shape=[out_sds, out_sds],
        scratch_shapes=[pltpu.VMEM((2, SCAN_SLABS, SCAN_ROWS, LANES), F32)],
        compiler_params=_cparams(("arbitrary",)),
        name="rg_scan",
    )(coef_c[0], coef_c[1], coef_l[0], coef_l[1])


N_CTX_CHUNKS = 4
CONV_SLABS = 3 * D // LANES
CHAIN_GROUP = 16
NEUMANN_STEPS = 6


def _delta_prep(d, is_ctx, refs_l, refs_c, prevraw, cw_ref, alog_ref, dtb_ref, ybuf, gbuf, slot):
    q_l, k_l, v_l, ab_l = refs_l
    q_c, k_c, v_c, ab_c = refs_c

    srcs = ((q_c, q_l), (k_c, k_l), (v_c, v_l))
    base = 8 if d == 0 else 0
    pair_w = 2 * LANES
    slabs_per_src = D // LANES

    def conv_slab(j):
        rc, rl = srcs[j // slabs_per_src]
        ls = slice((j % slabs_per_src) * LANES, (j % slabs_per_src + 1) * LANES)
        raw = jnp.where(is_ctx, rc[0, :, ls], rl[0, :, ls]).astype(F32)
        prevraw[d, j, base:base + CHUNK, :] = raw
        wj = cw_ref[d, :, j * LANES:(j + 1) * LANES]
        acc = raw * wj[3:4]
        for sft in (1, 2, 3):
            off = base - sft if d == 0 else sft
            acc = acc + prevraw[d, j, off:off + CHUNK, :] * wj[3 - sft:4 - sft]
        if d == 0:
            prevraw[d, j, 0:8, :] = prevraw[d, j, CHUNK:CHUNK + 8, :]
        else:
            prevraw[d, j, CHUNK:CHUNK + 8, :] = prevraw[d, j, 0:8, :]
        return acc * _sigmoid(acc)

    def pair_piece(g):
        def run():
            y2 = jnp.concatenate([conv_slab(2 * g), conv_slab(2 * g + 1)], axis=1)
            if g < 2 * D // pair_w:
                bi = lax.broadcasted_iota(jnp.int32, (pair_w, pair_w), 0) // DK
                bj = lax.broadcasted_iota(jnp.int32, (pair_w, pair_w), 1) // DK
                ss = jnp.dot((y2 * y2).astype(BF16), (bi == bj).astype(BF16), preferred_element_type=F32)
                scale = (DK ** -0.5) if g < D // pair_w else 1.0
                y2 = y2 * lax.rsqrt(ss + EPS) * scale
            ybuf[slot, d, :, g * pair_w:(g + 1) * pair_w] = y2
        return run

    def gate_piece():
        abv = jnp.where(is_ctx, ab_c[...], ab_l[...])
        rp = lax.broadcasted_iota(jnp.int32, (2 * CHUNK, CHUNK), 0)
        cp = lax.broadcasted_iota(jnp.int32, (2 * CHUNK, CHUNK), 1)
        if d == 0:
            m_pad = ((rp >= cp) & (rp < CHUNK)).astype(F32)
        else:
            m_pad = ((rp <= cp) & (rp < CHUNK)).astype(F32)
        g_all = -jnp.exp(alog_ref[...]) * _softplus(abv + dtb_ref[...])
        gcum = jnp.dot(m_pad, g_all, preferred_element_type=F32, precision=HIGHEST)
        gbuf[slot, d, 0] = gcum
        gbuf[slot, d, 1] = gcum.T
        gbuf[slot, d, 2, 0:CHUNK, :] = _sigmoid(abv)

    return [gate_piece] + [pair_piece(g) for g in range(3 * D // pair_w)]


def _delta_heads(out_refs, state, ybuf, gbuf, slot):
    ri = lax.broadcasted_iota(jnp.int32, (CHUNK, CHUNK), 0)
    ci = lax.broadcasted_iota(jnp.int32, (CHUNK, CHUNK), 1)
    incl = (ri >= ci, ri <= ci)
    strict = (ri > ci, ri < ci)
    last = (CHUNK - 1, 0)
    eye = (ri == ci).astype(F32)
    keep = lax.broadcasted_iota(jnp.int32, (CHUNK, 2 * CHUNK), 1) >= CHUNK
    zeros_half = jnp.zeros((CHUNK, DK), F32)
    gcum = (gbuf[slot, 0, 0], gbuf[slot, 1, 0])
    gcum_t = (gbuf[slot, 0, 1], gbuf[slot, 1, 1])
    beta_all = (gbuf[slot, 0, 2, 0:CHUNK, :], gbuf[slot, 1, 2, 0:CHUNK, :])
    q_of = lambda c: ybuf[slot, c[0], :, c[1] * DK:(c[1] + 1) * DK]
    k_of = lambda c: ybuf[slot, c[0], :, D + c[1] * DK:D + (c[1] + 1) * DK]
    v_of = lambda c: ybuf[slot, c[0], :, 2 * D + c[1] * DK:2 * D + (c[1] + 1) * DK]
    lane_of = lambda c: c[0] * HEADS + c[1]
    gcol_of = lambda c: gcum[c[0]][0:CHUNK, lane_of(c):lane_of(c) + 1]
    bcol_of = lambda c: beta_all[c[0]][:, 2 * HEADS + lane_of(c):2 * HEADS + lane_of(c) + 1]

    chains = [(d, h) for h in range(HEADS) for d in (0, 1)]
    qmat, qkd, sol, wqh, v_new = {}, {}, {}, {}, {}

    def first(c):
        d = c[0]
        grow = gcum_t[d][lane_of(c):lane_of(c) + 1, 0:CHUNK]
        kn = k_of(c)
        knb = kn.astype(BF16)
        a_mat = _nt_dot((kn * bcol_of(c)).astype(BF16), knb)
        qk = _nt_dot(q_of(c).astype(BF16), knb)
        dec = jnp.where(incl[d], jnp.exp(jnp.where(incl[d], gcol_of(c) - grow, 0.0)), 0.0)
        s0 = jnp.where(strict[d], -a_mat * dec, 0.0)
        qmat[c] = jnp.concatenate([s0, eye], axis=1)
        qkd[c] = (qk * dec).astype(BF16)

    def neumann(c):
        qm = qmat[c]
        qmat[c] = jnp.dot(qm[:, 0:CHUNK].astype(BF16), qm.astype(BF16),
                          preferred_element_type=F32) + jnp.where(keep, qm, 0.0)

    def solve(c):
        bcol = bcol_of(c)
        rhs = jnp.concatenate([v_of(c) * bcol, k_of(c) * (bcol * jnp.exp(gcol_of(c)))], axis=1).astype(BF16)
        sol[c] = jnp.dot(qmat[c][:, CHUNK:2 * CHUNK].astype(BF16), rhs, preferred_element_type=F32)

    def apply_state(c):
        wq = jnp.concatenate([sol[c][:, DK:2 * DK], q_of(c) * jnp.exp(gcol_of(c))], axis=0)
        wqh[c] = jnp.dot(wq.astype(BF16), state[c[0], c[1]].astype(BF16), preferred_element_type=F32)

    def output(c):
        v_new[c] = sol[c][:, 0:DK] - wqh[c][0:CHUNK]
        o = wqh[c][CHUNK:2 * CHUNK] + jnp.dot(qkd[c], v_new[c].astype(BF16), preferred_element_type=F32)
        out_refs[c[0]][:, c[1] * DK:(c[1] + 1) * DK] = o.astype(BF16)

    def update_state(c):
        d = c[0]
        gl = gcum[d][last[d]:last[d] + 1, lane_of(c):lane_of(c) + 1]
        kd = k_of(c) * jnp.exp(gl - gcol_of(c))
        kd_t = jnp.concatenate([kd, zeros_half], axis=0).T
        v_pad = jnp.concatenate([v_new[c], zeros_half], axis=0)
        state[c[0], c[1]] = state[c[0], c[1]] * jnp.exp(gl) + jnp.dot(
            kd_t.astype(BF16), v_pad.astype(BF16), preferred_element_type=F32)

    stages = [first] + [neumann] * NEUMANN_STEPS + [solve, apply_state, output, update_state]
    groups = [chains[g:g + CHAIN_GROUP] for g in range(0, len(chains), CHAIN_GROUP)]
    return [functools.partial(stage, c) for group in groups for stage in stages for c in group]


def _delta_kernel(qf_l, kf_l, vf_l, abf_l, qb_l, kb_l, vb_l, abb_l,
                  qf_c, kf_c, vf_c, abf_c, qb_c, kb_c, vb_c, abb_c,
                  cw_ref, alog_ref, dtb_ref, of_ref, ob_ref, state, prevraw, ybuf, gbuf):
    s = pl.program_id(1)
    is_ctx = s < N_CTX_CHUNKS
    slot_p = s % 2
    slot_h = 1 - slot_p

    @pl.when(s == 0)
    def _():
        state[...] = jnp.zeros(state.shape, F32)
        ybuf[1] = jnp.zeros(ybuf.shape[1:], F32)
        gbuf[1] = jnp.zeros(gbuf.shape[1:], F32)
        gbuf[0] = jnp.zeros(gbuf.shape[1:], F32)

    @pl.when((s == 0) | (s == N_CTX_CHUNKS))
    def _():
        prevraw[...] = jnp.zeros(prevraw.shape, F32)

    prep = (_delta_prep(0, is_ctx, (qf_l, kf_l, vf_l, abf_l), (qf_c, kf_c, vf_c, abf_c), prevraw,
                        cw_ref, alog_ref, dtb_ref, ybuf, gbuf, slot_p)
            + _delta_prep(1, is_ctx, (qb_l, kb_l, vb_l, abb_l), (qb_c, kb_c, vb_c, abb_c), prevraw,
                          cw_ref, alog_ref, dtb_ref, ybuf, gbuf, slot_p))
    heads = _delta_heads((of_ref, ob_ref), state, ybuf, gbuf, slot_h)
    every = len(heads) // len(prep)
    for n, piece in enumerate(heads):
        piece()
        if n % every == every - 1 and prep:
            prep.pop(0)()
    for piece in prep:
        piece()


def _delta(qkv_col, ab_col, o_ctx, ab_ctx, conv_w, alog_l, dtb_l, *, b, s_lat, t_ctx):
    ncol = s_lat // CHUNK
    assert s_lat // GRID_W == CHUNK and t_ctx == N_CTX_CHUNKS * CHUNK

    lat_i = lambda s: jnp.clip(s - N_CTX_CHUNKS, 0, ncol - 1)
    col_f = lambda bb, s: bb * ncol + lat_i(s)
    col_b = lambda bb, s: bb * ncol + ncol - 1 - lat_i(s)
    cch_f = lambda bb, s: bb * N_CTX_CHUNKS + jnp.minimum(s, N_CTX_CHUNKS - 1)
    cch_b = lambda bb, s: bb * N_CTX_CHUNKS + N_CTX_CHUNKS - 1 - jnp.minimum(s, N_CTX_CHUNKS - 1)

    def tok_spec(j, fn):
        return pl.BlockSpec((1, CHUNK, D), lambda bb, s: (j, fn(bb, s), 0))

    ab_spec = lambda fn: pl.BlockSpec((CHUNK, LANES), lambda bb, s: (fn(bb, s), 0))
    full = lambda shape: pl.BlockSpec(shape, lambda bb, s: (0,) * len(shape))

    in_specs = ([tok_spec(0, col_f), tok_spec(1, col_f), tok_spec(2, col_f), ab_spec(col_f),
                 tok_spec(0, col_b), tok_spec(1, col_b), tok_spec(2, col_b), ab_spec(col_b),
                 tok_spec(2, cch_f), tok_spec(3, cch_f), tok_spec(4, cch_f), ab_spec(cch_f),
                 tok_spec(2, cch_b), tok_spec(3, cch_b), tok_spec(4, cch_b), ab_spec(cch_b),
                 full((2, 4, 3 * D)), full((1, LANES)), full((1, LANES))])
    out_sds = jax.ShapeDtypeStruct((b * s_lat, D), BF16)
    out_specs = [pl.BlockSpec((CHUNK, D), lambda bb, s: (col_f(bb, s - 1), 0)),
                 pl.BlockSpec((CHUNK, D), lambda bb, s: (col_b(bb, s - 1), 0))]
    return pl.pallas_call(
        _delta_kernel,
        grid=(b, N_CTX_CHUNKS + ncol + 1),
        in_specs=in_specs,
        out_specs=out_specs,
        out_shape=[out_sds, out_sds],
        scratch_shapes=[pltpu.VMEM((2, HEADS, DK, DK), F32), pltpu.VMEM((2, CONV_SLABS, CHUNK + 8, LANES), F32),
                        pltpu.VMEM((2, 2, CHUNK, 3 * D), F32), pltpu.VMEM((2, 2, 3, 2 * CHUNK, LANES), F32)],
        compiler_params=_cparams(("arbitrary", "arbitrary")),
        name="delta",
    )(qkv_col, qkv_col, qkv_col, ab_col, qkv_col, qkv_col, qkv_col, ab_col,
      o_ctx, o_ctx, o_ctx, ab_ctx, o_ctx, o_ctx, o_ctx, ab_ctx,
      conv_w, alog_l, dtb_l)


def _rms(x, g):
    return x * lax.rsqrt(jnp.mean(x * x, axis=-1, keepdims=True) + EPS) * g


def _store_row_tiles(ref, val, n):
    for s in range(ROW_SUB):
        ref[pl.ds(s, n, stride=ROW_SUB), :] = val[:, s * LANES:(s + 1) * LANES]


def _load_row_tiles(ref, n):
    return jnp.concatenate([ref[pl.ds(s, n, stride=ROW_SUB), :] for s in range(ROW_SUB)], axis=1)


def _scan_rows(ref, b, tm):
    return jnp.concatenate(
        [ref[q, pl.ds(b * 2 + half, tm, stride=SCAN_ROWS), :] for half in (0, 1) for q in range(SCAN_SLABS)],
        axis=1)


def _post_kernel(x_ref, gate_ref, z_ref, grg_ref, gdn_ref, hf_ref, hb_ref, of_ref, ob_ref, mod_ref,
                 wrg_ref, wdn_ref, wout_ref, dng_ref, postg_ref, preg_ref, rw_ref, rb_ref,
                 x1_ref, hf2_ref, topw_ref, route_ref, cnt_ref, carry, *, tm):
    b = pl.program_id(1)

    @pl.when((pl.program_id(0) == 0) & (b == 0))
    def _():
        carry[...] = jnp.zeros(carry.shape, F32)

    gate = gate_ref[0].astype(F32)
    gelu = 0.5 * gate * (1.0 + jnp.tanh(0.7978845608028654 * (gate + 0.044715 * gate * gate * gate)))
    rg_h = _scan_rows(hf_ref, b, tm) + _scan_rows(hb_ref, b, tm)
    y_rg = jnp.dot((rg_h * gelu).astype(BF16), wrg_ref[...], preferred_element_type=F32)

    dn = of_ref[...].astype(F32) + ob_ref[...].astype(F32)
    z = z_ref[0].astype(F32)
    parts = []
    for h in range(HEADS):
        seg = dn[:, h * DK:(h + 1) * DK]
        parts.append(seg * lax.rsqrt(jnp.mean(seg * seg, axis=-1, keepdims=True) + EPS))
    o = jnp.concatenate(parts, axis=1) * dng_ref[...] * (z * _sigmoid(z))
    y_dn = jnp.dot(o.astype(BF16), wdn_ref[...], preferred_element_type=F32)

    merged = _sigmoid(grg_ref[0].astype(F32)) * y_rg + _sigmoid(gdn_ref[0].astype(F32)) * y_dn
    y = jnp.dot(merged.astype(BF16), wout_ref[...], preferred_element_type=F32)

    g1 = mod_ref[pl.ds(b, 1), 2 * D:3 * D]
    sh2 = mod_ref[pl.ds(b, 1), 3 * D:4 * D]
    sc2 = mod_ref[pl.ds(b, 1), 4 * D:5 * D]
    x1 = x_ref[...] + g1 * _rms(y, postg_ref[...])
    x1_ref[...] = x1
    hf2 = _rms(x1, preg_ref[...]) * (1.0 + sc2) + sh2
    _store_row_tiles(hf2_ref, hf2, tm)

    lane = lax.broadcasted_iota(jnp.int32, (tm, LANES), 1)
    lane_f = lane.astype(F32)
    h_hi = hf2.astype(BF16)
    h_lo = (hf2 - h_hi.astype(F32)).astype(BF16)
    rw2 = rw_ref[...]
    hh = jnp.dot(h_hi, rw2, preferred_element_type=F32)
    logits = (hh[:, 0:LANES] + hh[:, LANES:2 * LANES]
              + jnp.dot(h_lo, rw2[:, 0:LANES], preferred_element_type=F32) + rb_ref[...])
    neg = jnp.float32(-jnp.inf)
    lg = jnp.where(lane < N_EXPERTS, logits, neg)
    vals, sels = [], []
    topi = jnp.zeros((tm, LANES), F32)
    onehot = jnp.zeros((tm, LANES), F32)
    for k in range(TOP_K):
        m = jnp.max(lg, axis=-1, keepdims=True)
        idx = jnp.min(jnp.where(lg == m, lane_f, float(LANES)), axis=-1, keepdims=True)
        sel = lane_f == idx
        vals.append(m)
        sels.append(sel)
        lg = jnp.where(sel, neg, lg)
        onehot = onehot + sel.astype(F32)
        topi = topi + jnp.where(lane == k, idx, 0.0)
    es = [jnp.exp(v - vals[0]) for v in vals]
    den = es[0] + es[1] + es[2] + es[3]
    topw = jnp.zeros((tm, LANES), F32)
    for k in range(TOP_K):
        topw = topw + jnp.where(lane == k, es[k] / den, 0.0)

    ri = lax.broadcasted_iota(jnp.int32, (tm, tm), 0)
    ci = lax.broadcasted_iota(jnp.int32, (tm, tm), 1)
    tri = (ri > ci).astype(BF16)
    cum = jnp.dot(tri, onehot.astype(BF16), preferred_element_type=F32) + carry[...]
    rank = jnp.zeros((tm, LANES), F32)
    for k in range(TOP_K):
        rk = jnp.sum(jnp.where(sels[k], cum, 0.0), axis=-1, keepdims=True)
        rank = rank + jnp.where(lane == TOP_K + k, rk, 0.0)
    new_carry = carry[...] + jnp.sum(onehot, axis=0, keepdims=True)
    carry[...] = new_carry
    topw_ref[...] = topw
    route_ref[...] = (topi + rank).T[0:2 * TOP_K, :].astype(jnp.int32)
    cnt_ref[...] = jnp.broadcast_to(new_carry, (8, LANES)).astype(jnp.int32)


def _post(x2d, o_lat, hf, hb, of, ob, mod, wrg, wdn, wout, dng, postg, preg, rw, rb, *, tm, s_lat):
    n = x2d.shape[0]
    tpb = s_lat // tm
    nb = n // s_lat
    kern = functools.partial(_post_kernel, tm=tm)
    tok = pl.BlockSpec((tm, D), lambda t, b: (b * tpb + t, 0))
    oj = lambda j: pl.BlockSpec((1, tm, D), lambda t, b: (j, b * tpb + t, 0))
    hspec = pl.BlockSpec((SCAN_SLABS, tm * SCAN_ROWS, LANES), lambda t, b: (0, t, 0))
    full = lambda shape: pl.BlockSpec(shape, lambda t, b: (0,) * len(shape))
    lane_out = pl.BlockSpec((tm, LANES), lambda t, b: (b * tpb + t, 0))
    return pl.pallas_call(
        kern,
        grid=(tpb, nb),
        in_specs=[tok, oj(1), oj(2), oj(3), oj(4), hspec, hspec, tok, tok, full((8, 6 * D)),
                  full((D, D)), full((D, D)), full((D, D)), full((1, D)), full((1, D)), full((1, D)),
                  full((D, 2 * LANES)), full((1, LANES))],
        out_specs=[tok, pl.BlockSpec((tm * ROW_SUB, LANES), lambda t, b: (b * tpb + t, 0)), lane_out,
                   pl.BlockSpec((2 * TOP_K, tm), lambda t, b: (0, b * tpb + t)),
                   pl.BlockSpec((8, LANES), lambda t, b: (0, 0))],
        out_shape=[jax.ShapeDtypeStruct((n, D), F32), jax.ShapeDtypeStruct((n * ROW_SUB, LANES), F32),
                   jax.ShapeDtypeStruct((n, LANES), F32), jax.ShapeDtypeStruct((2 * TOP_K, n), jnp.int32),
                   jax.ShapeDtypeStruct((8, LANES), jnp.int32)],
        scratch_shapes=[pltpu.VMEM((1, LANES), F32)],
        compiler_params=_cparams(("arbitrary", "arbitrary")),
        name="post_mix",
    )(x2d, o_lat, o_lat, o_lat, o_lat, hf, hb, of, ob, mod, wrg, wdn, wout, dng, postg, preg, rw, rb)


def _idx_copy(pos_hbm, idx, isem, tile, slot):
    n_idx = pos_hbm.shape[1]
    return pltpu.make_async_copy(pos_hbm.at[tile], idx.at[pl.ds(pl.multiple_of(slot * n_idx, n_idx), n_idx)],
                                 isem.at[slot])


def _row_dma_loop(idx, slot, tb, issue):
    base = slot * (tb * TOP_K)

    def body(rt, c):
        r0 = pl.multiple_of(rt * 8, 8)
        for j in range(8):
            for k in range(TOP_K):
                issue(r0 + j, k, idx[base + k * tb + r0 + j])
        return c

    lax.fori_loop(0, tb // 8, body, 0)


def _dispatch_kernel(starts_ref, ends_ref, pos_hbm, x_ref, xs_hbm, idx, zbuf, sem, isem, *, tb):
    i = pl.program_id(0)
    n = pl.num_programs(0)
    slot = i % 2

    @pl.when(i == 0)
    def _():
        _idx_copy(pos_hbm, idx, isem, 0, 0).start()
        zbuf[...] = jnp.zeros(zbuf.shape, F32)

        def zero_copy(e):
            r0 = pl.multiple_of(ends_ref[e] - MOE_TM, MOE_TM)
            return pltpu.make_async_copy(zbuf, xs_hbm.at[pl.ds(r0, MOE_TM)], sem)

        for e in range(N_EXPERTS):
            @pl.when(ends_ref[e] > starts_ref[e])
            def _():
                zero_copy(e).start()
        for e in range(N_EXPERTS):
            @pl.when(ends_ref[e] > starts_ref[e])
            def _():
                zero_copy(e).wait()

        def tail_copy(t):
            return pltpu.make_async_copy(zbuf, xs_hbm.at[pl.ds(pl.multiple_of(t * MOE_TM, MOE_TM), MOE_TM)], sem)

        n_used = ends_ref[N_EXPERTS - 1] // MOE_TM
        n_tiles = xs_hbm.shape[0] // MOE_TM

        def tail_start(t, c):
            tail_copy(t).start()
            return c

        def tail_wait(t, c):
            tail_copy(t).wait()
            return c

        lax.fori_loop(n_used, n_tiles, tail_start, 0)
        lax.fori_loop(n_used, n_tiles, tail_wait, 0)

    @pl.when(i + 1 < n)
    def _():
        _idx_copy(pos_hbm, idx, isem, i + 1, 1 - slot).start()

    _idx_copy(pos_hbm, idx, isem, i, slot).wait()

    def issue(r, k, p):
        pltpu.make_async_copy(x_ref.at[r], xs_hbm.at[p], sem).start(priority=k % 2)

    _row_dma_loop(idx, slot, tb, issue)
    for k in range(TOP_K):
        pltpu.make_async_copy(x_ref, xs_hbm.at[pl.ds(0, tb)], sem).wait()


def _dispatch(starts, ends, pos2d, xp, n_rows, *, tb):
    n = xp.shape[0]
    grid_spec = pltpu.PrefetchScalarGridSpec(
        num_scalar_prefetch=2,
        grid=(n // tb,),
        in_specs=[pl.BlockSpec(memory_space=pl.ANY),
                  pl.BlockSpec((tb, ROW_SUB, LANES), lambda i, st, en: (i, 0, 0))],
        out_specs=pl.BlockSpec(memory_space=pl.ANY),
        scratch_shapes=[pltpu.SMEM((2 * tb * TOP_K,), jnp.int32), pltpu.VMEM((MOE_TM, ROW_SUB, LANES), F32),
                        pltpu.SemaphoreType.DMA, pltpu.SemaphoreType.DMA((2,))],
    )
    return pl.pallas_call(
        functools.partial(_dispatch_kernel, tb=tb),
        grid_spec=grid_spec,
        out_shape=jax.ShapeDtypeStruct((n_rows, ROW_SUB, LANES), F32),
        compiler_params=_cparams(("arbitrary",)),
        name="dispatch",
    )(starts, ends, pos2d, xp)


def _expert_kernel(te_ref, nu_ref, xs_ref, w1_ref, b1_ref, w2_ref, b2_ref, ys_ref, w1b, w2b):
    i = pl.program_id(0)

    @pl.when(i < nu_ref[0])
    def _():
        changed = (i == 0) | (te_ref[i] != te_ref[jnp.maximum(i - 1, 0)])

        @pl.when(changed)
        def _():
            w1b[...] = w1_ref[0].astype(BF16)
            w2b[...] = w2_ref[0].astype(BF16)

        x = _load_row_tiles(xs_ref, MOE_TM).astype(BF16)
        hh = jnp.dot(x, w1b[...], preferred_element_type=F32) + b1_ref[0]
        glu = jnp.minimum(hh[:, 0:D], SWIGLU_LIMIT)
        lin = jnp.clip(hh[:, D:2 * D], -SWIGLU_LIMIT, SWIGLU_LIMIT)
        act = glu * _sigmoid(SWIGLU_ALPHA * glu) * (lin + 1.0)
        out = jnp.dot(act.astype(BF16), w2b[...], preferred_element_type=F32) + b2_ref[0]
        _store_row_tiles(ys_ref, out, MOE_TM)

    @pl.when(i >= nu_ref[0])
    def _():
        ys_ref[...] = jnp.zeros(ys_ref.shape, F32)


def _experts(tile_expert, n_used, xs, w1, b1, w2, b2):
    n_rows = xs.shape[0] // ROW_SUB
    n_tiles = n_rows // MOE_TM
    row_map = lambda i, te, nu: (jnp.minimum(i, nu[0] - 1), 0)
    grid_spec = pltpu.PrefetchScalarGridSpec(
        num_scalar_prefetch=2,
        grid=(n_tiles,),
        in_specs=[pl.BlockSpec((MOE_TM * ROW_SUB, LANES), row_map),
                  pl.BlockSpec((1, D, 2 * D), lambda i, te, nu: (te[i], 0, 0)),
                  pl.BlockSpec((1, 1, 2 * D), lambda i, te, nu: (te[i], 0, 0)),
                  pl.BlockSpec((1, D, D), lambda i, te, nu: (te[i], 0, 0)),
                  pl.BlockSpec((1, 1, D), lambda i, te, nu: (te[i], 0, 0))],
        out_specs=pl.BlockSpec((MOE_TM * ROW_SUB, LANES), lambda i, te, nu: (i, 0)),
        scratch_shapes=[pltpu.VMEM((D, 2 * D), BF16), pltpu.VMEM((D, D), BF16)],
    )
    return pl.pallas_call(
        _expert_kernel,
        grid_spec=grid_spec,
        out_shape=jax.ShapeDtypeStruct((n_rows * ROW_SUB, LANES), F32),
        compiler_params=_cparams(("arbitrary",)),
        name="experts",
    )(tile_expert, n_used, xs, w1, b1, w2, b2)


def _final_kernel(pos_hbm, ys_hbm, topw_ref, x1_ref, mod_ref, postg_ref, o_ref, idx, buf, sem, isem,
                  *, tb, tiles_per_batch):
    i = pl.program_id(0)
    n = pl.num_programs(0)
    slot = i % 2

    @pl.when(i == 0)
    def _():
        _idx_copy(pos_hbm, idx, isem, 0, 0).start()

    @pl.when(i + 1 < n)
    def _():
        _idx_copy(pos_hbm, idx, isem, i + 1, 1 - slot).start()

    _idx_copy(pos_hbm, idx, isem, i, slot).wait()

    def issue(r, k, p):
        src = ys_hbm.at[pl.ds(pl.multiple_of(p * ROW_SUB, ROW_SUB), ROW_SUB)]
        dst = buf.at[k, pl.ds(pl.multiple_of(r * ROW_SUB, ROW_SUB), ROW_SUB)]
        pltpu.make_async_copy(src, dst, sem).start(priority=k % 2)

    _row_dma_loop(idx, slot, tb, issue)
    for k in range(TOP_K):
        pltpu.make_async_copy(ys_hbm.at[pl.ds(0, tb * ROW_SUB)], buf.at[k], sem).wait()

    row = i // tiles_per_batch
    tw = topw_ref[...]
    f = _load_row_tiles(buf.at[0], tb) * tw[:, 0:1]
    for k in range(1, TOP_K):
        f = f + _load_row_tiles(buf.at[k], tb) * tw[:, k:k + 1]
    g2 = mod_ref[pl.ds(row, 1), 5 * D:6 * D]
    o_ref[...] = x1_ref[...] + g2 * _rms(f, postg_ref[...])


def _final(pos2d, ys, topw, x1, mod, postg, *, tb, s_lat):
    n = x1.shape[0]
    kern = functools.partial(_final_kernel, tb=tb, tiles_per_batch=s_lat // tb)
    return pl.pallas_call(
        kern,
        grid=(n // tb,),
        in_specs=[pl.BlockSpec(memory_space=pl.ANY),
                  pl.BlockSpec(memory_space=pl.ANY),
                  pl.BlockSpec((tb, LANES), lambda i: (i, 0)),
                  pl.BlockSpec((tb, D), lambda i: (i, 0)),
                  pl.BlockSpec((8, 6 * D), lambda i: (0, 0)),
                  pl.BlockSpec((1, D), lambda i: (0, 0))],
        out_specs=pl.BlockSpec((tb, D), lambda i: (i, 0)),
        out_shape=jax.ShapeDtypeStruct((n, D), F32),
        scratch_shapes=[pltpu.SMEM((2 * tb * TOP_K,), jnp.int32), pltpu.VMEM((TOP_K, tb * ROW_SUB, LANES), F32),
                        pltpu.SemaphoreType.DMA, pltpu.SemaphoreType.DMA((2,))],
        compiler_params=_cparams(("arbitrary",)),
        name="final",
    )(pos2d, ys, topw, x1, mod, postg)


def _block_diag(w):
    w = w.reshape(2, 4, 4, RG_BW, RG_BW)
    eye = jnp.eye(4, dtype=w.dtype)
    return jnp.einsum('dgiab,ij->dgiajb', w, eye).reshape(2, 4, 4 * RG_BW, 4 * RG_BW)


def _grid_transpose(t, b):
    return t.reshape(b, GRID_W, GRID_W, -1).transpose(0, 2, 1, 3).reshape(t.shape)


def kernel(x, c, ctx, c_ctx, ada_w, ada_b, mix_pre_g, mix_post_g, w_in, rg_conv_w, rg_conv_b, rg_wa, rg_ba,
           rg_wi, rg_bi, rg_lam, rg_w_o, dn_conv_w, dn_a_log, dn_dt_bias, dn_norm_g, dn_w_o, w_out,
           ffn_pre_g, ffn_post_g, router_w, router_b, e_w1, e_b1, e_w2, e_b2):
    b, s_lat, _ = x.shape
    t_ctx = ctx.shape[1]
    depth = ada_w.shape[0]
    assert depth == 1 and b * 2 == SCAN_ROWS and s_lat == GRID_W * GRID_W
    n_lat = b * s_lat
    l = 0

    c8 = jnp.zeros((8, D), F32).at[0:b].set(c).at[b].set(c_ctx)
    mod = _ada_mod(c8, ada_w[l], ada_b[l].reshape(1, -1))

    wi = w_in[l]
    blk = lambda j0: wi[:, j0:j0 + D]
    w_raster = jnp.stack([blk(0), blk(D), blk(5 * D), blk(6 * D + 32), blk(7 * D + 32)]).astype(BF16)
    w_qkv = jnp.stack([blk(2 * D), blk(3 * D), blk(4 * D)]).astype(BF16)
    w_all = jnp.stack([blk(0), blk(D), blk(2 * D), blk(3 * D), blk(4 * D)]).astype(BF16)
    w_ab = jnp.pad(wi[:, 6 * D:6 * D + 32], ((0, 0), (0, LANES - 32))).astype(BF16)
    pre_g = mix_pre_g[l].reshape(1, D)
    x2d = x.reshape(n_lat, D)
    x_col = _grid_transpose(x2d, b)
    tpb = s_lat // INPROJ_TM
    o_lat, _ = _inproj(x2d, pre_g, mod, w_raster, w_ab, tm=INPROJ_TM, tiles_per_batch=tpb, row_offset=0)
    qkv_col, ab_col = _inproj(x_col, pre_g, mod, w_qkv, w_ab, tm=INPROJ_TM, tiles_per_batch=tpb, row_offset=0)
    o_ctx, ab_ctx = _inproj(ctx.reshape(b * t_ctx, D), pre_g, mod, w_all, w_ab, tm=b * t_ctx,
                            tiles_per_batch=1, row_offset=b)

    wblk = jnp.stack([_block_diag(rg_wa[l]), _block_diag(rg_wi[l])], axis=1).astype(BF16)
    rg_args = (rg_conv_w[l], rg_conv_b[l], wblk, rg_ba[l], rg_bi[l], rg_lam[l])
    coef_c = _rg_coef(o_ctx[0].reshape(b, t_ctx, D), *rg_args, tt=t_ctx, reset_first=True)
    coef_l = _rg_coef(o_lat[0].reshape(b, s_lat, D), *rg_args, tt=256, reset_first=False)
    hf, hb = _rg_scan(coef_c, coef_l, t_ctx=t_ctx, t_lat=s_lat, tt=256)

    alog_l = jnp.zeros((1, LANES), F32).at[0, 0:2 * HEADS].set(dn_a_log[l].reshape(-1))
    dtb_l = jnp.zeros((1, LANES), F32).at[0, 0:2 * HEADS].set(dn_dt_bias[l].reshape(-1))
    of, ob = _delta(qkv_col, ab_col, o_ctx, ab_ctx, dn_conv_w[l], alog_l, dtb_l, b=b, s_lat=s_lat, t_ctx=t_ctx)
    of = _grid_transpose(of, b)
    ob = _grid_transpose(ob, b)

    rw = jnp.pad(router_w[l], ((0, 0), (0, LANES - N_EXPERTS)))
    rw_hi = rw.astype(BF16)
    rw = jnp.concatenate([rw_hi, (rw - rw_hi.astype(F32)).astype(BF16)], axis=1)
    rb = jnp.pad(router_b[l].reshape(1, -1), ((0, 0), (0, LANES - N_EXPERTS)))
    x1, hf2, topw, route, cnt = _post(
        x2d, o_lat, hf, hb, of, ob, mod,
        rg_w_o[l].astype(BF16), dn_w_o[l].astype(BF16), w_out[l].astype(BF16),
        jnp.tile(dn_norm_g[l], HEADS).reshape(1, D), mix_post_g[l].reshape(1, D), ffn_pre_g[l].reshape(1, D),
        rw, rb, tm=256, s_lat=s_lat)

    counts = cnt[0, 0:N_EXPERTS]
    padded = ((counts + MOE_TM - 1) // MOE_TM) * MOE_TM
    ends = jnp.cumsum(padded)
    starts = ends - padded
    pos = route[TOP_K:2 * TOP_K]
    for e in range(N_EXPERTS):
        pos = pos + jnp.where(route[0:TOP_K] == e, starts[e], 0)
    pos2d = pos.reshape(TOP_K, n_lat // MOE_TB, MOE_TB).transpose(1, 0, 2).reshape(n_lat // MOE_TB, TOP_K * MOE_TB)
    n_tiles = n_lat * TOP_K // MOE_TM + N_EXPERTS
    tile_start = jnp.arange(n_tiles, dtype=jnp.int32) * MOE_TM
    tile_expert = jnp.minimum(jnp.sum(tile_start[:, None] >= ends[None, :], axis=1), N_EXPERTS - 1).astype(jnp.int32)
    n_used = (ends[-1] // MOE_TM).astype(jnp.int32).reshape(1)

    xs = _dispatch(starts, ends, pos2d, hf2.reshape(n_lat, ROW_SUB, LANES), n_tiles * MOE_TM, tb=MOE_TB)
    xs = xs.reshape(n_tiles * MOE_TM * ROW_SUB, LANES)
    ys = _experts(tile_expert, n_used, xs, e_w1[l], e_b1[l].reshape(N_EXPERTS, 1, -1),
                  e_w2[l], e_b2[l].reshape(N_EXPERTS, 1, -1))
    out = _final(pos2d, ys, topw, x1, mod, ffn_post_g[l].reshape(1, D), tb=MOE_TB, s_lat=s_lat)
    return out.reshape(b, s_lat, D)
```

```python
import functools

import jax
import jax.numpy as jnp
from jax import lax
from jax.experimental import pallas as pl
from jax.experimental.pallas import tpu as pltpu

F32 = jnp.float32
BF16 = jnp.bfloat16
U32 = jnp.uint32
HIGHEST = lax.Precision.HIGHEST

D = 1024
EPS = 1e-6
GRID_W = 64
CHUNK = 64
HEADS = 8
DK = 128
RG_C = 8.0
RG_BW = 64
N_EXPERTS = 32
TOP_K = 4
SWIGLU_LIMIT = 7.0
SWIGLU_ALPHA = 1.702
LANES = 128
INPROJ_TM = 2048
MOE_TM = 512
MOE_TB = 512
ROW_SUB = D // LANES
VMEM_LIMIT = 56 * 1024 * 1024


def _cparams(sem):
    return pltpu.CompilerParams(dimension_semantics=sem, vmem_limit_bytes=VMEM_LIMIT)


def _sigmoid(x):
    return 1.0 / (1.0 + jnp.exp(-x))


def _softplus(y):
    return jnp.maximum(y, 0.0) + jnp.log1p(jnp.exp(-jnp.abs(y)))


def _nt_dot(a, b):
    return lax.dot_general(a, b, (((1,), (1,)), ((), ())), preferred_element_type=F32)


def _ada_kernel(c_ref, w_ref, b_ref, o_ref):
    c = c_ref[...]
    a = c * _sigmoid(c)
    o_ref[...] = jnp.dot(a, w_ref[...], preferred_element_type=F32, precision=HIGHEST) + b_ref[...]


def _ada_mod(c8, ada_w, ada_b):
    n = ada_w.shape[1]
    tn = 1024
    return pl.pallas_call(
        _ada_kernel,
        grid=(n // tn,),
        in_specs=[pl.BlockSpec((8, D), lambda j: (0, 0)),
                  pl.BlockSpec((D, tn), lambda j: (0, j)),
                  pl.BlockSpec((1, tn), lambda j: (0, j))],
        out_specs=pl.BlockSpec((8, tn), lambda j: (0, j)),
        out_shape=jax.ShapeDtypeStruct((8, n), F32),
        compiler_params=_cparams(("arbitrary",)),
        name="ada_mod",
    )(c8, ada_w, ada_b)


def _inproj_kernel(x_ref, g_ref, mod_ref, w_ref, wab_ref, o_ref, ab_ref, h_scr, *, tiles_per_batch, row_offset):
    i = pl.program_id(0)
    j = pl.program_id(1)

    @pl.when(j == 0)
    def _():
        x = x_ref[...]
        y = x * lax.rsqrt(jnp.mean(x * x, axis=-1, keepdims=True) + EPS) * g_ref[...]
        row = row_offset + i // tiles_per_batch
        sh = mod_ref[pl.ds(row, 1), 0:D]
        sc = mod_ref[pl.ds(row, 1), D:2 * D]
        h = (y * (1.0 + sc) + sh).astype(BF16)
        h_scr[...] = h
        ab_ref[...] = jnp.dot(h, wab_ref[...], preferred_element_type=F32)

    o_ref[0] = jnp.dot(h_scr[...], w_ref[0], preferred_element_type=F32).astype(BF16)


def _inproj(x2d, g, mod, w_main, w_ab, *, tm, tiles_per_batch, row_offset):
    n = x2d.shape[0]
    nj = w_main.shape[0]
    kern = functools.partial(_inproj_kernel, tiles_per_batch=tiles_per_batch, row_offset=row_offset)
    return pl.pallas_call(
        kern,
        grid=(n // tm, nj),
        in_specs=[pl.BlockSpec((tm, D), lambda i, j: (i, 0)),
                  pl.BlockSpec((1, D), lambda i, j: (0, 0)),
                  pl.BlockSpec((8, 6 * D), lambda i, j: (0, 0)),
                  pl.BlockSpec((1, D, D), lambda i, j: (j, 0, 0)),
                  pl.BlockSpec((D, LANES), lambda i, j: (0, 0))],
        out_specs=[pl.BlockSpec((1, tm, D), lambda i, j: (j, i, 0)),
                   pl.BlockSpec((tm, LANES), lambda i, j: (i, 0))],
        out_shape=[jax.ShapeDtypeStruct((nj, n, D), BF16),
                   jax.ShapeDtypeStruct((n, LANES), F32)],
        scratch_shapes=[pltpu.VMEM((tm, D), BF16)],
        compiler_params=_cparams(("arbitrary", "arbitrary")),
        name="inproj",
    )(x2d, g, mod, w_main, w_ab)


RG_HALO = 16
SCAN_ROWS = 8
SCAN_SLABS = 4


def _pack_bf16_pair(hi, lo):
    hi_bits = lax.bitcast_convert_type(hi.astype(BF16).astype(F32), U32)
    lo_bits = lax.bitcast_convert_type(lo.astype(BF16).astype(F32), U32)
    return hi_bits | (lo_bits >> 16)


def _rg_coef_kernel(cur_ref, prev_ref, next_ref, cw_ref, cb_ref, wblk_ref, ba_ref, bi_ref, lam_ref,
                    wf_ref, wb_ref, xbuf, *, tt, nt, reset_first):
    i = pl.program_id(0)
    b = pl.program_id(1)
    x = cur_ref[0].astype(F32)
    xp = jnp.where(i > 0, prev_ref[0].astype(F32), 0.0)
    xn = jnp.where(i < nt - 1, next_ref[0].astype(F32), 0.0)
    n_slab = D // LANES
    for j in range(n_slab):
        ls = slice(j * LANES, (j + 1) * LANES)
        xbuf[j, 0:RG_HALO, :] = xp[:, ls]
        xbuf[j, RG_HALO:RG_HALO + tt, :] = x[:, ls]
        xbuf[j, RG_HALO + tt:2 * RG_HALO + tt, :] = xn[:, ls]
    row = lax.broadcasted_iota(jnp.int32, (tt, 1), 0)
    outs = (wf_ref, wb_ref)
    for d in (0, 1):
        w = cw_ref[d]
        cols = []
        for j in range(n_slab):
            ls = slice(j * LANES, (j + 1) * LANES)
            acc = x[:, ls] * w[3:4, ls]
            for sft in (1, 2, 3):
                off = RG_HALO - sft if d == 0 else RG_HALO + sft
                acc = acc + xbuf[j, off:off + tt, :] * w[3 - sft:4 - sft, ls]
            cols.append(acc)
        xc = jnp.concatenate(cols, axis=1) + cb_ref[d:d + 1]
        xcb = xc.astype(BF16)
        pre_a = jnp.concatenate(
            [jnp.dot(xcb[:, g * 256:(g + 1) * 256], wblk_ref[d, 0, g], preferred_element_type=F32)
             for g in range(4)], axis=1) + ba_ref[d:d + 1]
        pre_i = jnp.concatenate(
            [jnp.dot(xcb[:, g * 256:(g + 1) * 256], wblk_ref[d, 1, g], preferred_element_type=F32)
             for g in range(4)], axis=1) + bi_ref[d:d + 1]
        r = _sigmoid(pre_a)
        ig = _sigmoid(pre_i)
        la = r * (-RG_C * _softplus(-lam_ref[d:d + 1]))
        a = jnp.exp(la)
        om = 1.0 - a * a
        mult = jnp.where(om > 0.0, om * lax.rsqrt(om), 0.0)
        if reset_first:
            if d == 0:
                first = (row == 0) & (i == 0)
            else:
                first = (row == tt - 1) & (i == nt - 1)
            mult = jnp.where(first, 1.0, mult)
        word = _pack_bf16_pair(la, mult * ig * xc)
        for half in (0, 1):
            for q in range(SCAN_SLABS):
                c0 = (half * SCAN_SLABS + q) * LANES
                outs[d][q, pl.ds(b * 2 + half, tt, stride=SCAN_ROWS), :] = word[:, c0:c0 + LANES]


def _rg_coef(xr, conv_w, conv_b, wblk, ba, bi, lam, *, b, tt, reset_first):
    t = xr.shape[1]
    nt = t // tt
    hb = tt // RG_HALO
    kern = functools.partial(_rg_coef_kernel, tt=tt, nt=nt, reset_first=reset_first)
    full = lambda shape: pl.BlockSpec(shape, lambda i, bb: (0,) * len(shape))
    out_sds = jax.ShapeDtypeStruct((SCAN_SLABS, t * SCAN_ROWS, LANES), U32)
    out_spec = pl.BlockSpec((SCAN_SLABS, tt * SCAN_ROWS, LANES), lambda i, bb: (0, i, 0))
    return pl.pallas_call(
        kern,
        grid=(nt, b),
        in_specs=[pl.BlockSpec((1, tt, D), lambda i, bb: (bb, i, 0)),
                  pl.BlockSpec((1, RG_HALO, D), lambda i, bb: (bb, jnp.maximum(i * hb - 1, 0), 0)),
                  pl.BlockSpec((1, RG_HALO, D), lambda i, bb: (bb, jnp.minimum((i + 1) * hb, t // RG_HALO - 1), 0)),
                  full((2, 4, D)), full((2, D)), full((2, 2, 4, 256, 256)),
                  full((2, D)), full((2, D)), full((2, D))],
        out_specs=[out_spec, out_spec],
        out_shape=[out_sds, out_sds],
        scratch_shapes=[pltpu.VMEM((D // LANES, tt + 2 * RG_HALO, LANES), F32)],
        compiler_params=_cparams(("arbitrary", "arbitrary")),
        name="rg_coef",
    )(xr, xr, xr, conv_w, conv_b, wblk, ba, bi, lam)


def _scan_step(word, h):
    la = lax.bitcast_convert_type(word & jnp.uint32(0xFFFF0000), F32)
    bb = lax.bitcast_convert_type(word << 16, F32)
    return jnp.exp(la) * h + bb


def _rg_scan_run(wf, wb, hf, hb, carry, n):
    def body(t, c):
        h_f, h_b = c
        r0 = pl.multiple_of(t * SCAN_ROWS, SCAN_ROWS)
        r1 = pl.multiple_of((n - 1 - t) * SCAN_ROWS, SCAN_ROWS)
        new_f, new_b = [], []
        for q in range(SCAN_SLABS):
            f = _scan_step(wf[q, pl.ds(r0, SCAN_ROWS), :], h_f[q])
            g = _scan_step(wb[q, pl.ds(r1, SCAN_ROWS), :], h_b[q])
            if hf is not None:
                hf[q, pl.ds(r0, SCAN_ROWS), :] = f
                hb[q, pl.ds(r1, SCAN_ROWS), :] = g
            new_f.append(f)
            new_b.append(g)
        return tuple(new_f), tuple(new_b)

    init = (tuple(carry[0, q] for q in range(SCAN_SLABS)), tuple(carry[1, q] for q in range(SCAN_SLABS)))
    h_f, h_b = lax.fori_loop(0, n, body, init, unroll=8)
    for q in range(SCAN_SLABS):
        carry[0, q] = h_f[q]
        carry[1, q] = h_b[q]


def _rg_scan_kernel(wf_c, wb_c, wf_l, wb_l, hf_ref, hb_ref, carry, *, t_ctx, tt):
    s = pl.program_id(0)

    @pl.when(s == 0)
    def _():
        carry[...] = jnp.zeros(carry.shape, F32)
        _rg_scan_run(wf_c, wb_c, None, None, carry, t_ctx)

    @pl.when(s > 0)
    def _():
        _rg_scan_run(wf_l, wb_l, hf_ref, hb_ref, carry, tt)


def _rg_scan(coef_c, coef_l, *, t_ctx, t_lat, tt):
    nt = t_lat // tt
    ctx_spec = pl.BlockSpec((SCAN_SLABS, t_ctx * SCAN_ROWS, LANES), lambda s: (0, 0, 0))
    f_spec = pl.BlockSpec((SCAN_SLABS, tt * SCAN_ROWS, LANES), lambda s: (0, jnp.maximum(s - 1, 0), 0))
    b_spec = pl.BlockSpec((SCAN_SLABS, tt * SCAN_ROWS, LANES), lambda s: (0, nt - jnp.maximum(s, 1), 0))
    out_sds = jax.ShapeDtypeStruct((SCAN_SLABS, t_lat * SCAN_ROWS, LANES), F32)
    kern = functools.partial(_rg_scan_kernel, t_ctx=t_ctx, tt=tt)
    return pl.pallas_call(
        kern,
        grid=(nt + 1,),
        in_specs=[ctx_spec, ctx_spec, f_spec, b_spec],
        out_specs=[f_spec, b_spec],
        out_shape=[out_sds, out_sds],
        scratch_shapes=[pltpu.VMEM((2, SCAN_SLABS, SCAN_ROWS, LANES), F32)],
        compiler_params=_cparams(("arbitrary",)),
        name="rg_scan",
    )(coef_c[0], coef_c[1], coef_l[0], coef_l[1])


N_CTX_CHUNKS = 4
CONV_SLABS = 3 * D // LANES
CHAIN_GROUP = 16
NEUMANN_STEPS = 6


def _delta_prep(d, is_ctx, refs_l, refs_c, prevraw, cw_ref, alog_ref, dtb_ref, ybuf, gbuf, slot):
    q_l, k_l, v_l, ab_l = refs_l
    q_c, k_c, v_c, ab_c = refs_c

    srcs = ((q_c, q_l), (k_c, k_l), (v_c, v_l))
    base = 8 if d == 0 else 0
    pair_w = 2 * LANES
    slabs_per_src = D // LANES

    def conv_slab(j):
        rc, rl = srcs[j // slabs_per_src]
        ls = slice((j % slabs_per_src) * LANES, (j % slabs_per_src + 1) * LANES)
        raw = jnp.where(is_ctx, rc[0, :, ls], rl[0, :, ls]).astype(F32)
        prevraw[d, j, base:base + CHUNK, :] = raw
        wj = cw_ref[d, :, j * LANES:(j + 1) * LANES]
        acc = raw * wj[3:4]
        for sft in (1, 2, 3):
            off = base - sft if d == 0 else sft
            acc = acc + prevraw[d, j, off:off + CHUNK, :] * wj[3 - sft:4 - sft]
        if d == 0:
            prevraw[d, j, 0:8, :] = prevraw[d, j, CHUNK:CHUNK + 8, :]
        else:
            prevraw[d, j, CHUNK:CHUNK + 8, :] = prevraw[d, j, 0:8, :]
        return acc * _sigmoid(acc)

    def pair_piece(g):
        def run():
            y2 = jnp.concatenate([conv_slab(2 * g), conv_slab(2 * g + 1)], axis=1)
            if g < 2 * D // pair_w:
                bi = lax.broadcasted_iota(jnp.int32, (pair_w, pair_w), 0) // DK
                bj = lax.broadcasted_iota(jnp.int32, (pair_w, pair_w), 1) // DK
                ss = jnp.dot((y2 * y2).astype(BF16), (bi == bj).astype(BF16), preferred_element_type=F32)
                scale = (DK ** -0.5) if g < D // pair_w else 1.0
                y2 = y2 * lax.rsqrt(ss + EPS) * scale
            ybuf[slot, d, :, g * pair_w:(g + 1) * pair_w] = y2
        return run

    def gate_piece():
        abv = jnp.where(is_ctx, ab_c[...], ab_l[...])
        rp = lax.broadcasted_iota(jnp.int32, (2 * CHUNK, CHUNK), 0)
        cp = lax.broadcasted_iota(jnp.int32, (2 * CHUNK, CHUNK), 1)
        if d == 0:
            m_pad = ((rp >= cp) & (rp < CHUNK)).astype(F32)
        else:
            m_pad = ((rp <= cp) & (rp < CHUNK)).astype(F32)
        g_all = -jnp.exp(alog_ref[...]) * _softplus(abv + dtb_ref[...])
        gcum = jnp.dot(m_pad, g_all, preferred_element_type=F32, precision=HIGHEST)
        gbuf[slot, d, 0] = gcum
        gbuf[slot, d, 1] = gcum.T
        gbuf[slot, d, 2, 0:CHUNK, :] = _sigmoid(abv)

    return [gate_piece] + [pair_piece(g) for g in range(3 * D // pair_w)]


def _delta_heads(out_refs, state, ybuf, gbuf, slot):
    ri = lax.broadcasted_iota(jnp.int32, (CHUNK, CHUNK), 0)
    ci = lax.broadcasted_iota(jnp.int32, (CHUNK, CHUNK), 1)
    incl = (ri >= ci, ri <= ci)
    strict = (ri > ci, ri < ci)
    last = (CHUNK - 1, 0)
    eye = (ri == ci).astype(F32)
    keep = lax.broadcasted_iota(jnp.int32, (CHUNK, 2 * CHUNK), 1) >= CHUNK
    zeros_half = jnp.zeros((CHUNK, DK), F32)
    gcum = (gbuf[slot, 0, 0], gbuf[slot, 1, 0])
    gcum_t = (gbuf[slot, 0, 1], gbuf[slot, 1, 1])
    beta_all = (gbuf[slot, 0, 2, 0:CHUNK, :], gbuf[slot, 1, 2, 0:CHUNK, :])
    q_of = lambda c: ybuf[slot, c[0], :, c[1] * DK:(c[1] + 1) * DK]
    k_of = lambda c: ybuf[slot, c[0], :, D + c[1] * DK:D + (c[1] + 1) * DK]
    v_of = lambda c: ybuf[slot, c[0], :, 2 * D + c[1] * DK:2 * D + (c[1] + 1) * DK]
    lane_of = lambda c: c[0] * HEADS + c[1]
    gcol_of = lambda c: gcum[c[0]][0:CHUNK, lane_of(c):lane_of(c) + 1]
    bcol_of = lambda c: beta_all[c[0]][:, 2 * HEADS + lane_of(c):2 * HEADS + lane_of(c) + 1]

    chains = [(d, h) for h in range(HEADS) for d in (0, 1)]
    qmat, qkd, sol, wqh, v_new = {}, {}, {}, {}, {}

    def first(c):
        d = c[0]
        grow = gcum_t[d][lane_of(c):lane_of(c) + 1, 0:CHUNK]
        kn = k_of(c)
        knb = kn.astype(BF16)
        a_mat = _nt_dot((kn * bcol_of(c)).astype(BF16), knb)
        qk = _nt_dot(q_of(c).astype(BF16), knb)
        dec = jnp.where(incl[d], jnp.exp(jnp.where(incl[d], gcol_of(c) - grow, 0.0)), 0.0)
        s0 = jnp.where(strict[d], -a_mat * dec, 0.0)
        qmat[c] = jnp.concatenate([s0, eye], axis=1)
        qkd[c] = (qk * dec).astype(BF16)

    def neumann(c):
        qm = qmat[c]
        qmat[c] = jnp.dot(qm[:, 0:CHUNK].astype(BF16), qm.astype(BF16),
                          preferred_element_type=F32) + jnp.where(keep, qm, 0.0)

    def solve(c):
        bcol = bcol_of(c)
        rhs = jnp.concatenate([v_of(c) * bcol, k_of(c) * (bcol * jnp.exp(gcol_of(c)))], axis=1).astype(BF16)
        sol[c] = jnp.dot(qmat[c][:, CHUNK:2 * CHUNK].astype(BF16), rhs, preferred_element_type=F32)

    def apply_state(c):
        wq = jnp.concatenate([sol[c][:, DK:2 * DK], q_of(c) * jnp.exp(gcol_of(c))], axis=0)
        wqh[c] = jnp.dot(wq.astype(BF16), state[c[0], c[1]].astype(BF16), preferred_element_type=F32)

    def output(c):
        v_new[c] = sol[c][:, 0:DK] - wqh[c][0:CHUNK]
        o = wqh[c][CHUNK:2 * CHUNK] + jnp.dot(qkd[c], v_new[c].astype(BF16), preferred_element_type=F32)
        out_refs[c[0]][:, c[1] * DK:(c[1] + 1) * DK] = o.astype(BF16)

    def update_state(c):
        d = c[0]
        gl = gcum[d][last[d]:last[d] + 1, lane_of(c):lane_of(c) + 1]
        kd = k_of(c) * jnp.exp(gl - gcol_of(c))
        kd_t = jnp.concatenate([kd, zeros_half], axis=0).T
        v_pad = jnp.concatenate([v_new[c], zeros_half], axis=0)
        state[c[0], c[1]] = state[c[0], c[1]] * jnp.exp(gl) + jnp.dot(
            kd_t.astype(BF16), v_pad.astype(BF16), preferred_element_type=F32)

    stages = [first] + [neumann] * NEUMANN_STEPS + [solve, apply_state, output, update_state]
    groups = [chains[g:g + CHAIN_GROUP] for g in range(0, len(chains), CHAIN_GROUP)]
    return [functools.partial(stage, c) for group in groups for stage in stages for c in group]


def _delta_kernel(qf_l, kf_l, vf_l, abf_l, qb_l, kb_l, vb_l, abb_l,
                  qf_c, kf_c, vf_c, abf_c, qb_c, kb_c, vb_c, abb_c,
                  cw_ref, alog_ref, dtb_ref, of_ref, ob_ref, state, prevraw, ybuf, gbuf):
    s = pl.program_id(1)
    is_ctx = s < N_CTX_CHUNKS
    slot_p = s % 2
    slot_h = 1 - slot_p

    @pl.when(s == 0)
    def _():
        state[...] = jnp.zeros(state.shape, F32)
        ybuf[1] = jnp.zeros(ybuf.shape[1:], F32)
        gbuf[1] = jnp.zeros(gbuf.shape[1:], F32)
        gbuf[0] = jnp.zeros(gbuf.shape[1:], F32)

    @pl.when((s == 0) | (s == N_CTX_CHUNKS))
    def _():
        prevraw[...] = jnp.zeros(prevraw.shape, F32)

    prep = (_delta_prep(0, is_ctx, (qf_l, kf_l, vf_l, abf_l), (qf_c, kf_c, vf_c, abf_c), prevraw,
                        cw_ref, alog_ref, dtb_ref, ybuf, gbuf, slot_p)
            + _delta_prep(1, is_ctx, (qb_l, kb_l, vb_l, abb_l), (qb_c, kb_c, vb_c, abb_c), prevraw,
                          cw_ref, alog_ref, dtb_ref, ybuf, gbuf, slot_p))
    heads = _delta_heads((of_ref, ob_ref), state, ybuf, gbuf, slot_h)
    every = len(heads) // len(prep)
    for n, piece in enumerate(heads):
        piece()
        if n % every == every - 1 and prep:
            prep.pop(0)()
    for piece in prep:
        piece()


def _delta(qkv_col, ab_col, o_ctx, ab_ctx, conv_w, alog_l, dtb_l, *, b, s_lat, t_ctx):
    ncol = s_lat // CHUNK
    assert s_lat // GRID_W == CHUNK and t_ctx == N_CTX_CHUNKS * CHUNK

    lat_i = lambda s: jnp.clip(s - N_CTX_CHUNKS, 0, ncol - 1)
    col_f = lambda bb, s: bb * ncol + lat_i(s)
    col_b = lambda bb, s: bb * ncol + ncol - 1 - lat_i(s)
    cch_f = lambda bb, s: bb * N_CTX_CHUNKS + jnp.minimum(s, N_CTX_CHUNKS - 1)
    cch_b = lambda bb, s: bb * N_CTX_CHUNKS + N_CTX_CHUNKS - 1 - jnp.minimum(s, N_CTX_CHUNKS - 1)

    def tok_spec(j, fn):
        return pl.BlockSpec((1, CHUNK, D), lambda bb, s: (j, fn(bb, s), 0))

    ab_spec = lambda fn: pl.BlockSpec((CHUNK, LANES), lambda bb, s: (fn(bb, s), 0))
    full = lambda shape: pl.BlockSpec(shape, lambda bb, s: (0,) * len(shape))

    in_specs = ([tok_spec(0, col_f), tok_spec(1, col_f), tok_spec(2, col_f), ab_spec(col_f),
                 tok_spec(0, col_b), tok_spec(1, col_b), tok_spec(2, col_b), ab_spec(col_b),
                 tok_spec(2, cch_f), tok_spec(3, cch_f), tok_spec(4, cch_f), ab_spec(cch_f),
                 tok_spec(2, cch_b), tok_spec(3, cch_b), tok_spec(4, cch_b), ab_spec(cch_b),
                 full((2, 4, 3 * D)), full((1, LANES)), full((1, LANES))])
    out_sds = jax.ShapeDtypeStruct((b * s_lat, D), BF16)
    out_specs = [pl.BlockSpec((CHUNK, D), lambda bb, s: (col_f(bb, s - 1), 0)),
                 pl.BlockSpec((CHUNK, D), lambda bb, s: (col_b(bb, s - 1), 0))]
    return pl.pallas_call(
        _delta_kernel,
        grid=(b, N_CTX_CHUNKS + ncol + 1),
        in_specs=in_specs,
        out_specs=out_specs,
        out_shape=[out_sds, out_sds],
        scratch_shapes=[pltpu.VMEM((2, HEADS, DK, DK), F32), pltpu.VMEM((2, CONV_SLABS, CHUNK + 8, LANES), F32),
                        pltpu.VMEM((2, 2, CHUNK, 3 * D), F32), pltpu.VMEM((2, 2, 3, 2 * CHUNK, LANES), F32)],
        compiler_params=_cparams(("arbitrary", "arbitrary")),
        name="delta",
    )(qkv_col, qkv_col, qkv_col, ab_col, qkv_col, qkv_col, qkv_col, ab_col,
      o_ctx, o_ctx, o_ctx, ab_ctx, o_ctx, o_ctx, o_ctx, ab_ctx,
      conv_w, alog_l, dtb_l)


def _rms(x, g):
    return x * lax.rsqrt(jnp.mean(x * x, axis=-1, keepdims=True) + EPS) * g


def _store_row_tiles(ref, val, n):
    for s in range(ROW_SUB):
        ref[pl.ds(s, n, stride=ROW_SUB), :] = val[:, s * LANES:(s + 1) * LANES]


def _load_row_tiles(ref, n):
    return jnp.concatenate([ref[pl.ds(s, n, stride=ROW_SUB), :] for s in range(ROW_SUB)], axis=1)


def _scan_rows(ref, b, tm):
    return jnp.concatenate(
        [ref[q, pl.ds(b * 2 + half, tm, stride=SCAN_ROWS), :] for half in (0, 1) for q in range(SCAN_SLABS)],
        axis=1)


def _post_kernel(x_ref, gate_ref, z_ref, grg_ref, gdn_ref, hf_ref, hb_ref, of_ref, ob_ref, mod_ref,
                 wrg_ref, wdn_ref, wout_ref, dng_ref, postg_ref, preg_ref, rw_ref, rb_ref,
                 x1_ref, hf2_ref, topw_ref, route_ref, cnt_ref, carry, *, tm):
    b = pl.program_id(1)

    @pl.when((pl.program_id(0) == 0) & (b == 0))
    def _():
        carry[...] = jnp.zeros(carry.shape, F32)

    gate = gate_ref[0].astype(F32)
    gelu = 0.5 * gate * (1.0 + jnp.tanh(0.7978845608028654 * (gate + 0.044715 * gate * gate * gate)))
    rg_h = _scan_rows(hf_ref, b, tm) + _scan_rows(hb_ref, b, tm)
    y_rg = jnp.dot((rg_h * gelu).astype(BF16), wrg_ref[...], preferred_element_type=F32)

    dn = of_ref[...].astype(F32) + ob_ref[...].astype(F32)
    z = z_ref[0].astype(F32)
    parts = []
    for h in range(HEADS):
        seg = dn[:, h * DK:(h + 1) * DK]
        parts.append(seg * lax.rsqrt(jnp.mean(seg * seg, axis=-1, keepdims=True) + EPS))
    o = jnp.concatenate(parts, axis=1) * dng_ref[...] * (z * _sigmoid(z))
    y_dn = jnp.dot(o.astype(BF16), wdn_ref[...], preferred_element_type=F32)

    merged = _sigmoid(grg_ref[0].astype(F32)) * y_rg + _sigmoid(gdn_ref[0].astype(F32)) * y_dn
    y = jnp.dot(merged.astype(BF16), wout_ref[...], preferred_element_type=F32)

    g1 = mod_ref[pl.ds(b, 1), 2 * D:3 * D]
    sh2 = mod_ref[pl.ds(b, 1), 3 * D:4 * D]
    sc2 = mod_ref[pl.ds(b, 1), 4 * D:5 * D]
    x1 = x_ref[...] + g1 * _rms(y, postg_ref[...])
    x1_ref[...] = x1
    hf2 = _rms(x1, preg_ref[...]) * (1.0 + sc2) + sh2
    _store_row_tiles(hf2_ref, hf2, tm)

    lane = lax.broadcasted_iota(jnp.int32, (tm, LANES), 1)
    lane_f = lane.astype(F32)
    h_hi = hf2.astype(BF16)
    h_lo = (hf2 - h_hi.astype(F32)).astype(BF16)
    rw2 = rw_ref[...]
    hh = jnp.dot(h_hi, rw2, preferred_element_type=F32)
    logits = (hh[:, 0:LANES] + hh[:, LANES:2 * LANES]
              + jnp.dot(h_lo, rw2[:, 0:LANES], preferred_element_type=F32) + rb_ref[...])
    neg = jnp.float32(-jnp.inf)
    lg = jnp.where(lane < N_EXPERTS, logits, neg)
    vals, sels = [], []
    topi = jnp.zeros((tm, LANES), F32)
    onehot = jnp.zeros((tm, LANES), F32)
    for k in range(TOP_K):
        m = jnp.max(lg, axis=-1, keepdims=True)
        idx = jnp.min(jnp.where(lg == m, lane_f, float(LANES)), axis=-1, keepdims=True)
        sel = lane_f == idx
        vals.append(m)
        sels.append(sel)
        lg = jnp.where(sel, neg, lg)
        onehot = onehot + sel.astype(F32)
        topi = topi + jnp.where(lane == k, idx, 0.0)
    es = [jnp.exp(v - vals[0]) for v in vals]
    den = es[0] + es[1] + es[2] + es[3]
    topw = jnp.zeros((tm, LANES), F32)
    for k in range(TOP_K):
        topw = topw + jnp.where(lane == k, es[k] / den, 0.0)

    ri = lax.broadcasted_iota(jnp.int32, (tm, tm), 0)
    ci = lax.broadcasted_iota(jnp.int32, (tm, tm), 1)
    tri = (ri > ci).astype(BF16)
    cum = jnp.dot(tri, onehot.astype(BF16), preferred_element_type=F32) + carry[...]
    rank = jnp.zeros((tm, LANES), F32)
    for k in range(TOP_K):
        rk = jnp.sum(jnp.where(sels[k], cum, 0.0), axis=-1, keepdims=True)
        rank = rank + jnp.where(lane == TOP_K + k, rk, 0.0)
    new_carry = carry[...] + jnp.sum(onehot, axis=0, keepdims=True)
    carry[...] = new_carry
    topw_ref[...] = topw
    route_ref[...] = (topi + rank).T[0:2 * TOP_K, :].astype(jnp.int32)
    cnt_ref[...] = jnp.broadcast_to(new_carry, (8, LANES)).astype(jnp.int32)


def _post(x2d, o_lat, hf, hb, of, ob, mod, wrg, wdn, wout, dng, postg, preg, rw, rb, *, tm, s_lat):
    n = x2d.shape[0]
    tpb = s_lat // tm
    nb = n // s_lat
    kern = functools.partial(_post_kernel, tm=tm)
    tok = pl.BlockSpec((tm, D), lambda t, b: (b * tpb + t, 0))
    oj = lambda j: pl.BlockSpec((1, tm, D), lambda t, b: (j, b * tpb + t, 0))
    hspec = pl.BlockSpec((SCAN_SLABS, tm * SCAN_ROWS, LANES), lambda t, b: (0, t, 0))
    full = lambda shape: pl.BlockSpec(shape, lambda t, b: (0,) * len(shape))
    lane_out = pl.BlockSpec((tm, LANES), lambda t, b: (b * tpb + t, 0))
    return pl.pallas_call(
        kern,
        grid=(tpb, nb),
        in_specs=[tok, oj(1), oj(2), oj(3), oj(4), hspec, hspec, tok, tok, full((8, 6 * D)),
                  full((D, D)), full((D, D)), full((D, D)), full((1, D)), full((1, D)), full((1, D)),
                  full((D, 2 * LANES)), full((1, LANES))],
        out_specs=[tok, pl.BlockSpec((tm * ROW_SUB, LANES), lambda t, b: (b * tpb + t, 0)), lane_out,
                   pl.BlockSpec((2 * TOP_K, tm), lambda t, b: (0, b * tpb + t)),
                   pl.BlockSpec((8, LANES), lambda t, b: (0, 0))],
        out_shape=[jax.ShapeDtypeStruct((n, D), F32), jax.ShapeDtypeStruct((n * ROW_SUB, LANES), F32),
                   jax.ShapeDtypeStruct((n, LANES), F32), jax.ShapeDtypeStruct((2 * TOP_K, n), jnp.int32),
                   jax.ShapeDtypeStruct((8, LANES), jnp.int32)],
        scratch_shapes=[pltpu.VMEM((1, LANES), F32)],
        compiler_params=_cparams(("arbitrary", "arbitrary")),
        name="post_mix",
    )(x2d, o_lat, o_lat, o_lat, o_lat, hf, hb, of, ob, mod, wrg, wdn, wout, dng, postg, preg, rw, rb)


def _idx_copy(pos_hbm, idx, isem, tile, slot):
    n_idx = pos_hbm.shape[1]
    return pltpu.make_async_copy(pos_hbm.at[tile], idx.at[pl.ds(pl.multiple_of(slot * n_idx, n_idx), n_idx)],
                                 isem.at[slot])


def _row_dma_loop(idx, slot, tb, issue):
    base = slot * (tb * TOP_K)

    def body(rt, c):
        r0 = pl.multiple_of(rt * 8, 8)
        for j in range(8):
            for k in range(TOP_K):
                issue(r0 + j, k, idx[base + k * tb + r0 + j])
        return c

    lax.fori_loop(0, tb // 8, body, 0)


def _dispatch_kernel(starts_ref, ends_ref, pos_hbm, x_ref, xs_hbm, idx, zbuf, sem, isem, *, tb):
    i = pl.program_id(0)
    n = pl.num_programs(0)
    slot = i % 2

    @pl.when(i == 0)
    def _():
        _idx_copy(pos_hbm, idx, isem, 0, 0).start()
        zbuf[...] = jnp.zeros(zbuf.shape, F32)

        def zero_copy(e):
            r0 = pl.multiple_of(ends_ref[e] - MOE_TM, MOE_TM)
            return pltpu.make_async_copy(zbuf, xs_hbm.at[pl.ds(r0, MOE_TM)], sem)

        for e in range(N_EXPERTS):
            @pl.when(ends_ref[e] > starts_ref[e])
            def _():
                zero_copy(e).start()
        for e in range(N_EXPERTS):
            @pl.when(ends_ref[e] > starts_ref[e])
            def _():
                zero_copy(e).wait()

        def tail_copy(t):
            return pltpu.make_async_copy(zbuf, xs_hbm.at[pl.ds(pl.multiple_of(t * MOE_TM, MOE_TM), MOE_TM)], sem)

        n_used = ends_ref[N_EXPERTS - 1] // MOE_TM
        n_tiles = xs_hbm.shape[0] // MOE_TM

        def tail_start(t, c):
            tail_copy(t).start()
            return c

        def tail_wait(t, c):
            tail_copy(t).wait()
            return c

        lax.fori_loop(n_used, n_tiles, tail_start, 0)
        lax.fori_loop(n_used, n_tiles, tail_wait, 0)

    @pl.when(i + 1 < n)
    def _():
        _idx_copy(pos_hbm, idx, isem, i + 1, 1 - slot).start()

    _idx_copy(pos_hbm, idx, isem, i, slot).wait()

    def issue(r, k, p):
        pltpu.make_async_copy(x_ref.at[r], xs_hbm.at[p], sem).start(priority=k % 2)

    _row_dma_loop(idx, slot, tb, issue)
    for k in range(TOP_K):
        pltpu.make_async_copy(x_ref, xs_hbm.at[pl.ds(0, tb)], sem).wait()


def _dispatch(starts, ends, pos2d, xp, n_rows, *, tb):
    n = xp.shape[0]
    grid_spec = pltpu.PrefetchScalarGridSpec(
        num_scalar_prefetch=2,
        grid=(n // tb,),
        in_specs=[pl.BlockSpec(memory_space=pl.ANY),
                  pl.BlockSpec((tb, ROW_SUB, LANES), lambda i, st, en: (i, 0, 0))],
        out_specs=pl.BlockSpec(memory_space=pl.ANY),
        scratch_shapes=[pltpu.SMEM((2 * tb * TOP_K,), jnp.int32), pltpu.VMEM((MOE_TM, ROW_SUB, LANES), F32),
                        pltpu.SemaphoreType.DMA, pltpu.SemaphoreType.DMA((2,))],
    )
    return pl.pallas_call(
        functools.partial(_dispatch_kernel, tb=tb),
        grid_spec=grid_spec,
        out_shape=jax.ShapeDtypeStruct((n_rows, ROW_SUB, LANES), F32),
        compiler_params=_cparams(("arbitrary",)),
        name="dispatch",
    )(starts, ends, pos2d, xp)


def _expert_kernel(te_ref, nu_ref, xs_ref, w1_ref, b1_ref, w2_ref, b2_ref, ys_ref, w1b, w2b):
    i = pl.program_id(0)

    @pl.when(i < nu_ref[0])
    def _():
        changed = (i == 0) | (te_ref[i] != te_ref[jnp.maximum(i - 1, 0)])

        @pl.when(changed)
        def _():
            w1b[...] = w1_ref[0].astype(BF16)
            w2b[...] = w2_ref[0].astype(BF16)

        x = _load_row_tiles(xs_ref, MOE_TM).astype(BF16)
        hh = jnp.dot(x, w1b[...], preferred_element_type=F32) + b1_ref[0]
        glu = jnp.minimum(hh[:, 0:D], SWIGLU_LIMIT)
        lin = jnp.clip(hh[:, D:2 * D], -SWIGLU_LIMIT, SWIGLU_LIMIT)
        act = glu * _sigmoid(SWIGLU_ALPHA * glu) * (lin + 1.0)
        out = jnp.dot(act.astype(BF16), w2b[...], preferred_element_type=F32) + b2_ref[0]
        _store_row_tiles(ys_ref, out, MOE_TM)

    @pl.when(i >= nu_ref[0])
    def _():
        ys_ref[...] = jnp.zeros(ys_ref.shape, F32)


def _experts(tile_expert, n_used, xs, w1, b1, w2, b2):
    n_rows = xs.shape[0] // ROW_SUB
    n_tiles = n_rows // MOE_TM
    row_map = lambda i, te, nu: (jnp.minimum(i, nu[0] - 1), 0)
    grid_spec = pltpu.PrefetchScalarGridSpec(
        num_scalar_prefetch=2,
        grid=(n_tiles,),
        in_specs=[pl.BlockSpec((MOE_TM * ROW_SUB, LANES), row_map),
                  pl.BlockSpec((1, D, 2 * D), lambda i, te, nu: (te[i], 0, 0)),
                  pl.BlockSpec((1, 1, 2 * D), lambda i, te, nu: (te[i], 0, 0)),
                  pl.BlockSpec((1, D, D), lambda i, te, nu: (te[i], 0, 0)),
                  pl.BlockSpec((1, 1, D), lambda i, te, nu: (te[i], 0, 0))],
        out_specs=pl.BlockSpec((MOE_TM * ROW_SUB, LANES), lambda i, te, nu: (i, 0)),
        scratch_shapes=[pltpu.VMEM((D, 2 * D), BF16), pltpu.VMEM((D, D), BF16)],
    )
    return pl.pallas_call(
        _expert_kernel,
        grid_spec=grid_spec,
        out_shape=jax.ShapeDtypeStruct((n_rows * ROW_SUB, LANES), F32),
        compiler_params=_cparams(("arbitrary",)),
        name="experts",
    )(tile_expert, n_used, xs, w1, b1, w2, b2)


def _final_kernel(pos_hbm, ys_hbm, topw_ref, x1_ref, mod_ref, postg_ref, o_ref, idx, buf, sem, isem,
                  *, tb, tiles_per_batch):
    i = pl.program_id(0)
    n = pl.num_programs(0)
    slot = i % 2

    @pl.when(i == 0)
    def _():
        _idx_copy(pos_hbm, idx, isem, 0, 0).start()

    @pl.when(i + 1 < n)
    def _():
        _idx_copy(pos_hbm, idx, isem, i + 1, 1 - slot).start()

    _idx_copy(pos_hbm, idx, isem, i, slot).wait()

    def issue(r, k, p):
        src = ys_hbm.at[pl.ds(pl.multiple_of(p * ROW_SUB, ROW_SUB), ROW_SUB)]
        dst = buf.at[k, pl.ds(pl.multiple_of(r * ROW_SUB, ROW_SUB), ROW_SUB)]
        pltpu.make_async_copy(src, dst, sem).start(priority=k % 2)

    _row_dma_loop(idx, slot, tb, issue)
    for k in range(TOP_K):
        pltpu.make_async_copy(ys_hbm.at[pl.ds(0, tb * ROW_SUB)], buf.at[k], sem).wait()

    row = i // tiles_per_batch
    tw = topw_ref[...]
    f = _load_row_tiles(buf.at[0], tb) * tw[:, 0:1]
    for k in range(1, TOP_K):
        f = f + _load_row_tiles(buf.at[k], tb) * tw[:, k:k + 1]
    g2 = mod_ref[pl.ds(row, 1), 5 * D:6 * D]
    o_ref[...] = x1_ref[...] + g2 * _rms(f, postg_ref[...])


def _final(pos2d, ys, topw, x1, mod, postg, *, tb, s_lat):
    n = x1.shape[0]
    kern = functools.partial(_final_kernel, tb=tb, tiles_per_batch=s_lat // tb)
    return pl.pallas_call(
        kern,
        grid=(n // tb,),
        in_specs=[pl.BlockSpec(memory_space=pl.ANY),
                  pl.BlockSpec(memory_space=pl.ANY),
                  pl.BlockSpec((tb, LANES), lambda i: (i, 0)),
                  pl.BlockSpec((tb, D), lambda i: (i, 0)),
                  pl.BlockSpec((8, 6 * D), lambda i: (0, 0)),
                  pl.BlockSpec((1, D), lambda i: (0, 0))],
        out_specs=pl.BlockSpec((tb, D), lambda i: (i, 0)),
        out_shape=jax.ShapeDtypeStruct((n, D), F32),
        scratch_shapes=[pltpu.SMEM((2 * tb * TOP_K,), jnp.int32), pltpu.VMEM((TOP_K, tb * ROW_SUB, LANES), F32),
                        pltpu.SemaphoreType.DMA, pltpu.SemaphoreType.DMA((2,))],
        compiler_params=_cparams(("arbitrary",)),
        name="final",
    )(pos2d, ys, topw, x1, mod, postg)


def _block_diag(w):
    w = w.reshape(2, 4, 4, RG_BW, RG_BW)
    eye = jnp.eye(4, dtype=w.dtype)
    return jnp.einsum('dgiab,ij->dgiajb', w, eye).reshape(2, 4, 4 * RG_BW, 4 * RG_BW)


def _grid_transpose(t, b):
    return t.reshape(b, GRID_W, GRID_W, -1).transpose(0, 2, 1, 3).reshape(t.shape)


def kernel(x, c, ctx, c_ctx, ada_w, ada_b, mix_pre_g, mix_post_g, w_in, rg_conv_w, rg_conv_b, rg_wa, rg_ba,
           rg_wi, rg_bi, rg_lam, rg_w_o, dn_conv_w, dn_a_log, dn_dt_bias, dn_norm_g, dn_w_o, w_out,
           ffn_pre_g, ffn_post_g, router_w, router_b, e_w1, e_b1, e_w2, e_b2):
    b, s_lat, _ = x.shape
    t_ctx = ctx.shape[1]
    depth = ada_w.shape[0]
    assert depth == 1 and b * 2 == SCAN_ROWS and s_lat == GRID_W * GRID_W
    n_lat = b * s_lat
    l = 0

    c8 = jnp.zeros((8, D), F32).at[0:b].set(c).at[b].set(c_ctx)
    mod = _ada_mod(c8, ada_w[l], ada_b[l].reshape(1, -1))

    wi = w_in[l]
    blk = lambda j0: wi[:, j0:j0 + D]
    w_raster = jnp.stack([blk(0), blk(D), blk(5 * D), blk(6 * D + 32), blk(7 * D + 32)]).astype(BF16)
    w_qkv = jnp.stack([blk(2 * D), blk(3 * D), blk(4 * D)]).astype(BF16)
    w_all = jnp.stack([blk(0), blk(D), blk(2 * D), blk(3 * D), blk(4 * D)]).astype(BF16)
    w_ab = jnp.pad(wi[:, 6 * D:6 * D + 32], ((0, 0), (0, LANES - 32))).astype(BF16)
    pre_g = mix_pre_g[l].reshape(1, D)
    x2d = x.reshape(n_lat, D)
    x_col = _grid_transpose(x2d, b)
    tpb = s_lat // INPROJ_TM
    o_lat, _ = _inproj(x2d, pre_g, mod, w_raster, w_ab, tm=INPROJ_TM, tiles_per_batch=tpb, row_offset=0)
    qkv_col, ab_col = _inproj(x_col, pre_g, mod, w_qkv, w_ab, tm=INPROJ_TM, tiles_per_batch=tpb, row_offset=0)
    o_ctx, ab_ctx = _inproj(ctx.reshape(b * t_ctx, D), pre_g, mod, w_all, w_ab, tm=b * t_ctx,
                            tiles_per_batch=1, row_offset=b)

    wblk = jnp.stack([_block_diag(rg_wa[l]), _block_diag(rg_wi[l])], axis=1).astype(BF16)
    rg_args = (rg_conv_w[l], rg_conv_b[l], wblk, rg_ba[l], rg_bi[l], rg_lam[l])
    coef_c = _rg_coef(o_ctx.reshape(-1, t_ctx, D), *rg_args, b=b, tt=t_ctx, reset_first=True)
    coef_l = _rg_coef(o_lat.reshape(-1, s_lat, D), *rg_args, b=b, tt=256, reset_first=False)
    hf, hb = _rg_scan(coef_c, coef_l, t_ctx=t_ctx, t_lat=s_lat, tt=256)

    alog_l = jnp.zeros((1, LANES), F32).at[0, 0:2 * HEADS].set(dn_a_log[l].reshape(-1))
    dtb_l = jnp.zeros((1, LANES), F32).at[0, 0:2 * HEADS].set(dn_dt_bias[l].reshape(-1))
    of, ob = _delta(qkv_col, ab_col, o_ctx, ab_ctx, dn_conv_w[l], alog_l, dtb_l, b=b, s_lat=s_lat, t_ctx=t_ctx)
    of = _grid_transpose(of, b)
    ob = _grid_transpose(ob, b)

    rw = jnp.pad(router_w[l], ((0, 0), (0, LANES - N_EXPERTS)))
    rw_hi = rw.astype(BF16)
    rw = jnp.concatenate([rw_hi, (rw - rw_hi.astype(F32)).astype(BF16)], axis=1)
    rb = jnp.pad(router_b[l].reshape(1, -1), ((0, 0), (0, LANES - N_EXPERTS)))
    x1, hf2, topw, route, cnt = _post(
        x2d, o_lat, hf, hb, of, ob, mod,
        rg_w_o[l].astype(BF16), dn_w_o[l].astype(BF16), w_out[l].astype(BF16),
        jnp.tile(dn_norm_g[l], HEADS).reshape(1, D), mix_post_g[l].reshape(1, D), ffn_pre_g[l].reshape(1, D),
        rw, rb, tm=256, s_lat=s_lat)

    counts = cnt[0, 0:N_EXPERTS]
    padded = ((counts + MOE_TM - 1) // MOE_TM) * MOE_TM
    ends = jnp.cumsum(padded)
    starts = ends - padded
    pos = route[TOP_K:2 * TOP_K]
    for e in range(N_EXPERTS):
        pos = pos + jnp.where(route[0:TOP_K] == e, starts[e], 0)
    pos2d = pos.reshape(TOP_K, n_lat // MOE_TB, MOE_TB).transpose(1, 0, 2).reshape(n_lat // MOE_TB, TOP_K * MOE_TB)
    n_tiles = n_lat * TOP_K // MOE_TM + N_EXPERTS
    tile_start = jnp.arange(n_tiles, dtype=jnp.int32) * MOE_TM
    tile_expert = jnp.minimum(jnp.sum(tile_start[:, None] >= ends[None, :], axis=1), N_EXPERTS - 1).astype(jnp.int32)
    n_used = (ends[-1] // MOE_TM).astype(jnp.int32).reshape(1)

    xs = _dispatch(starts, ends, pos2d, hf2.reshape(n_lat, ROW_SUB, LANES), n_tiles * MOE_TM, tb=MOE_TB)
    xs = xs.reshape(n_tiles * MOE_TM * ROW_SUB, LANES)
    ys = _experts(tile_expert, n_used, xs, e_w1[l], e_b1[l].reshape(N_EXPERTS, 1, -1),
                  e_w2[l], e_b2[l].reshape(N_EXPERTS, 1, -1))
    out = _final(pos2d, ys, topw, x1, mod, ffn_post_g[l].reshape(1, D), tb=MOE_TB, s_lat=s_lat)
    return out.reshape(b, s_lat, D)
```

```python
import functools

import jax
import jax.numpy as jnp
from jax import lax
from jax.experimental import pallas as pl
from jax.experimental.pallas import tpu as pltpu

F32 = jnp.float32
BF16 = jnp.bfloat16
U32 = jnp.uint32
HIGHEST = lax.Precision.HIGHEST

D = 1024
EPS = 1e-6
GRID_W = 64
CHUNK = 64
HEADS = 8
DK = 128
RG_C = 8.0
RG_BW = 64
N_EXPERTS = 32
TOP_K = 4
SWIGLU_LIMIT = 7.0
SWIGLU_ALPHA = 1.702
LANES = 128
INPROJ_TM = 2048
MOE_TM = 512
MOE_TB = 512
ROW_SUB = D // LANES
VMEM_LIMIT = 56 * 1024 * 1024


def _cparams(sem):
    return pltpu.CompilerParams(dimension_semantics=sem, vmem_limit_bytes=VMEM_LIMIT)


def _sigmoid(x):
    return 1.0 / (1.0 + jnp.exp(-x))


def _softplus(y):
    return jnp.maximum(y, 0.0) + jnp.log1p(jnp.exp(-jnp.abs(y)))


def _nt_dot(a, b):
    return lax.dot_general(a, b, (((1,), (1,)), ((), ())), preferred_element_type=F32)


def _ada_kernel(c_ref, w_ref, b_ref, o_ref):
    c = c_ref[...]
    a = c * _sigmoid(c)
    o_ref[...] = jnp.dot(a, w_ref[...], preferred_element_type=F32, precision=HIGHEST) + b_ref[...]


def _ada_mod(c8, ada_w, ada_b):
    n = ada_w.shape[1]
    tn = 1024
    return pl.pallas_call(
        _ada_kernel,
        grid=(n // tn,),
        in_specs=[pl.BlockSpec((8, D), lambda j: (0, 0)),
                  pl.BlockSpec((D, tn), lambda j: (0, j)),
                  pl.BlockSpec((1, tn), lambda j: (0, j))],
        out_specs=pl.BlockSpec((8, tn), lambda j: (0, j)),
        out_shape=jax.ShapeDtypeStruct((8, n), F32),
        compiler_params=_cparams(("arbitrary",)),
        name="ada_mod",
    )(c8, ada_w, ada_b)


def _inproj_kernel(x_ref, g_ref, mod_ref, w_ref, wab_ref, o_ref, ab_ref, h_scr, *, tiles_per_batch, row_offset):
    i = pl.program_id(0)
    j = pl.program_id(1)

    @pl.when(j == 0)
    def _():
        x = x_ref[...]
        y = x * lax.rsqrt(jnp.mean(x * x, axis=-1, keepdims=True) + EPS) * g_ref[...]
        row = row_offset + i // tiles_per_batch
        sh = mod_ref[pl.ds(row, 1), 0:D]
        sc = mod_ref[pl.ds(row, 1), D:2 * D]
        h = (y * (1.0 + sc) + sh).astype(BF16)
        h_scr[...] = h
        ab_ref[...] = jnp.dot(h, wab_ref[...], preferred_element_type=F32)

    o_ref[0] = jnp.dot(h_scr[...], w_ref[0], preferred_element_type=F32).astype(BF16)


def _inproj(x2d, g, mod, w_main, w_ab, *, tm, tiles_per_batch, row_offset):
    n = x2d.shape[0]
    nj = w_main.shape[0]
    kern = functools.partial(_inproj_kernel, tiles_per_batch=tiles_per_batch, row_offset=row_offset)
    return pl.pallas_call(
        kern,
        grid=(n // tm, nj),
        in_specs=[pl.BlockSpec((tm, D), lambda i, j: (i, 0)),
                  pl.BlockSpec((1, D), lambda i, j: (0, 0)),
                  pl.BlockSpec((8, 6 * D), lambda i, j: (0, 0)),
                  pl.BlockSpec((1, D, D), lambda i, j: (j, 0, 0)),
                  pl.BlockSpec((D, LANES), lambda i, j: (0, 0))],
        out_specs=[pl.BlockSpec((1, tm, D), lambda i, j: (j, i, 0)),
                   pl.BlockSpec((tm, LANES), lambda i, j: (i, 0))],
        out_shape=[jax.ShapeDtypeStruct((nj, n, D), BF16),
                   jax.ShapeDtypeStruct((n, LANES), F32)],
        scratch_shapes=[pltpu.VMEM((tm, D), BF16)],
        compiler_params=_cparams(("arbitrary", "arbitrary")),
        name="inproj",
    )(x2d, g, mod, w_main, w_ab)


RG_HALO = 16
SCAN_ROWS = 8
SCAN_SLABS = 4


def _pack_bf16_pair(hi, lo):
    hi_bits = lax.bitcast_convert_type(hi.astype(BF16).astype(F32), U32)
    lo_bits = lax.bitcast_convert_type(lo.astype(BF16).astype(F32), U32)
    return hi_bits | (lo_bits >> 16)


def _rg_coef_kernel(cur_ref, prev_ref, next_ref, cw_ref, cb_ref, wblk_ref, ba_ref, bi_ref, lam_ref,
                    wf_ref, wb_ref, xbuf, *, tt, nt, reset_first):
    i = pl.program_id(0)
    b = pl.program_id(1)
    x = cur_ref[0].astype(F32)
    xp = jnp.where(i > 0, prev_ref[0].astype(F32), 0.0)
    xn = jnp.where(i < nt - 1, next_ref[0].astype(F32), 0.0)
    n_slab = D // LANES
    for j in range(n_slab):
        ls = slice(j * LANES, (j + 1) * LANES)
        xbuf[j, 0:RG_HALO, :] = xp[:, ls]
        xbuf[j, RG_HALO:RG_HALO + tt, :] = x[:, ls]
        xbuf[j, RG_HALO + tt:2 * RG_HALO + tt, :] = xn[:, ls]
    row = lax.broadcasted_iota(jnp.int32, (tt, 1), 0)
    outs = (wf_ref, wb_ref)
    for d in (0, 1):
        w = cw_ref[d]
        cols = []
        for j in range(n_slab):
            ls = slice(j * LANES, (j + 1) * LANES)
            acc = x[:, ls] * w[3:4, ls]
            for sft in (1, 2, 3):
                off = RG_HALO - sft if d == 0 else RG_HALO + sft
                acc = acc + xbuf[j, off:off + tt, :] * w[3 - sft:4 - sft, ls]
            cols.append(acc)
        xc = jnp.concatenate(cols, axis=1) + cb_ref[d:d + 1]
        xcb = xc.astype(BF16)
        pre_a = jnp.concatenate(
            [jnp.dot(xcb[:, g * 256:(g + 1) * 256], wblk_ref[d, 0, g], preferred_element_type=F32)
             for g in range(4)], axis=1) + ba_ref[d:d + 1]
        pre_i = jnp.concatenate(
            [jnp.dot(xcb[:, g * 256:(g + 1) * 256], wblk_ref[d, 1, g], preferred_element_type=F32)
             for g in range(4)], axis=1) + bi_ref[d:d + 1]
        r = _sigmoid(pre_a)
        ig = _sigmoid(pre_i)
        la = r * (-RG_C * _softplus(-lam_ref[d:d + 1]))
        a = jnp.exp(la)
        om = 1.0 - a * a
        mult = jnp.where(om > 0.0, om * lax.rsqrt(om), 0.0)
        if reset_first:
            if d == 0:
                first = (row == 0) & (i == 0)
            else:
                first = (row == tt - 1) & (i == nt - 1)
            mult = jnp.where(first, 1.0, mult)
        word = _pack_bf16_pair(la, mult * ig * xc)
        for half in (0, 1):
            for q in range(SCAN_SLABS):
                c0 = (half * SCAN_SLABS + q) * LANES
                outs[d][q, pl.ds(b * 2 + half, tt, stride=SCAN_ROWS), :] = word[:, c0:c0 + LANES]


def _rg_coef(xr, conv_w, conv_b, wblk, ba, bi, lam, *, b, tt, reset_first):
    t = xr.shape[1]
    nt = t // tt
    hb = tt // RG_HALO
    kern = functools.partial(_rg_coef_kernel, tt=tt, nt=nt, reset_first=reset_first)
    full = lambda shape: pl.BlockSpec(shape, lambda i, bb: (0,) * len(shape))
    out_sds = jax.ShapeDtypeStruct((SCAN_SLABS, t * SCAN_ROWS, LANES), U32)
    out_spec = pl.BlockSpec((SCAN_SLABS, tt * SCAN_ROWS, LANES), lambda i, bb: (0, i, 0))
    return pl.pallas_call(
        kern,
        grid=(nt, b),
        in_specs=[pl.BlockSpec((1, tt, D), lambda i, bb: (bb, i, 0)),
                  pl.BlockSpec((1, RG_HALO, D), lambda i, bb: (bb, jnp.maximum(i * hb - 1, 0), 0)),
                  pl.BlockSpec((1, RG_HALO, D), lambda i, bb: (bb, jnp.minimum((i + 1) * hb, t // RG_HALO - 1), 0)),
                  full((2, 4, D)), full((2, D)), full((2, 2, 4, 256, 256)),
                  full((2, D)), full((2, D)), full((2, D))],
        out_specs=[out_spec, out_spec],
        out_shape=[out_sds, out_sds],
        scratch_shapes=[pltpu.VMEM((D // LANES, tt + 2 * RG_HALO, LANES), F32)],
        compiler_params=_cparams(("arbitrary", "arbitrary")),
        name="rg_coef",
    )(xr, xr, xr, conv_w, conv_b, wblk, ba, bi, lam)


def _scan_step(word, h):
    la = lax.bitcast_convert_type(word & jnp.uint32(0xFFFF0000), F32)
    bb = lax.bitcast_convert_type(word << 16, F32)
    return jnp.exp(la) * h + bb


def _rg_scan_run(wf, wb, hf, hb, carry, n):
    def body(t, c):
        h_f, h_b = c
        r0 = pl.multiple_of(t * SCAN_ROWS, SCAN_ROWS)
        r1 = pl.multiple_of((n - 1 - t) * SCAN_ROWS, SCAN_ROWS)
        new_f, new_b = [], []
        for q in range(SCAN_SLABS):
            f = _scan_step(wf[q, pl.ds(r0, SCAN_ROWS), :], h_f[q])
            g = _scan_step(wb[q, pl.ds(r1, SCAN_ROWS), :], h_b[q])
            if hf is not None:
                hf[q, pl.ds(r0, SCAN_ROWS), :] = f
                hb[q, pl.ds(r1, SCAN_ROWS), :] = g
            new_f.append(f)
            new_b.append(g)
        return tuple(new_f), tuple(new_b)

    init = (tuple(carry[0, q] for q in range(SCAN_SLABS)), tuple(carry[1, q] for q in range(SCAN_SLABS)))
    h_f, h_b = lax.fori_loop(0, n, body, init, unroll=8)
    for q in range(SCAN_SLABS):
        carry[0, q] = h_f[q]
        carry[1, q] = h_b[q]


def _rg_scan_kernel(wf_c, wb_c, wf_l, wb_l, hf_ref, hb_ref, carry, *, t_ctx, tt):
    s = pl.program_id(0)

    @pl.when(s == 0)
    def _():
        carry[...] = jnp.zeros(carry.shape, F32)
        _rg_scan_run(wf_c, wb_c, None, None, carry, t_ctx)

    @pl.when(s > 0)
    def _():
        _rg_scan_run(wf_l, wb_l, hf_ref, hb_ref, carry, tt)


def _rg_scan(coef_c, coef_l, *, t_ctx, t_lat, tt):
    nt = t_lat // tt
    ctx_spec = pl.BlockSpec((SCAN_SLABS, t_ctx * SCAN_ROWS, LANES), lambda s: (0, 0, 0))
    f_spec = pl.BlockSpec((SCAN_SLABS, tt * SCAN_ROWS, LANES), lambda s: (0, jnp.maximum(s - 1, 0), 0))
    b_spec = pl.BlockSpec((SCAN_SLABS, tt * SCAN_ROWS, LANES), lambda s: (0, nt - jnp.maximum(s, 1), 0))
    out_sds = jax.ShapeDtypeStruct((SCAN_SLABS, t_lat * SCAN_ROWS, LANES), F32)
    kern = functools.partial(_rg_scan_kernel, t_ctx=t_ctx, tt=tt)
    return pl.pallas_call(
        kern,
        grid=(nt + 1,),
        in_specs=[ctx_spec, ctx_spec, f_spec, b_spec],
        out_specs=[f_spec, b_spec],
        out_shape=[out_sds, out_sds],
        scratch_shapes=[pltpu.VMEM((2, SCAN_SLABS, SCAN_ROWS, LANES), F32)],
        compiler_params=_cparams(("arbitrary",)),
        name="rg_scan",
    )(coef_c[0], coef_c[1], coef_l[0], coef_l[1])


N_CTX_CHUNKS = 4
CONV_SLABS = 3 * D // LANES
CHAIN_GROUP = 16
NEUMANN_STEPS = 6


def _delta_prep(d, is_ctx, refs_l, refs_c, prevraw, cw_ref, alog_ref, dtb_ref, ybuf, gbuf, slot):
    q_l, k_l, v_l, ab_l = refs_l
    q_c, k_c, v_c, ab_c = refs_c

    srcs = ((q_c, q_l), (k_c, k_l), (v_c, v_l))
    base = 8 if d == 0 else 0
    pair_w = 2 * LANES
    slabs_per_src = D // LANES

    def conv_slab(j):
        rc, rl = srcs[j // slabs_per_src]
        ls = slice((j % slabs_per_src) * LANES, (j % slabs_per_src + 1) * LANES)
        raw = jnp.where(is_ctx, rc[0, :, ls], rl[0, :, ls]).astype(F32)
        prevraw[d, j, base:base + CHUNK, :] = raw
        wj = cw_ref[d, :, j * LANES:(j + 1) * LANES]
        acc = raw * wj[3:4]
        for sft in (1, 2, 3):
            off = base - sft if d == 0 else sft
            acc = acc + prevraw[d, j, off:off + CHUNK, :] * wj[3 - sft:4 - sft]
        if d == 0:
            prevraw[d, j, 0:8, :] = prevraw[d, j, CHUNK:CHUNK + 8, :]
        else:
            prevraw[d, j, CHUNK:CHUNK + 8, :] = prevraw[d, j, 0:8, :]
        return acc * _sigmoid(acc)

    def pair_piece(g):
        def run():
            y2 = jnp.concatenate([conv_slab(2 * g), conv_slab(2 * g + 1)], axis=1)
            if g < 2 * D // pair_w:
                bi = lax.broadcasted_iota(jnp.int32, (pair_w, pair_w), 0) // DK
                bj = lax.broadcasted_iota(jnp.int32, (pair_w, pair_w), 1) // DK
                ss = jnp.dot((y2 * y2).astype(BF16), (bi == bj).astype(BF16), preferred_element_type=F32)
                scale = (DK ** -0.5) if g < D // pair_w else 1.0
                y2 = y2 * lax.rsqrt(ss + EPS) * scale
            ybuf[slot, d, :, g * pair_w:(g + 1) * pair_w] = y2
        return run

    def gate_piece():
        abv = jnp.where(is_ctx, ab_c[...], ab_l[...])
        rp = lax.broadcasted_iota(jnp.int32, (2 * CHUNK, CHUNK), 0)
        cp = lax.broadcasted_iota(jnp.int32, (2 * CHUNK, CHUNK), 1)
        if d == 0:
            m_pad = ((rp >= cp) & (rp < CHUNK)).astype(F32)
        else:
            m_pad = ((rp <= cp) & (rp < CHUNK)).astype(F32)
        g_all = -jnp.exp(alog_ref[...]) * _softplus(abv + dtb_ref[...])
        gcum = jnp.dot(m_pad, g_all, preferred_element_type=F32, precision=HIGHEST)
        gbuf[slot, d, 0] = gcum
        gbuf[slot, d, 1] = gcum.T
        gbuf[slot, d, 2, 0:CHUNK, :] = _sigmoid(abv)

    return [gate_piece] + [pair_piece(g) for g in range(3 * D // pair_w)]


def _delta_heads(out_refs, state, ybuf, gbuf, slot):
    ri = lax.broadcasted_iota(jnp.int32, (CHUNK, CHUNK), 0)
    ci = lax.broadcasted_iota(jnp.int32, (CHUNK, CHUNK), 1)
    incl = (ri >= ci, ri <= ci)
    strict = (ri > ci, ri < ci)
    last = (CHUNK - 1, 0)
    eye = (ri == ci).astype(F32)
    keep = lax.broadcasted_iota(jnp.int32, (CHUNK, 2 * CHUNK), 1) >= CHUNK
    zeros_half = jnp.zeros((CHUNK, DK), F32)
    gcum = (gbuf[slot, 0, 0], gbuf[slot, 1, 0])
    gcum_t = (gbuf[slot, 0, 1], gbuf[slot, 1, 1])
    beta_all = (gbuf[slot, 0, 2, 0:CHUNK, :], gbuf[slot, 1, 2, 0:CHUNK, :])
    q_of = lambda c: ybuf[slot, c[0], :, c[1] * DK:(c[1] + 1) * DK]
    k_of = lambda c: ybuf[slot, c[0], :, D + c[1] * DK:D + (c[1] + 1) * DK]
    v_of = lambda c: ybuf[slot, c[0], :, 2 * D + c[1] * DK:2 * D + (c[1] + 1) * DK]
    lane_of = lambda c: c[0] * HEADS + c[1]
    gcol_of = lambda c: gcum[c[0]][0:CHUNK, lane_of(c):lane_of(c) + 1]
    bcol_of = lambda c: beta_all[c[0]][:, 2 * HEADS + lane_of(c):2 * HEADS + lane_of(c) + 1]

    chains = [(d, h) for h in range(HEADS) for d in (0, 1)]
    qmat, qkd, sol, wqh, v_new = {}, {}, {}, {}, {}

    def first(c):
        d = c[0]
        grow = gcum_t[d][lane_of(c):lane_of(c) + 1, 0:CHUNK]
        kn = k_of(c)
        knb = kn.astype(BF16)
        a_mat = _nt_dot((kn * bcol_of(c)).astype(BF16), knb)
        qk = _nt_dot(q_of(c).astype(BF16), knb)
        dec = jnp.where(incl[d], jnp.exp(jnp.where(incl[d], gcol_of(c) - grow, 0.0)), 0.0)
        s0 = jnp.where(strict[d], -a_mat * dec, 0.0)
        qmat[c] = jnp.concatenate([s0, eye], axis=1)
        qkd[c] = (qk * dec).astype(BF16)

    def neumann(c):
        qm = qmat[c]
        qmat[c] = jnp.dot(qm[:, 0:CHUNK].astype(BF16), qm.astype(BF16),
                          preferred_element_type=F32) + jnp.where(keep, qm, 0.0)

    def solve(c):
        bcol = bcol_of(c)
        rhs = jnp.concatenate([v_of(c) * bcol, k_of(c) * (bcol * jnp.exp(gcol_of(c)))], axis=1).astype(BF16)
        sol[c] = jnp.dot(qmat[c][:, CHUNK:2 * CHUNK].astype(BF16), rhs, preferred_element_type=F32)

    def apply_state(c):
        wq = jnp.concatenate([sol[c][:, DK:2 * DK], q_of(c) * jnp.exp(gcol_of(c))], axis=0)
        wqh[c] = jnp.dot(wq.astype(BF16), state[c[0], c[1]].astype(BF16), preferred_element_type=F32)

    def output(c):
        v_new[c] = sol[c][:, 0:DK] - wqh[c][0:CHUNK]
        o = wqh[c][CHUNK:2 * CHUNK] + jnp.dot(qkd[c], v_new[c].astype(BF16), preferred_element_type=F32)
        out_refs[c[0]][:, c[1] * DK:(c[1] + 1) * DK] = o.astype(BF16)

    def update_state(c):
        d = c[0]
        gl = gcum[d][last[d]:last[d] + 1, lane_of(c):lane_of(c) + 1]
        kd = k_of(c) * jnp.exp(gl - gcol_of(c))
        kd_t = jnp.concatenate([kd, zeros_half], axis=0).T
        v_pad = jnp.concatenate([v_new[c], zeros_half], axis=0)
        state[c[0], c[1]] = state[c[0], c[1]] * jnp.exp(gl) + jnp.dot(
            kd_t.astype(BF16), v_pad.astype(BF16), preferred_element_type=F32)

    stages = [first] + [neumann] * NEUMANN_STEPS + [solve, apply_state, output, update_state]
    groups = [chains[g:g + CHAIN_GROUP] for g in range(0, len(chains), CHAIN_GROUP)]
    return [functools.partial(stage, c) for group in groups for stage in stages for c in group]


def _delta_kernel(qf_l, kf_l, vf_l, abf_l, qb_l, kb_l, vb_l, abb_l,
                  qf_c, kf_c, vf_c, abf_c, qb_c, kb_c, vb_c, abb_c,
                  cw_ref, alog_ref, dtb_ref, of_ref, ob_ref, state, prevraw, ybuf, gbuf):
    s = pl.program_id(1)
    is_ctx = s < N_CTX_CHUNKS
    slot_p = s % 2
    slot_h = 1 - slot_p

    @pl.when(s == 0)
    def _():
        state[...] = jnp.zeros(state.shape, F32)
        ybuf[1] = jnp.zeros(ybuf.shape[1:], F32)
        gbuf[1] = jnp.zeros(gbuf.shape[1:], F32)
        gbuf[0] = jnp.zeros(gbuf.shape[1:], F32)

    @pl.when((s == 0) | (s == N_CTX_CHUNKS))
    def _():
        prevraw[...] = jnp.zeros(prevraw.shape, F32)

    prep = (_delta_prep(0, is_ctx, (qf_l, kf_l, vf_l, abf_l), (qf_c, kf_c, vf_c, abf_c), prevraw,
                        cw_ref, alog_ref, dtb_ref, ybuf, gbuf, slot_p)
            + _delta_prep(1, is_ctx, (qb_l, kb_l, vb_l, abb_l), (qb_c, kb_c, vb_c, abb_c), prevraw,
                          cw_ref, alog_ref, dtb_ref, ybuf, gbuf, slot_p))
    heads = _delta_heads((of_ref, ob_ref), state, ybuf, gbuf, slot_h)
    every = len(heads) // len(prep)
    for n, piece in enumerate(heads):
        piece()
        if n % every == every - 1 and prep:
            prep.pop(0)()
    for piece in prep:
        piece()


def _delta(qkv_col, ab_col, o_ctx, ab_ctx, conv_w, alog_l, dtb_l, *, b, s_lat, t_ctx):
    ncol = s_lat // CHUNK
    assert s_lat // GRID_W == CHUNK and t_ctx == N_CTX_CHUNKS * CHUNK

    lat_i = lambda s: jnp.clip(s - N_CTX_CHUNKS, 0, ncol - 1)
    col_f = lambda bb, s: bb * ncol + lat_i(s)
    col_b = lambda bb, s: bb * ncol + ncol - 1 - lat_i(s)
    cch_f = lambda bb, s: bb * N_CTX_CHUNKS + jnp.minimum(s, N_CTX_CHUNKS - 1)
    cch_b = lambda bb, s: bb * N_CTX_CHUNKS + N_CTX_CHUNKS - 1 - jnp.minimum(s, N_CTX_CHUNKS - 1)

    def tok_spec(j, fn):
        return pl.BlockSpec((1, CHUNK, D), lambda bb, s: (j, fn(bb, s), 0))

    ab_spec = lambda fn: pl.BlockSpec((CHUNK, LANES), lambda bb, s: (fn(bb, s), 0))
    full = lambda shape: pl.BlockSpec(shape, lambda bb, s: (0,) * len(shape))

    in_specs = ([tok_spec(0, col_f), tok_spec(1, col_f), tok_spec(2, col_f), ab_spec(col_f),
                 tok_spec(0, col_b), tok_spec(1, col_b), tok_spec(2, col_b), ab_spec(col_b),
                 tok_spec(2, cch_f), tok_spec(3, cch_f), tok_spec(4, cch_f), ab_spec(cch_f),
                 tok_spec(2, cch_b), tok_spec(3, cch_b), tok_spec(4, cch_b), ab_spec(cch_b),
                 full((2, 4, 3 * D)), full((1, LANES)), full((1, LANES))])
    out_sds = jax.ShapeDtypeStruct((b * s_lat, D), BF16)
    out_specs = [pl.BlockSpec((CHUNK, D), lambda bb, s: (col_f(bb, s - 1), 0)),
                 pl.BlockSpec((CHUNK, D), lambda bb, s: (col_b(bb, s - 1), 0))]
    return pl.pallas_call(
        _delta_kernel,
        grid=(b, N_CTX_CHUNKS + ncol + 1),
        in_specs=in_specs,
        out_specs=out_specs,
        out_shape=[out_sds, out_sds],
        scratch_shapes=[pltpu.VMEM((2, HEADS, DK, DK), F32), pltpu.VMEM((2, CONV_SLABS, CHUNK + 8, LANES), F32),
                        pltpu.VMEM((2, 2, CHUNK, 3 * D), F32), pltpu.VMEM((2, 2, 3, 2 * CHUNK, LANES), F32)],
        compiler_params=_cparams(("arbitrary", "arbitrary")),
        name="delta",
    )(qkv_col, qkv_col, qkv_col, ab_col, qkv_col, qkv_col, qkv_col, ab_col,
      o_ctx, o_ctx, o_ctx, ab_ctx, o_ctx, o_ctx, o_ctx, ab_ctx,
      conv_w, alog_l, dtb_l)


def _rms(x, g):
    return x * lax.rsqrt(jnp.mean(x * x, axis=-1, keepdims=True) + EPS) * g


def _store_row_tiles(ref, val, n):
    for s in range(ROW_SUB):
        ref[pl.ds(s, n, stride=ROW_SUB), :] = val[:, s * LANES:(s + 1) * LANES]


def _load_row_tiles(ref, n):
    return jnp.concatenate([ref[pl.ds(s, n, stride=ROW_SUB), :] for s in range(ROW_SUB)], axis=1)


def _scan_rows(ref, b, tm):
    return jnp.concatenate(
        [ref[q, pl.ds(b * 2 + half, tm, stride=SCAN_ROWS), :] for half in (0, 1) for q in range(SCAN_SLABS)],
        axis=1)


def _post_kernel(x_ref, gate_ref, z_ref, grg_ref, gdn_ref, hf_ref, hb_ref, of_ref, ob_ref, mod_ref,
                 wrg_ref, wdn_ref, wout_ref, dng_ref, postg_ref, preg_ref, rw_ref, rb_ref,
                 x1_ref, hf2_ref, topw_ref, route_ref, cnt_ref, carry, *, tm):
    b = pl.program_id(1)

    @pl.when((pl.program_id(0) == 0) & (b == 0))
    def _():
        carry[...] = jnp.zeros(carry.shape, F32)

    gate = gate_ref[0].astype(F32)
    gelu = 0.5 * gate * (1.0 + jnp.tanh(0.7978845608028654 * (gate + 0.044715 * gate * gate * gate)))
    rg_h = _scan_rows(hf_ref, b, tm) + _scan_rows(hb_ref, b, tm)
    y_rg = jnp.dot((rg_h * gelu).astype(BF16), wrg_ref[...], preferred_element_type=F32)

    dn = of_ref[...].astype(F32) + ob_ref[...].astype(F32)
    z = z_ref[0].astype(F32)
    parts = []
    for h in range(HEADS):
        seg = dn[:, h * DK:(h + 1) * DK]
        parts.append(seg * lax.rsqrt(jnp.mean(seg * seg, axis=-1, keepdims=True) + EPS))
    o = jnp.concatenate(parts, axis=1) * dng_ref[...] * (z * _sigmoid(z))
    y_dn = jnp.dot(o.astype(BF16), wdn_ref[...], preferred_element_type=F32)

    merged = _sigmoid(grg_ref[0].astype(F32)) * y_rg + _sigmoid(gdn_ref[0].astype(F32)) * y_dn
    y = jnp.dot(merged.astype(BF16), wout_ref[...], preferred_element_type=F32)

    g1 = mod_ref[pl.ds(b, 1), 2 * D:3 * D]
    sh2 = mod_ref[pl.ds(b, 1), 3 * D:4 * D]
    sc2 = mod_ref[pl.ds(b, 1), 4 * D:5 * D]
    x1 = x_ref[...] + g1 * _rms(y, postg_ref[...])
    x1_ref[...] = x1
    hf2 = _rms(x1, preg_ref[...]) * (1.0 + sc2) + sh2
    _store_row_tiles(hf2_ref, hf2, tm)

    lane = lax.broadcasted_iota(jnp.int32, (tm, LANES), 1)
    lane_f = lane.astype(F32)
    h_hi = hf2.astype(BF16)
    h_lo = (hf2 - h_hi.astype(F32)).astype(BF16)
    rw2 = rw_ref[...]
    hh = jnp.dot(h_hi, rw2, preferred_element_type=F32)
    logits = (hh[:, 0:LANES] + hh[:, LANES:2 * LANES]
              + jnp.dot(h_lo, rw2[:, 0:LANES], preferred_element_type=F32) + rb_ref[...])
    neg = jnp.float32(-jnp.inf)
    lg = jnp.where(lane < N_EXPERTS, logits, neg)
    vals, sels = [], []
    topi = jnp.zeros((tm, LANES), F32)
    onehot = jnp.zeros((tm, LANES), F32)
    for k in range(TOP_K):
        m = jnp.max(lg, axis=-1, keepdims=True)
        idx = jnp.min(jnp.where(lg == m, lane_f, float(LANES)), axis=-1, keepdims=True)
        sel = lane_f == idx
        vals.append(m)
        sels.append(sel)
        lg = jnp.where(sel, neg, lg)
        onehot = onehot + sel.astype(F32)
        topi = topi + jnp.where(lane == k, idx, 0.0)
    es = [jnp.exp(v - vals[0]) for v in vals]
    den = es[0] + es[1] + es[2] + es[3]
    topw = jnp.zeros((tm, LANES), F32)
    for k in range(TOP_K):
        topw = topw + jnp.where(lane == k, es[k] / den, 0.0)

    ri = lax.broadcasted_iota(jnp.int32, (tm, tm), 0)
    ci = lax.broadcasted_iota(jnp.int32, (tm, tm), 1)
    tri = (ri > ci).astype(BF16)
    cum = jnp.dot(tri, onehot.astype(BF16), preferred_element_type=F32) + carry[...]
    rank = jnp.zeros((tm, LANES), F32)
    for k in range(TOP_K):
        rk = jnp.sum(jnp.where(sels[k], cum, 0.0), axis=-1, keepdims=True)
        rank = rank + jnp.where(lane == TOP_K + k, rk, 0.0)
    new_carry = carry[...] + jnp.sum(onehot, axis=0, keepdims=True)
    carry[...] = new_carry
    topw_ref[...] = topw
    route_ref[...] = (topi + rank).T[0:2 * TOP_K, :].astype(jnp.int32)
    cnt_ref[...] = jnp.broadcast_to(new_carry, (8, LANES)).astype(jnp.int32)


def _post(x2d, o_lat, hf, hb, of, ob, mod, wrg, wdn, wout, dng, postg, preg, rw, rb, *, tm, s_lat):
    n = x2d.shape[0]
    tpb = s_lat // tm
    nb = n // s_lat
    kern = functools.partial(_post_kernel, tm=tm)
    tok = pl.BlockSpec((tm, D), lambda t, b: (b * tpb + t, 0))
    oj = lambda j: pl.BlockSpec((1, tm, D), lambda t, b: (j, b * tpb + t, 0))
    hspec = pl.BlockSpec((SCAN_SLABS, tm * SCAN_ROWS, LANES), lambda t, b: (0, t, 0))
    full = lambda shape: pl.BlockSpec(shape, lambda t, b: (0,) * len(shape))
    lane_out = pl.BlockSpec((tm, LANES), lambda t, b: (b * tpb + t, 0))
    return pl.pallas_call(
        kern,
        grid=(tpb, nb),
        in_specs=[tok, oj(1), oj(2), oj(3), oj(4), hspec, hspec, tok, tok, full((8, 6 * D)),
                  full((D, D)), full((D, D)), full((D, D)), full((1, D)), full((1, D)), full((1, D)),
                  full((D, 2 * LANES)), full((1, LANES))],
        out_specs=[tok, pl.BlockSpec((tm * ROW_SUB, LANES), lambda t, b: (b * tpb + t, 0)), lane_out,
                   pl.BlockSpec((2 * TOP_K, tm), lambda t, b: (0, b * tpb + t)),
                   pl.BlockSpec((8, LANES), lambda t, b: (0, 0))],
        out_shape=[jax.ShapeDtypeStruct((n, D), F32), jax.ShapeDtypeStruct((n * ROW_SUB, LANES), F32),
                   jax.ShapeDtypeStruct((n, LANES), F32), jax.ShapeDtypeStruct((2 * TOP_K, n), jnp.int32),
                   jax.ShapeDtypeStruct((8, LANES), jnp.int32)],
        scratch_shapes=[pltpu.VMEM((1, LANES), F32)],
        compiler_params=_cparams(("arbitrary", "arbitrary")),
        name="post_mix",
    )(x2d, o_lat, o_lat, o_lat, o_lat, hf, hb, of, ob, mod, wrg, wdn, wout, dng, postg, preg, rw, rb)


def _idx_copy(pos_hbm, idx, isem, tile, slot):
    n_idx = pos_hbm.shape[1]
    return pltpu.make_async_copy(pos_hbm.at[tile], idx.at[pl.ds(pl.multiple_of(slot * n_idx, n_idx), n_idx)],
                                 isem.at[slot])


def _row_dma_loop(idx, slot, tb, issue, ks=tuple(range(TOP_K))):
    base = slot * (tb * TOP_K)

    def body(rt, c):
        r0 = pl.multiple_of(rt * 8, 8)
        for j in range(8):
            for k in ks:
                issue(r0 + j, k, idx[base + k * tb + r0 + j])
        return c

    lax.fori_loop(0, tb // 8, body, 0)


def _dispatch_kernel(starts_ref, ends_ref, pos_hbm, x_ref, xs_hbm, idx, zbuf, sem, isem, *, tb):
    i = pl.program_id(0)
    n = pl.num_programs(0)
    slot = i % 2

    @pl.when(i == 0)
    def _():
        _idx_copy(pos_hbm, idx, isem, 0, 0).start()
        zbuf[...] = jnp.zeros(zbuf.shape, F32)

        def zero_copy(e):
            r0 = pl.multiple_of(ends_ref[e] - MOE_TM, MOE_TM)
            return pltpu.make_async_copy(zbuf, xs_hbm.at[pl.ds(r0, MOE_TM)], sem)

        for e in range(N_EXPERTS):
            @pl.when(ends_ref[e] > starts_ref[e])
            def _():
                zero_copy(e).start()
        for e in range(N_EXPERTS):
            @pl.when(ends_ref[e] > starts_ref[e])
            def _():
                zero_copy(e).wait()

        def tail_copy(t):
            return pltpu.make_async_copy(zbuf, xs_hbm.at[pl.ds(pl.multiple_of(t * MOE_TM, MOE_TM), MOE_TM)], sem)

        n_used = ends_ref[N_EXPERTS - 1] // MOE_TM
        n_tiles = xs_hbm.shape[0] // MOE_TM

        def tail_start(t, c):
            tail_copy(t).start()
            return c

        def tail_wait(t, c):
            tail_copy(t).wait()
            return c

        lax.fori_loop(n_used, n_tiles, tail_start, 0)
        lax.fori_loop(n_used, n_tiles, tail_wait, 0)

    @pl.when(i + 1 < n)
    def _():
        _idx_copy(pos_hbm, idx, isem, i + 1, 1 - slot).start()

    _idx_copy(pos_hbm, idx, isem, i, slot).wait()

    def issue(r, k, p):
        pltpu.make_async_copy(x_ref.at[r], xs_hbm.at[p], sem).start(priority=k % 2)

    _row_dma_loop(idx, slot, tb, issue)
    for k in range(TOP_K):
        pltpu.make_async_copy(x_ref, xs_hbm.at[pl.ds(0, tb)], sem).wait()


def _dispatch(starts, ends, pos2d, xp, n_rows, *, tb):
    n = xp.shape[0]
    grid_spec = pltpu.PrefetchScalarGridSpec(
        num_scalar_prefetch=2,
        grid=(n // tb,),
        in_specs=[pl.BlockSpec(memory_space=pl.ANY),
                  pl.BlockSpec((tb, ROW_SUB, LANES), lambda i, st, en: (i, 0, 0))],
        out_specs=pl.BlockSpec(memory_space=pl.ANY),
        scratch_shapes=[pltpu.SMEM((2 * tb * TOP_K,), jnp.int32), pltpu.VMEM((MOE_TM, ROW_SUB, LANES), F32),
                        pltpu.SemaphoreType.DMA, pltpu.SemaphoreType.DMA((2,))],
    )
    return pl.pallas_call(
        functools.partial(_dispatch_kernel, tb=tb),
        grid_spec=grid_spec,
        out_shape=jax.ShapeDtypeStruct((n_rows, ROW_SUB, LANES), F32),
        compiler_params=_cparams(("arbitrary",)),
        name="dispatch",
    )(starts, ends, pos2d, xp)


def _expert_kernel(te_ref, nu_ref, xs_ref, w1_ref, b1_ref, w2_ref, b2_ref, ys_ref, w1b, w2b):
    i = pl.program_id(0)

    @pl.when(i < nu_ref[0])
    def _():
        changed = (i == 0) | (te_ref[i] != te_ref[jnp.maximum(i - 1, 0)])

        @pl.when(changed)
        def _():
            w1b[...] = w1_ref[0].astype(BF16)
            w2b[...] = w2_ref[0].astype(BF16)

        x = _load_row_tiles(xs_ref, MOE_TM).astype(BF16)
        hh = jnp.dot(x, w1b[...], preferred_element_type=F32) + b1_ref[0]
        glu = jnp.minimum(hh[:, 0:D], SWIGLU_LIMIT)
        lin = jnp.clip(hh[:, D:2 * D], -SWIGLU_LIMIT, SWIGLU_LIMIT)
        act = glu * _sigmoid(SWIGLU_ALPHA * glu) * (lin + 1.0)
        out = jnp.dot(act.astype(BF16), w2b[...], preferred_element_type=F32) + b2_ref[0]
        _store_row_tiles(ys_ref, out, MOE_TM)

    @pl.when(i >= nu_ref[0])
    def _():
        ys_ref[...] = jnp.zeros(ys_ref.shape, F32)


def _experts(tile_expert, n_used, xs, w1, b1, w2, b2):
    n_rows = xs.shape[0] // ROW_SUB
    n_tiles = n_rows // MOE_TM
    row_map = lambda i, te, nu: (jnp.minimum(i, nu[0] - 1), 0)
    grid_spec = pltpu.PrefetchScalarGridSpec(
        num_scalar_prefetch=2,
        grid=(n_tiles,),
        in_specs=[pl.BlockSpec((MOE_TM * ROW_SUB, LANES), row_map),
                  pl.BlockSpec((1, D, 2 * D), lambda i, te, nu: (te[i], 0, 0)),
                  pl.BlockSpec((1, 1, 2 * D), lambda i, te, nu: (te[i], 0, 0)),
                  pl.BlockSpec((1, D, D), lambda i, te, nu: (te[i], 0, 0)),
                  pl.BlockSpec((1, 1, D), lambda i, te, nu: (te[i], 0, 0))],
        out_specs=pl.BlockSpec((MOE_TM * ROW_SUB, LANES), lambda i, te, nu: (i, 0)),
        scratch_shapes=[pltpu.VMEM((D, 2 * D), BF16), pltpu.VMEM((D, D), BF16)],
    )
    return pl.pallas_call(
        _expert_kernel,
        grid_spec=grid_spec,
        out_shape=jax.ShapeDtypeStruct((n_rows * ROW_SUB, LANES), F32),
        compiler_params=_cparams(("arbitrary",)),
        name="experts",
    )(tile_expert, n_used, xs, w1, b1, w2, b2)


def _final_kernel(pos_hbm, ys_hbm, topw_ref, x1_ref, mod_ref, postg_ref, o_ref, idx, buf, sem, isem,
                  *, tb, tiles_per_batch):
    i = pl.program_id(0)
    n = pl.num_programs(0)
    slot = i % 2

    @pl.when(i == 0)
    def _():
        _idx_copy(pos_hbm, idx, isem, 0, 0).start()

    @pl.when(i + 1 < n)
    def _():
        _idx_copy(pos_hbm, idx, isem, i + 1, 1 - slot).start()

    _idx_copy(pos_hbm, idx, isem, i, slot).wait()

    def issue(r, k, p):
        src = ys_hbm.at[pl.ds(pl.multiple_of(p * ROW_SUB, ROW_SUB), ROW_SUB)]
        dst = buf.at[k, pl.ds(pl.multiple_of(r * ROW_SUB, ROW_SUB), ROW_SUB)]
        pltpu.make_async_copy(src, dst, sem.at[k]).start(priority=k % 2)

    def wait_slot(k):
        pltpu.make_async_copy(ys_hbm.at[pl.ds(0, tb * ROW_SUB)], buf.at[k], sem.at[k]).wait()

    _row_dma_loop(idx, slot, tb, issue, ks=(0, 1))
    _row_dma_loop(idx, slot, tb, issue, ks=(2, 3))
    row = i // tiles_per_batch
    tw = topw_ref[...]
    wait_slot(0)
    wait_slot(1)
    f = _load_row_tiles(buf.at[0], tb) * tw[:, 0:1] + _load_row_tiles(buf.at[1], tb) * tw[:, 1:2]
    wait_slot(2)
    wait_slot(3)
    f = f + _load_row_tiles(buf.at[2], tb) * tw[:, 2:3] + _load_row_tiles(buf.at[3], tb) * tw[:, 3:4]
    g2 = mod_ref[pl.ds(row, 1), 5 * D:6 * D]
    o_ref[...] = x1_ref[...] + g2 * _rms(f, postg_ref[...])


def _final(pos2d, ys, topw, x1, mod, postg, *, tb, s_lat):
    n = x1.shape[0]
    kern = functools.partial(_final_kernel, tb=tb, tiles_per_batch=s_lat // tb)
    return pl.pallas_call(
        kern,
        grid=(n // tb,),
        in_specs=[pl.BlockSpec(memory_space=pl.ANY),
                  pl.BlockSpec(memory_space=pl.ANY),
                  pl.BlockSpec((tb, LANES), lambda i: (i, 0)),
                  pl.BlockSpec((tb, D), lambda i: (i, 0)),
                  pl.BlockSpec((8, 6 * D), lambda i: (0, 0)),
                  pl.BlockSpec((1, D), lambda i: (0, 0))],
        out_specs=pl.BlockSpec((tb, D), lambda i: (i, 0)),
        out_shape=jax.ShapeDtypeStruct((n, D), F32),
        scratch_shapes=[pltpu.SMEM((2 * tb * TOP_K,), jnp.int32), pltpu.VMEM((TOP_K, tb * ROW_SUB, LANES), F32),
                        pltpu.SemaphoreType.DMA((TOP_K,)), pltpu.SemaphoreType.DMA((2,))],
        compiler_params=_cparams(("arbitrary",)),
        name="final",
    )(pos2d, ys, topw, x1, mod, postg)


def _block_diag(w):
    w = w.reshape(2, 4, 4, RG_BW, RG_BW)
    eye = jnp.eye(4, dtype=w.dtype)
    return jnp.einsum('dgiab,ij->dgiajb', w, eye).reshape(2, 4, 4 * RG_BW, 4 * RG_BW)


def _grid_transpose(t, b):
    return t.reshape(b, GRID_W, GRID_W, -1).transpose(0, 2, 1, 3).reshape(t.shape)


def kernel(x, c, ctx, c_ctx, ada_w, ada_b, mix_pre_g, mix_post_g, w_in, rg_conv_w, rg_conv_b, rg_wa, rg_ba,
           rg_wi, rg_bi, rg_lam, rg_w_o, dn_conv_w, dn_a_log, dn_dt_bias, dn_norm_g, dn_w_o, w_out,
           ffn_pre_g, ffn_post_g, router_w, router_b, e_w1, e_b1, e_w2, e_b2):
    b, s_lat, _ = x.shape
    t_ctx = ctx.shape[1]
    depth = ada_w.shape[0]
    assert depth == 1 and b * 2 == SCAN_ROWS and s_lat == GRID_W * GRID_W
    n_lat = b * s_lat
    l = 0

    c8 = jnp.zeros((8, D), F32).at[0:b].set(c).at[b].set(c_ctx)
    mod = _ada_mod(c8, ada_w[l], ada_b[l].reshape(1, -1))

    wi = w_in[l]
    blk = lambda j0: wi[:, j0:j0 + D]
    w_raster = jnp.stack([blk(0), blk(D), blk(5 * D), blk(6 * D + 32), blk(7 * D + 32)]).astype(BF16)
    w_qkv = jnp.stack([blk(2 * D), blk(3 * D), blk(4 * D)]).astype(BF16)
    w_all = jnp.stack([blk(0), blk(D), blk(2 * D), blk(3 * D), blk(4 * D)]).astype(BF16)
    w_ab = jnp.pad(wi[:, 6 * D:6 * D + 32], ((0, 0), (0, LANES - 32))).astype(BF16)
    pre_g = mix_pre_g[l].reshape(1, D)
    x2d = x.reshape(n_lat, D)
    x_col = _grid_transpose(x2d, b)
    tpb = s_lat // INPROJ_TM
    o_lat, _ = _inproj(x2d, pre_g, mod, w_raster, w_ab, tm=INPROJ_TM, tiles_per_batch=tpb, row_offset=0)
    qkv_col, ab_col = _inproj(x_col, pre_g, mod, w_qkv, w_ab, tm=INPROJ_TM, tiles_per_batch=tpb, row_offset=0)
    o_ctx, ab_ctx = _inproj(ctx.reshape(b * t_ctx, D), pre_g, mod, w_all, w_ab, tm=b * t_ctx,
                            tiles_per_batch=1, row_offset=b)

    wblk = jnp.stack([_block_diag(rg_wa[l]), _block_diag(rg_wi[l])], axis=1).astype(BF16)
    rg_args = (rg_conv_w[l], rg_conv_b[l], wblk, rg_ba[l], rg_bi[l], rg_lam[l])
    coef_c = _rg_coef(o_ctx.reshape(-1, t_ctx, D), *rg_args, b=b, tt=t_ctx, reset_first=True)
    coef_l = _rg_coef(o_lat.reshape(-1, s_lat, D), *rg_args, b=b, tt=256, reset_first=False)
    hf, hb = _rg_scan(coef_c, coef_l, t_ctx=t_ctx, t_lat=s_lat, tt=256)

    alog_l = jnp.zeros((1, LANES), F32).at[0, 0:2 * HEADS].set(dn_a_log[l].reshape(-1))
    dtb_l = jnp.zeros((1, LANES), F32).at[0, 0:2 * HEADS].set(dn_dt_bias[l].reshape(-1))
    of, ob = _delta(qkv_col, ab_col, o_ctx, ab_ctx, dn_conv_w[l], alog_l, dtb_l, b=b, s_lat=s_lat, t_ctx=t_ctx)
    of = _grid_transpose(of, b)
    ob = _grid_transpose(ob, b)

    rw = jnp.pad(router_w[l], ((0, 0), (0, LANES - N_EXPERTS)))
    rw_hi = rw.astype(BF16)
    rw = jnp.concatenate([rw_hi, (rw - rw_hi.astype(F32)).astype(BF16)], axis=1)
    rb = jnp.pad(router_b[l].reshape(1, -1), ((0, 0), (0, LANES - N_EXPERTS)))
    x1, hf2, topw, route, cnt = _post(
        x2d, o_lat, hf, hb, of, ob, mod,
        rg_w_o[l].astype(BF16), dn_w_o[l].astype(BF16), w_out[l].astype(BF16),
        jnp.tile(dn_norm_g[l], HEADS).reshape(1, D), mix_post_g[l].reshape(1, D), ffn_pre_g[l].reshape(1, D),
        rw, rb, tm=256, s_lat=s_lat)

    counts = cnt[0, 0:N_EXPERTS]
    padded = ((counts + MOE_TM - 1) // MOE_TM) * MOE_TM
    ends = jnp.cumsum(padded)
    starts = ends - padded
    pos = route[TOP_K:2 * TOP_K]
    for e in range(N_EXPERTS):
        pos = pos + jnp.where(route[0:TOP_K] == e, starts[e], 0)
    pos2d = pos.reshape(TOP_K, n_lat // MOE_TB, MOE_TB).transpose(1, 0, 2).reshape(n_lat // MOE_TB, TOP_K * MOE_TB)
    n_tiles = n_lat * TOP_K // MOE_TM + N_EXPERTS
    tile_start = jnp.arange(n_tiles, dtype=jnp.int32) * MOE_TM
    tile_expert = jnp.minimum(jnp.sum(tile_start[:, None] >= ends[None, :], axis=1), N_EXPERTS - 1).astype(jnp.int32)
    n_used = (ends[-1] // MOE_TM).astype(jnp.int32).reshape(1)

    xs = _dispatch(starts, ends, pos2d, hf2.reshape(n_lat, ROW_SUB, LANES), n_tiles * MOE_TM, tb=MOE_TB)
    xs = xs.reshape(n_tiles * MOE_TM * ROW_SUB, LANES)
    ys = _experts(tile_expert, n_used, xs, e_w1[l], e_b1[l].reshape(N_EXPERTS, 1, -1),
                  e_w2[l], e_b2[l].reshape(N_EXPERTS, 1, -1))
    out = _final(pos2d, ys, topw, x1, mod, ffn_post_g[l].reshape(1, D), tb=MOE_TB, s_lat=s_lat)
    return out.reshape(b, s_lat, D)
```

```python
import functools

import jax
import jax.numpy as jnp
from jax import lax
from jax.experimental import pallas as pl
from jax.experimental.pallas import tpu as pltpu

F32 = jnp.float32
BF16 = jnp.bfloat16
U32 = jnp.uint32
HIGHEST = lax.Precision.HIGHEST

D = 1024
EPS = 1e-6
GRID_W = 64
CHUNK = 64
HEADS = 8
DK = 128
RG_C = 8.0
RG_BW = 64
N_EXPERTS = 32
TOP_K = 4
SWIGLU_LIMIT = 7.0
SWIGLU_ALPHA = 1.702
LANES = 128
INPROJ_TM = 2048
MOE_TM = 512
MOE_TB = 512
ROW_SUB = D // LANES
VMEM_LIMIT = 56 * 1024 * 1024


def _cparams(sem):
    return pltpu.CompilerParams(dimension_semantics=sem, vmem_limit_bytes=VMEM_LIMIT)


def _sigmoid(x):
    return 1.0 / (1.0 + jnp.exp(-x))


def _softplus(y):
    return jnp.maximum(y, 0.0) + jnp.log1p(jnp.exp(-jnp.abs(y)))


def _nt_dot(a, b):
    return lax.dot_general(a, b, (((1,), (1,)), ((), ())), preferred_element_type=F32)


def _ada_kernel(c_ref, w_ref, b_ref, o_ref):
    c = c_ref[...]
    a = c * _sigmoid(c)
    o_ref[...] = jnp.dot(a, w_ref[...], preferred_element_type=F32, precision=HIGHEST) + b_ref[...]


def _ada_mod(c8, ada_w, ada_b):
    n = ada_w.shape[1]
    tn = 1024
    return pl.pallas_call(
        _ada_kernel,
        grid=(n // tn,),
        in_specs=[pl.BlockSpec((8, D), lambda j: (0, 0)),
                  pl.BlockSpec((D, tn), lambda j: (0, j)),
                  pl.BlockSpec((1, tn), lambda j: (0, j))],
        out_specs=pl.BlockSpec((8, tn), lambda j: (0, j)),
        out_shape=jax.ShapeDtypeStruct((8, n), F32),
        compiler_params=_cparams(("arbitrary",)),
        name="ada_mod",
    )(c8, ada_w, ada_b)


def _inproj_kernel(x_ref, g_ref, mod_ref, w_ref, wab_ref, o_ref, ab_ref, h_scr, *, tiles_per_batch, row_offset):
    i = pl.program_id(0)
    j = pl.program_id(1)

    @pl.when(j == 0)
    def _():
        x = x_ref[...]
        y = x * lax.rsqrt(jnp.mean(x * x, axis=-1, keepdims=True) + EPS) * g_ref[...]
        row = row_offset + i // tiles_per_batch
        sh = mod_ref[pl.ds(row, 1), 0:D]
        sc = mod_ref[pl.ds(row, 1), D:2 * D]
        h = (y * (1.0 + sc) + sh).astype(BF16)
        h_scr[...] = h
        ab_ref[...] = jnp.dot(h, wab_ref[...], preferred_element_type=F32)

    o_ref[0] = jnp.dot(h_scr[...], w_ref[0], preferred_element_type=F32).astype(BF16)


def _inproj(x2d, g, mod, w_main, w_ab, *, tm, tiles_per_batch, row_offset):
    n = x2d.shape[0]
    nj = w_main.shape[0]
    kern = functools.partial(_inproj_kernel, tiles_per_batch=tiles_per_batch, row_offset=row_offset)
    return pl.pallas_call(
        kern,
        grid=(n // tm, nj),
        in_specs=[pl.BlockSpec((tm, D), lambda i, j: (i, 0)),
                  pl.BlockSpec((1, D), lambda i, j: (0, 0)),
                  pl.BlockSpec((8, 6 * D), lambda i, j: (0, 0)),
                  pl.BlockSpec((1, D, D), lambda i, j: (j, 0, 0)),
                  pl.BlockSpec((D, LANES), lambda i, j: (0, 0))],
        out_specs=[pl.BlockSpec((1, tm, D), lambda i, j: (j, i, 0)),
                   pl.BlockSpec((tm, LANES), lambda i, j: (i, 0))],
        out_shape=[jax.ShapeDtypeStruct((nj, n, D), BF16),
                   jax.ShapeDtypeStruct((n, LANES), F32)],
        scratch_shapes=[pltpu.VMEM((tm, D), BF16)],
        compiler_params=_cparams(("arbitrary", "arbitrary")),
        name="inproj",
    )(x2d, g, mod, w_main, w_ab)


RG_HALO = 16
SCAN_ROWS = 8
SCAN_SLABS = 4


def _pack_bf16_pair(hi, lo):
    hi_bits = lax.bitcast_convert_type(hi.astype(BF16).astype(F32), U32)
    lo_bits = lax.bitcast_convert_type(lo.astype(BF16).astype(F32), U32)
    return hi_bits | (lo_bits >> 16)


def _rg_coef_kernel(cur_ref, prev_ref, next_ref, cw_ref, cb_ref, wblk_ref, ba_ref, bi_ref, lam_ref,
                    wf_ref, wb_ref, xbuf, *, tt, nt, reset_first):
    i = pl.program_id(0)
    b = pl.program_id(1)
    x = cur_ref[0].astype(F32)
    xp = jnp.where(i > 0, prev_ref[0].astype(F32), 0.0)
    xn = jnp.where(i < nt - 1, next_ref[0].astype(F32), 0.0)
    n_slab = D // LANES
    for j in range(n_slab):
        ls = slice(j * LANES, (j + 1) * LANES)
        xbuf[j, 0:RG_HALO, :] = xp[:, ls]
        xbuf[j, RG_HALO:RG_HALO + tt, :] = x[:, ls]
        xbuf[j, RG_HALO + tt:2 * RG_HALO + tt, :] = xn[:, ls]
    row = lax.broadcasted_iota(jnp.int32, (tt, 1), 0)
    outs = (wf_ref, wb_ref)
    for d in (0, 1):
        w = cw_ref[d]
        cols = []
        for j in range(n_slab):
            ls = slice(j * LANES, (j + 1) * LANES)
            acc = x[:, ls] * w[3:4, ls]
            for sft in (1, 2, 3):
                off = RG_HALO - sft if d == 0 else RG_HALO + sft
                acc = acc + xbuf[j, off:off + tt, :] * w[3 - sft:4 - sft, ls]
            cols.append(acc)
        xc = jnp.concatenate(cols, axis=1) + cb_ref[d:d + 1]
        xcb = xc.astype(BF16)
        pre_a = jnp.concatenate(
            [jnp.dot(xcb[:, g * 256:(g + 1) * 256], wblk_ref[d, 0, g], preferred_element_type=F32)
             for g in range(4)], axis=1) + ba_ref[d:d + 1]
        pre_i = jnp.concatenate(
            [jnp.dot(xcb[:, g * 256:(g + 1) * 256], wblk_ref[d, 1, g], preferred_element_type=F32)
             for g in range(4)], axis=1) + bi_ref[d:d + 1]
        r = _sigmoid(pre_a)
        ig = _sigmoid(pre_i)
        la = r * (-RG_C * _softplus(-lam_ref[d:d + 1]))
        a = jnp.exp(la)
        om = 1.0 - a * a
        mult = jnp.where(om > 0.0, om * lax.rsqrt(om), 0.0)
        if reset_first:
            if d == 0:
                first = (row == 0) & (i == 0)
            else:
                first = (row == tt - 1) & (i == nt - 1)
            mult = jnp.where(first, 1.0, mult)
        word = _pack_bf16_pair(la, mult * ig * xc)
        for half in (0, 1):
            for q in range(SCAN_SLABS):
                c0 = (half * SCAN_SLABS + q) * LANES
                outs[d][q, pl.ds(b * 2 + half, tt, stride=SCAN_ROWS), :] = word[:, c0:c0 + LANES]


def _rg_coef(xr, conv_w, conv_b, wblk, ba, bi, lam, *, b, tt, reset_first):
    t = xr.shape[1]
    nt = t // tt
    hb = tt // RG_HALO
    kern = functools.partial(_rg_coef_kernel, tt=tt, nt=nt, reset_first=reset_first)
    full = lambda shape: pl.BlockSpec(shape, lambda i, bb: (0,) * len(shape))
    out_sds = jax.ShapeDtypeStruct((SCAN_SLABS, t * SCAN_ROWS, LANES), U32)
    out_spec = pl.BlockSpec((SCAN_SLABS, tt * SCAN_ROWS, LANES), lambda i, bb: (0, i, 0))
    return pl.pallas_call(
        kern,
        grid=(nt, b),
        in_specs=[pl.BlockSpec((1, tt, D), lambda i, bb: (bb, i, 0)),
                  pl.BlockSpec((1, RG_HALO, D), lambda i, bb: (bb, jnp.maximum(i * hb - 1, 0), 0)),
                  pl.BlockSpec((1, RG_HALO, D), lambda i, bb: (bb, jnp.minimum((i + 1) * hb, t // RG_HALO - 1), 0)),
                  full((2, 4, D)), full((2, D)), full((2, 2, 4, 256, 256)),
                  full((2, D)), full((2, D)), full((2, D))],
        out_specs=[out_spec, out_spec],
        out_shape=[out_sds, out_sds],
        scratch_shapes=[pltpu.VMEM((D // LANES, tt + 2 * RG_HALO, LANES), F32)],
        compiler_params=_cparams(("arbitrary", "arbitrary")),
        name="rg_coef",
    )(xr, xr, xr, conv_w, conv_b, wblk, ba, bi, lam)


def _scan_step(word, h):
    la = lax.bitcast_convert_type(word & jnp.uint32(0xFFFF0000), F32)
    bb = lax.bitcast_convert_type(word << 16, F32)
    return jnp.exp(la) * h + bb


def _rg_scan_run(wf, wb, hf, hb, carry, n):
    def body(t, c):
        h_f, h_b = c
        r0 = pl.multiple_of(t * SCAN_ROWS, SCAN_ROWS)
        r1 = pl.multiple_of((n - 1 - t) * SCAN_ROWS, SCAN_ROWS)
        new_f, new_b = [], []
        for q in range(SCAN_SLABS):
            f = _scan_step(wf[q, pl.ds(r0, SCAN_ROWS), :], h_f[q])
            g = _scan_step(wb[q, pl.ds(r1, SCAN_ROWS), :], h_b[q])
            if hf is not None:
                hf[q, pl.ds(r0, SCAN_ROWS), :] = f
                hb[q, pl.ds(r1, SCAN_ROWS), :] = g
            new_f.append(f)
            new_b.append(g)
        return tuple(new_f), tuple(new_b)

    init = (tuple(carry[0, q] for q in range(SCAN_SLABS)), tuple(carry[1, q] for q in range(SCAN_SLABS)))
    h_f, h_b = lax.fori_loop(0, n, body, init, unroll=8)
    for q in range(SCAN_SLABS):
        carry[0, q] = h_f[q]
        carry[1, q] = h_b[q]


def _rg_scan_kernel(wf_c, wb_c, wf_l, wb_l, hf_ref, hb_ref, carry, *, t_ctx, tt):
    s = pl.program_id(0)

    @pl.when(s == 0)
    def _():
        carry[...] = jnp.zeros(carry.shape, F32)
        _rg_scan_run(wf_c, wb_c, None, None, carry, t_ctx)

    @pl.when(s > 0)
    def _():
        _rg_scan_run(wf_l, wb_l, hf_ref, hb_ref, carry, tt)


def _rg_scan(coef_c, coef_l, *, t_ctx, t_lat, tt):
    nt = t_lat // tt
    ctx_spec = pl.BlockSpec((SCAN_SLABS, t_ctx * SCAN_ROWS, LANES), lambda s: (0, 0, 0))
    f_spec = pl.BlockSpec((SCAN_SLABS, tt * SCAN_ROWS, LANES), lambda s: (0, jnp.maximum(s - 1, 0), 0))
    b_spec = pl.BlockSpec((SCAN_SLABS, tt * SCAN_ROWS, LANES), lambda s: (0, nt - jnp.maximum(s, 1), 0))
    out_sds = jax.ShapeDtypeStruct((SCAN_SLABS, t_lat * SCAN_ROWS, LANES), F32)
    kern = functools.partial(_rg_scan_kernel, t_ctx=t_ctx, tt=tt)
    return pl.pallas_call(
        kern,
        grid=(nt + 1,),
        in_specs=[ctx_spec, ctx_spec, f_spec, b_spec],
        out_specs=[f_spec, b_spec],
        out_shape=[out_sds, out_sds],
        scratch_shapes=[pltpu.VMEM((2, SCAN_SLABS, SCAN_ROWS, LANES), F32)],
        compiler_params=_cparams(("arbitrary",)),
        name="rg_scan",
    )(coef_c[0], coef_c[1], coef_l[0], coef_l[1])


N_CTX_CHUNKS = 4
CONV_SLABS = 3 * D // LANES
CHAIN_GROUP = 16
PREP_EVERY = 4
NEUMANN_STEPS = 6


def _delta_prep(d, is_ctx, refs_l, refs_c, prevraw, cw_ref, alog_ref, dtb_ref, ybuf, gbuf, slot):
    q_l, k_l, v_l, ab_l = refs_l
    q_c, k_c, v_c, ab_c = refs_c

    srcs = ((q_c, q_l), (k_c, k_l), (v_c, v_l))
    base = 8 if d == 0 else 0
    pair_w = 2 * LANES
    slabs_per_src = D // LANES

    def conv_slab(j):
        rc, rl = srcs[j // slabs_per_src]
        ls = slice((j % slabs_per_src) * LANES, (j % slabs_per_src + 1) * LANES)
        raw = jnp.where(is_ctx, rc[0, :, ls], rl[0, :, ls]).astype(F32)
        prevraw[d, j, base:base + CHUNK, :] = raw
        wj = cw_ref[d, :, j * LANES:(j + 1) * LANES]
        acc = raw * wj[3:4]
        for sft in (1, 2, 3):
            off = base - sft if d == 0 else sft
            acc = acc + prevraw[d, j, off:off + CHUNK, :] * wj[3 - sft:4 - sft]
        if d == 0:
            prevraw[d, j, 0:8, :] = prevraw[d, j, CHUNK:CHUNK + 8, :]
        else:
            prevraw[d, j, CHUNK:CHUNK + 8, :] = prevraw[d, j, 0:8, :]
        return acc * _sigmoid(acc)

    def pair_piece(g):
        def run():
            y2 = jnp.concatenate([conv_slab(2 * g), conv_slab(2 * g + 1)], axis=1)
            if g < 2 * D // pair_w:
                bi = lax.broadcasted_iota(jnp.int32, (pair_w, pair_w), 0) // DK
                bj = lax.broadcasted_iota(jnp.int32, (pair_w, pair_w), 1) // DK
                ss = jnp.dot((y2 * y2).astype(BF16), (bi == bj).astype(BF16), preferred_element_type=F32)
                scale = (DK ** -0.5) if g < D // pair_w else 1.0
                y2 = y2 * lax.rsqrt(ss + EPS) * scale
            ybuf[slot, d, :, g * pair_w:(g + 1) * pair_w] = y2
        return run

    def gate_piece():
        abv = jnp.where(is_ctx, ab_c[...], ab_l[...])
        rp = lax.broadcasted_iota(jnp.int32, (2 * CHUNK, CHUNK), 0)
        cp = lax.broadcasted_iota(jnp.int32, (2 * CHUNK, CHUNK), 1)
        if d == 0:
            m_pad = ((rp >= cp) & (rp < CHUNK)).astype(F32)
        else:
            m_pad = ((rp <= cp) & (rp < CHUNK)).astype(F32)
        g_all = -jnp.exp(alog_ref[...]) * _softplus(abv + dtb_ref[...])
        gcum = jnp.dot(m_pad, g_all, preferred_element_type=F32, precision=HIGHEST)
        gbuf[slot, d, 0] = gcum
        gbuf[slot, d, 1] = gcum.T
        gbuf[slot, d, 2, 0:CHUNK, :] = _sigmoid(abv)

    return [gate_piece] + [pair_piece(g) for g in range(3 * D // pair_w)]


def _delta_heads(out_refs, state, ybuf, gbuf, slot):
    ri = lax.broadcasted_iota(jnp.int32, (CHUNK, CHUNK), 0)
    ci = lax.broadcasted_iota(jnp.int32, (CHUNK, CHUNK), 1)
    incl = (ri >= ci, ri <= ci)
    strict = (ri > ci, ri < ci)
    last = (CHUNK - 1, 0)
    eye = (ri == ci).astype(F32)
    keep = lax.broadcasted_iota(jnp.int32, (CHUNK, 2 * CHUNK), 1) >= CHUNK
    zeros_half = jnp.zeros((CHUNK, DK), F32)
    gcum = (gbuf[slot, 0, 0], gbuf[slot, 1, 0])
    gcum_t = (gbuf[slot, 0, 1], gbuf[slot, 1, 1])
    beta_all = (gbuf[slot, 0, 2, 0:CHUNK, :], gbuf[slot, 1, 2, 0:CHUNK, :])
    q_of = lambda c: ybuf[slot, c[0], :, c[1] * DK:(c[1] + 1) * DK]
    k_of = lambda c: ybuf[slot, c[0], :, D + c[1] * DK:D + (c[1] + 1) * DK]
    v_of = lambda c: ybuf[slot, c[0], :, 2 * D + c[1] * DK:2 * D + (c[1] + 1) * DK]
    lane_of = lambda c: c[0] * HEADS + c[1]
    gcol_of = lambda c: gcum[c[0]][0:CHUNK, lane_of(c):lane_of(c) + 1]
    bcol_of = lambda c: beta_all[c[0]][:, 2 * HEADS + lane_of(c):2 * HEADS + lane_of(c) + 1]

    chains = [(d, h) for h in range(HEADS) for d in (0, 1)]
    qmat, qkd, sol, wqh, v_new = {}, {}, {}, {}, {}

    def first(c):
        d = c[0]
        grow = gcum_t[d][lane_of(c):lane_of(c) + 1, 0:CHUNK]
        kn = k_of(c)
        knb = kn.astype(BF16)
        a_mat = _nt_dot((kn * bcol_of(c)).astype(BF16), knb)
        qk = _nt_dot(q_of(c).astype(BF16), knb)
        dec = jnp.where(incl[d], jnp.exp(jnp.where(incl[d], gcol_of(c) - grow, 0.0)), 0.0)
        s0 = jnp.where(strict[d], -a_mat * dec, 0.0)
        qmat[c] = jnp.concatenate([s0, eye], axis=1)
        qkd[c] = (qk * dec).astype(BF16)

    def neumann(c):
        qm = qmat[c]
        qmat[c] = jnp.dot(qm[:, 0:CHUNK].astype(BF16), qm.astype(BF16),
                          preferred_element_type=F32) + jnp.where(keep, qm, 0.0)

    def solve(c):
        bcol = bcol_of(c)
        rhs = jnp.concatenate([v_of(c) * bcol, k_of(c) * (bcol * jnp.exp(gcol_of(c)))], axis=1).astype(BF16)
        sol[c] = jnp.dot(qmat[c][:, CHUNK:2 * CHUNK].astype(BF16), rhs, preferred_element_type=F32)

    def apply_state(c):
        wq = jnp.concatenate([sol[c][:, DK:2 * DK], q_of(c) * jnp.exp(gcol_of(c))], axis=0)
        wqh[c] = jnp.dot(wq.astype(BF16), state[c[0], c[1]].astype(BF16), preferred_element_type=F32)

    def output(c):
        v_new[c] = sol[c][:, 0:DK] - wqh[c][0:CHUNK]
        o = wqh[c][CHUNK:2 * CHUNK] + jnp.dot(qkd[c], v_new[c].astype(BF16), preferred_element_type=F32)
        out_refs[c[0]][:, c[1] * DK:(c[1] + 1) * DK] = o.astype(BF16)

    def update_state(c):
        d = c[0]
        gl = gcum[d][last[d]:last[d] + 1, lane_of(c):lane_of(c) + 1]
        kd = k_of(c) * jnp.exp(gl - gcol_of(c))
        kd_t = jnp.concatenate([kd, zeros_half], axis=0).T
        v_pad = jnp.concatenate([v_new[c], zeros_half], axis=0)
        state[c[0], c[1]] = state[c[0], c[1]] * jnp.exp(gl) + jnp.dot(
            kd_t.astype(BF16), v_pad.astype(BF16), preferred_element_type=F32)

    stages = [first] + [neumann] * NEUMANN_STEPS + [solve, apply_state, output, update_state]
    groups = [chains[g:g + CHAIN_GROUP] for g in range(0, len(chains), CHAIN_GROUP)]
    return [functools.partial(stage, c) for group in groups for stage in stages for c in group]


def _delta_kernel(qf_l, kf_l, vf_l, abf_l, qb_l, kb_l, vb_l, abb_l,
                  qf_c, kf_c, vf_c, abf_c, qb_c, kb_c, vb_c, abb_c,
                  cw_ref, alog_ref, dtb_ref, of_ref, ob_ref, state, prevraw, ybuf, gbuf):
    s = pl.program_id(1)
    is_ctx = s < N_CTX_CHUNKS
    slot_p = s % 2
    slot_h = 1 - slot_p

    @pl.when(s == 0)
    def _():
        state[...] = jnp.zeros(state.shape, F32)
        ybuf[1] = jnp.zeros(ybuf.shape[1:], F32)
        gbuf[1] = jnp.zeros(gbuf.shape[1:], F32)
        gbuf[0] = jnp.zeros(gbuf.shape[1:], F32)

    @pl.when((s == 0) | (s == N_CTX_CHUNKS))
    def _():
        prevraw[...] = jnp.zeros(prevraw.shape, F32)

    prep = (_delta_prep(0, is_ctx, (qf_l, kf_l, vf_l, abf_l), (qf_c, kf_c, vf_c, abf_c), prevraw,
                        cw_ref, alog_ref, dtb_ref, ybuf, gbuf, slot_p)
            + _delta_prep(1, is_ctx, (qb_l, kb_l, vb_l, abb_l), (qb_c, kb_c, vb_c, abb_c), prevraw,
                          cw_ref, alog_ref, dtb_ref, ybuf, gbuf, slot_p))
    heads = _delta_heads((of_ref, ob_ref), state, ybuf, gbuf, slot_h)
    every = PREP_EVERY
    for n, piece in enumerate(heads):
        piece()
        if n % every == every - 1 and prep:
            prep.pop(0)()
    for piece in prep:
        piece()


def _delta(qkv_col, ab_col, o_ctx, ab_ctx, conv_w, alog_l, dtb_l, *, b, s_lat, t_ctx):
    ncol = s_lat // CHUNK
    assert s_lat // GRID_W == CHUNK and t_ctx == N_CTX_CHUNKS * CHUNK

    lat_i = lambda s: jnp.clip(s - N_CTX_CHUNKS, 0, ncol - 1)
    col_f = lambda bb, s: bb * ncol + lat_i(s)
    col_b = lambda bb, s: bb * ncol + ncol - 1 - lat_i(s)
    cch_f = lambda bb, s: bb * N_CTX_CHUNKS + jnp.minimum(s, N_CTX_CHUNKS - 1)
    cch_b = lambda bb, s: bb * N_CTX_CHUNKS + N_CTX_CHUNKS - 1 - jnp.minimum(s, N_CTX_CHUNKS - 1)

    def tok_spec(j, fn):
        return pl.BlockSpec((1, CHUNK, D), lambda bb, s: (j, fn(bb, s), 0))

    ab_spec = lambda fn: pl.BlockSpec((CHUNK, LANES), lambda bb, s: (fn(bb, s), 0))
    full = lambda shape: pl.BlockSpec(shape, lambda bb, s: (0,) * len(shape))

    in_specs = ([tok_spec(0, col_f), tok_spec(1, col_f), tok_spec(2, col_f), ab_spec(col_f),
                 tok_spec(0, col_b), tok_spec(1, col_b), tok_spec(2, col_b), ab_spec(col_b),
                 tok_spec(2, cch_f), tok_spec(3, cch_f), tok_spec(4, cch_f), ab_spec(cch_f),
                 tok_spec(2, cch_b), tok_spec(3, cch_b), tok_spec(4, cch_b), ab_spec(cch_b),
                 full((2, 4, 3 * D)), full((1, LANES)), full((1, LANES))])
    out_sds = jax.ShapeDtypeStruct((b * s_lat, D), BF16)
    out_specs = [pl.BlockSpec((CHUNK, D), lambda bb, s: (col_f(bb, s - 1), 0)),
                 pl.BlockSpec((CHUNK, D), lambda bb, s: (col_b(bb, s - 1), 0))]
    return pl.pallas_call(
        _delta_kernel,
        grid=(b, N_CTX_CHUNKS + ncol + 1),
        in_specs=in_specs,
        out_specs=out_specs,
        out_shape=[out_sds, out_sds],
        scratch_shapes=[pltpu.VMEM((2, HEADS, DK, DK), F32), pltpu.VMEM((2, CONV_SLABS, CHUNK + 8, LANES), F32),
                        pltpu.VMEM((2, 2, CHUNK, 3 * D), F32), pltpu.VMEM((2, 2, 3, 2 * CHUNK, LANES), F32)],
        compiler_params=_cparams(("arbitrary", "arbitrary")),
        name="delta",
    )(qkv_col, qkv_col, qkv_col, ab_col, qkv_col, qkv_col, qkv_col, ab_col,
      o_ctx, o_ctx, o_ctx, ab_ctx, o_ctx, o_ctx, o_ctx, ab_ctx,
      conv_w, alog_l, dtb_l)


def _rms(x, g):
    return x * lax.rsqrt(jnp.mean(x * x, axis=-1, keepdims=True) + EPS) * g


def _store_row_tiles(ref, val, n):
    for s in range(ROW_SUB):
        ref[pl.ds(s, n, stride=ROW_SUB), :] = val[:, s * LANES:(s + 1) * LANES]


def _load_row_tiles(ref, n):
    return jnp.concatenate([ref[pl.ds(s, n, stride=ROW_SUB), :] for s in range(ROW_SUB)], axis=1)


def _scan_rows(ref, b, tm):
    return jnp.concatenate(
        [ref[q, pl.ds(b * 2 + half, tm, stride=SCAN_ROWS), :] for half in (0, 1) for q in range(SCAN_SLABS)],
        axis=1)


def _post_kernel(x_ref, gate_ref, z_ref, grg_ref, gdn_ref, hf_ref, hb_ref, of_ref, ob_ref, mod_ref,
                 wrg_ref, wdn_ref, wout_ref, dng_ref, postg_ref, preg_ref, rw_ref, rb_ref,
                 x1_ref, hf2_ref, topw_ref, route_ref, cnt_ref, carry, *, tm):
    b = pl.program_id(1)

    @pl.when((pl.program_id(0) == 0) & (b == 0))
    def _():
        carry[...] = jnp.zeros(carry.shape, F32)

    gate = gate_ref[0].astype(F32)
    gelu = 0.5 * gate * (1.0 + jnp.tanh(0.7978845608028654 * (gate + 0.044715 * gate * gate * gate)))
    rg_h = _scan_rows(hf_ref, b, tm) + _scan_rows(hb_ref, b, tm)
    y_rg = jnp.dot((rg_h * gelu).astype(BF16), wrg_ref[...], preferred_element_type=F32)

    dn = of_ref[...].astype(F32) + ob_ref[...].astype(F32)
    z = z_ref[0].astype(F32)
    parts = []
    for h in range(HEADS):
        seg = dn[:, h * DK:(h + 1) * DK]
        parts.append(seg * lax.rsqrt(jnp.mean(seg * seg, axis=-1, keepdims=True) + EPS))
    o = jnp.concatenate(parts, axis=1) * dng_ref[...] * (z * _sigmoid(z))
    y_dn = jnp.dot(o.astype(BF16), wdn_ref[...], preferred_element_type=F32)

    merged = _sigmoid(grg_ref[0].astype(F32)) * y_rg + _sigmoid(gdn_ref[0].astype(F32)) * y_dn
    y = jnp.dot(merged.astype(BF16), wout_ref[...], preferred_element_type=F32)

    g1 = mod_ref[pl.ds(b, 1), 2 * D:3 * D]
    sh2 = mod_ref[pl.ds(b, 1), 3 * D:4 * D]
    sc2 = mod_ref[pl.ds(b, 1), 4 * D:5 * D]
    x1 = x_ref[...] + g1 * _rms(y, postg_ref[...])
    x1_ref[...] = x1
    hf2 = _rms(x1, preg_ref[...]) * (1.0 + sc2) + sh2
    _store_row_tiles(hf2_ref, hf2, tm)

    lane = lax.broadcasted_iota(jnp.int32, (tm, LANES), 1)
    lane_f = lane.astype(F32)
    h_hi = hf2.astype(BF16)
    h_lo = (hf2 - h_hi.astype(F32)).astype(BF16)
    rw2 = rw_ref[...]
    hh = jnp.dot(h_hi, rw2, preferred_element_type=F32)
    logits = (hh[:, 0:LANES] + hh[:, LANES:2 * LANES]
              + jnp.dot(h_lo, rw2[:, 0:LANES], preferred_element_type=F32) + rb_ref[...])
    neg = jnp.float32(-jnp.inf)
    lg = jnp.where(lane < N_EXPERTS, logits, neg)
    vals, sels = [], []
    topi = jnp.zeros((tm, LANES), F32)
    onehot = jnp.zeros((tm, LANES), F32)
    for k in range(TOP_K):
        m = jnp.max(lg, axis=-1, keepdims=True)
        idx = jnp.min(jnp.where(lg == m, lane_f, float(LANES)), axis=-1, keepdims=True)
        sel = lane_f == idx
        vals.append(m)
        sels.append(sel)
        lg = jnp.where(sel, neg, lg)
        onehot = onehot + sel.astype(F32)
        topi = topi + jnp.where(lane == k, idx, 0.0)
    es = [jnp.exp(v - vals[0]) for v in vals]
    den = es[0] + es[1] + es[2] + es[3]
    topw = jnp.zeros((tm, LANES), F32)
    for k in range(TOP_K):
        topw = topw + jnp.where(lane == k, es[k] / den, 0.0)

    ri = lax.broadcasted_iota(jnp.int32, (tm, tm), 0)
    ci = lax.broadcasted_iota(jnp.int32, (tm, tm), 1)
    tri = (ri > ci).astype(BF16)
    cum = jnp.dot(tri, onehot.astype(BF16), preferred_element_type=F32) + carry[...]
    rank = jnp.zeros((tm, LANES), F32)
    for k in range(TOP_K):
        rk = jnp.sum(jnp.where(sels[k], cum, 0.0), axis=-1, keepdims=True)
        rank = rank + jnp.where(lane == TOP_K + k, rk, 0.0)
    new_carry = carry[...] + jnp.sum(onehot, axis=0, keepdims=True)
    carry[...] = new_carry
    topw_ref[...] = topw
    route_ref[...] = (topi + rank).T[0:2 * TOP_K, :].astype(jnp.int32)
    cnt_ref[...] = jnp.broadcast_to(new_carry, (8, LANES)).astype(jnp.int32)


def _post(x2d, o_lat, hf, hb, of, ob, mod, wrg, wdn, wout, dng, postg, preg, rw, rb, *, tm, s_lat):
    n = x2d.shape[0]
    tpb = s_lat // tm
    nb = n // s_lat
    kern = functools.partial(_post_kernel, tm=tm)
    tok = pl.BlockSpec((tm, D), lambda t, b: (b * tpb + t, 0))
    oj = lambda j: pl.BlockSpec((1, tm, D), lambda t, b: (j, b * tpb + t, 0))
    hspec = pl.BlockSpec((SCAN_SLABS, tm * SCAN_ROWS, LANES), lambda t, b: (0, t, 0))
    full = lambda shape: pl.BlockSpec(shape, lambda t, b: (0,) * len(shape))
    lane_out = pl.BlockSpec((tm, LANES), lambda t, b: (b * tpb + t, 0))
    return pl.pallas_call(
        kern,
        grid=(tpb, nb),
        in_specs=[tok, oj(1), oj(2), oj(3), oj(4), hspec, hspec, tok, tok, full((8, 6 * D)),
                  full((D, D)), full((D, D)), full((D, D)), full((1, D)), full((1, D)), full((1, D)),
                  full((D, 2 * LANES)), full((1, LANES))],
        out_specs=[tok, pl.BlockSpec((tm * ROW_SUB, LANES), lambda t, b: (b * tpb + t, 0)), lane_out,
                   pl.BlockSpec((2 * TOP_K, tm), lambda t, b: (0, b * tpb + t)),
                   pl.BlockSpec((8, LANES), lambda t, b: (0, 0))],
        out_shape=[jax.ShapeDtypeStruct((n, D), F32), jax.ShapeDtypeStruct((n * ROW_SUB, LANES), F32),
                   jax.ShapeDtypeStruct((n, LANES), F32), jax.ShapeDtypeStruct((2 * TOP_K, n), jnp.int32),
                   jax.ShapeDtypeStruct((8, LANES), jnp.int32)],
        scratch_shapes=[pltpu.VMEM((1, LANES), F32)],
        compiler_params=_cparams(("arbitrary", "arbitrary")),
        name="post_mix",
    )(x2d, o_lat, o_lat, o_lat, o_lat, hf, hb, of, ob, mod, wrg, wdn, wout, dng, postg, preg, rw, rb)


def _idx_copy(pos_hbm, idx, isem, tile, slot):
    n_idx = pos_hbm.shape[1]
    return pltpu.make_async_copy(pos_hbm.at[tile], idx.at[pl.ds(pl.multiple_of(slot * n_idx, n_idx), n_idx)],
                                 isem.at[slot])


def _row_dma_loop(idx, slot, tb, issue, ks=tuple(range(TOP_K))):
    base = slot * (tb * TOP_K)

    def body(rt, c):
        r0 = pl.multiple_of(rt * 8, 8)
        for j in range(8):
            for k in ks:
                issue(r0 + j, k, idx[base + k * tb + r0 + j])
        return c

    lax.fori_loop(0, tb // 8, body, 0)


def _dispatch_kernel(starts_ref, ends_ref, pos_hbm, x_ref, xs_hbm, idx, zbuf, sem, isem, *, tb):
    i = pl.program_id(0)
    n = pl.num_programs(0)
    slot = i % 2

    @pl.when(i == 0)
    def _():
        _idx_copy(pos_hbm, idx, isem, 0, 0).start()
        zbuf[...] = jnp.zeros(zbuf.shape, F32)

        def zero_copy(e):
            r0 = pl.multiple_of(ends_ref[e] - MOE_TM, MOE_TM)
            return pltpu.make_async_copy(zbuf, xs_hbm.at[pl.ds(r0, MOE_TM)], sem)

        for e in range(N_EXPERTS):
            @pl.when(ends_ref[e] > starts_ref[e])
            def _():
                zero_copy(e).start()
        for e in range(N_EXPERTS):
            @pl.when(ends_ref[e] > starts_ref[e])
            def _():
                zero_copy(e).wait()

        def tail_copy(t):
            return pltpu.make_async_copy(zbuf, xs_hbm.at[pl.ds(pl.multiple_of(t * MOE_TM, MOE_TM), MOE_TM)], sem)

        n_used = ends_ref[N_EXPERTS - 1] // MOE_TM
        n_tiles = xs_hbm.shape[0] // MOE_TM

        def tail_start(t, c):
            tail_copy(t).start()
            return c

        def tail_wait(t, c):
            tail_copy(t).wait()
            return c

        lax.fori_loop(n_used, n_tiles, tail_start, 0)
        lax.fori_loop(n_used, n_tiles, tail_wait, 0)

    @pl.when(i + 1 < n)
    def _():
        _idx_copy(pos_hbm, idx, isem, i + 1, 1 - slot).start()

    _idx_copy(pos_hbm, idx, isem, i, slot).wait()

    def issue(r, k, p):
        pltpu.make_async_copy(x_ref.at[r], xs_hbm.at[p], sem).start(priority=k % 2)

    _row_dma_loop(idx, slot, tb, issue)
    for k in range(TOP_K):
        pltpu.make_async_copy(x_ref, xs_hbm.at[pl.ds(0, tb)], sem).wait()


def _dispatch(starts, ends, pos2d, xp, n_rows, *, tb):
    n = xp.shape[0]
    grid_spec = pltpu.PrefetchScalarGridSpec(
        num_scalar_prefetch=2,
        grid=(n // tb,),
        in_specs=[pl.BlockSpec(memory_space=pl.ANY),
                  pl.BlockSpec((tb, ROW_SUB, LANES), lambda i, st, en: (i, 0, 0))],
        out_specs=pl.BlockSpec(memory_space=pl.ANY),
        scratch_shapes=[pltpu.SMEM((2 * tb * TOP_K,), jnp.int32), pltpu.VMEM((MOE_TM, ROW_SUB, LANES), F32),
                        pltpu.SemaphoreType.DMA, pltpu.SemaphoreType.DMA((2,))],
    )
    return pl.pallas_call(
        functools.partial(_dispatch_kernel, tb=tb),
        grid_spec=grid_spec,
        out_shape=jax.ShapeDtypeStruct((n_rows, ROW_SUB, LANES), F32),
        compiler_params=_cparams(("arbitrary",)),
        name="dispatch",
    )(starts, ends, pos2d, xp)


def _expert_kernel(te_ref, nu_ref, xs_ref, w1_ref, b1_ref, w2_ref, b2_ref, ys_ref, w1b, w2b):
    i = pl.program_id(0)

    @pl.when(i < nu_ref[0])
    def _():
        changed = (i == 0) | (te_ref[i] != te_ref[jnp.maximum(i - 1, 0)])

        @pl.when(changed)
        def _():
            w1b[...] = w1_ref[0].astype(BF16)
            w2b[...] = w2_ref[0].astype(BF16)

        x = _load_row_tiles(xs_ref, MOE_TM).astype(BF16)
        hh = jnp.dot(x, w1b[...], preferred_element_type=F32) + b1_ref[0]
        glu = jnp.minimum(hh[:, 0:D], SWIGLU_LIMIT)
        lin = jnp.clip(hh[:, D:2 * D], -SWIGLU_LIMIT, SWIGLU_LIMIT)
        act = glu * _sigmoid(SWIGLU_ALPHA * glu) * (lin + 1.0)
        out = jnp.dot(act.astype(BF16), w2b[...], preferred_element_type=F32) + b2_ref[0]
        _store_row_tiles(ys_ref, out, MOE_TM)

    @pl.when(i >= nu_ref[0])
    def _():
        ys_ref[...] = jnp.zeros(ys_ref.shape, F32)


def _experts(tile_expert, n_used, xs, w1, b1, w2, b2):
    n_rows = xs.shape[0] // ROW_SUB
    n_tiles = n_rows // MOE_TM
    row_map = lambda i, te, nu: (jnp.minimum(i, nu[0] - 1), 0)
    grid_spec = pltpu.PrefetchScalarGridSpec(
        num_scalar_prefetch=2,
        grid=(n_tiles,),
        in_specs=[pl.BlockSpec((MOE_TM * ROW_SUB, LANES), row_map),
                  pl.BlockSpec((1, D, 2 * D), lambda i, te, nu: (te[i], 0, 0)),
                  pl.BlockSpec((1, 1, 2 * D), lambda i, te, nu: (te[i], 0, 0)),
                  pl.BlockSpec((1, D, D), lambda i, te, nu: (te[i], 0, 0)),
                  pl.BlockSpec((1, 1, D), lambda i, te, nu: (te[i], 0, 0))],
        out_specs=pl.BlockSpec((MOE_TM * ROW_SUB, LANES), lambda i, te, nu: (i, 0)),
        scratch_shapes=[pltpu.VMEM((D, 2 * D), BF16), pltpu.VMEM((D, D), BF16)],
    )
    return pl.pallas_call(
        _expert_kernel,
        grid_spec=grid_spec,
        out_shape=jax.ShapeDtypeStruct((n_rows * ROW_SUB, LANES), F32),
        compiler_params=_cparams(("arbitrary",)),
        name="experts",
    )(tile_expert, n_used, xs, w1, b1, w2, b2)


def _final_kernel(pos_hbm, ys_hbm, topw_ref, x1_ref, mod_ref, postg_ref, o_ref, idx, buf, sem, isem,
                  *, tb, tiles_per_batch):
    i = pl.program_id(0)
    n = pl.num_programs(0)
    slot = i % 2

    @pl.when(i == 0)
    def _():
        _idx_copy(pos_hbm, idx, isem, 0, 0).start()

    @pl.when(i + 1 < n)
    def _():
        _idx_copy(pos_hbm, idx, isem, i + 1, 1 - slot).start()

    _idx_copy(pos_hbm, idx, isem, i, slot).wait()

    def issue(r, k, p):
        src = ys_hbm.at[pl.ds(pl.multiple_of(p * ROW_SUB, ROW_SUB), ROW_SUB)]
        dst = buf.at[k, pl.ds(pl.multiple_of(r * ROW_SUB, ROW_SUB), ROW_SUB)]
        pltpu.make_async_copy(src, dst, sem.at[k]).start(priority=k % 2)

    def wait_slot(k):
        pltpu.make_async_copy(ys_hbm.at[pl.ds(0, tb * ROW_SUB)], buf.at[k], sem.at[k]).wait()

    _row_dma_loop(idx, slot, tb, issue, ks=(0, 1))
    _row_dma_loop(idx, slot, tb, issue, ks=(2, 3))
    row = i // tiles_per_batch
    tw = topw_ref[...]
    wait_slot(0)
    wait_slot(1)
    f = _load_row_tiles(buf.at[0], tb) * tw[:, 0:1] + _load_row_tiles(buf.at[1], tb) * tw[:, 1:2]
    wait_slot(2)
    wait_slot(3)
    f = f + _load_row_tiles(buf.at[2], tb) * tw[:, 2:3] + _load_row_tiles(buf.at[3], tb) * tw[:, 3:4]
    g2 = mod_ref[pl.ds(row, 1), 5 * D:6 * D]
    o_ref[...] = x1_ref[...] + g2 * _rms(f, postg_ref[...])


def _final(pos2d, ys, topw, x1, mod, postg, *, tb, s_lat):
    n = x1.shape[0]
    kern = functools.partial(_final_kernel, tb=tb, tiles_per_batch=s_lat // tb)
    return pl.pallas_call(
        kern,
        grid=(n // tb,),
        in_specs=[pl.BlockSpec(memory_space=pl.ANY),
                  pl.BlockSpec(memory_space=pl.ANY),
                  pl.BlockSpec((tb, LANES), lambda i: (i, 0)),
                  pl.BlockSpec((tb, D), lambda i: (i, 0)),
                  pl.BlockSpec((8, 6 * D), lambda i: (0, 0)),
                  pl.BlockSpec((1, D), lambda i: (0, 0))],
        out_specs=pl.BlockSpec((tb, D), lambda i: (i, 0)),
        out_shape=jax.ShapeDtypeStruct((n, D), F32),
        scratch_shapes=[pltpu.SMEM((2 * tb * TOP_K,), jnp.int32), pltpu.VMEM((TOP_K, tb * ROW_SUB, LANES), F32),
                        pltpu.SemaphoreType.DMA((TOP_K,)), pltpu.SemaphoreType.DMA((2,))],
        compiler_params=_cparams(("arbitrary",)),
        name="final",
    )(pos2d, ys, topw, x1, mod, postg)


def _block_diag(w):
    w = w.reshape(2, 4, 4, RG_BW, RG_BW)
    eye = jnp.eye(4, dtype=w.dtype)
    return jnp.einsum('dgiab,ij->dgiajb', w, eye).reshape(2, 4, 4 * RG_BW, 4 * RG_BW)


def _grid_transpose(t, b):
    return t.reshape(b, GRID_W, GRID_W, -1).transpose(0, 2, 1, 3).reshape(t.shape)


def kernel(x, c, ctx, c_ctx, ada_w, ada_b, mix_pre_g, mix_post_g, w_in, rg_conv_w, rg_conv_b, rg_wa, rg_ba,
           rg_wi, rg_bi, rg_lam, rg_w_o, dn_conv_w, dn_a_log, dn_dt_bias, dn_norm_g, dn_w_o, w_out,
           ffn_pre_g, ffn_post_g, router_w, router_b, e_w1, e_b1, e_w2, e_b2):
    b, s_lat, _ = x.shape
    t_ctx = ctx.shape[1]
    depth = ada_w.shape[0]
    assert depth == 1 and b * 2 == SCAN_ROWS and s_lat == GRID_W * GRID_W
    n_lat = b * s_lat
    l = 0

    c8 = jnp.zeros((8, D), F32).at[0:b].set(c).at[b].set(c_ctx)
    mod = _ada_mod(c8, ada_w[l], ada_b[l].reshape(1, -1))

    wi = w_in[l]
    blk = lambda j0: wi[:, j0:j0 + D]
    w_raster = jnp.stack([blk(0), blk(D), blk(5 * D), blk(6 * D + 32), blk(7 * D + 32)]).astype(BF16)
    w_qkv = jnp.stack([blk(2 * D), blk(3 * D), blk(4 * D)]).astype(BF16)
    w_all = jnp.stack([blk(0), blk(D), blk(2 * D), blk(3 * D), blk(4 * D)]).astype(BF16)
    w_ab = jnp.pad(wi[:, 6 * D:6 * D + 32], ((0, 0), (0, LANES - 32))).astype(BF16)
    pre_g = mix_pre_g[l].reshape(1, D)
    x2d = x.reshape(n_lat, D)
    x_col = _grid_transpose(x2d, b)
    tpb = s_lat // INPROJ_TM
    o_lat, _ = _inproj(x2d, pre_g, mod, w_raster, w_ab, tm=INPROJ_TM, tiles_per_batch=tpb, row_offset=0)
    qkv_col, ab_col = _inproj(x_col, pre_g, mod, w_qkv, w_ab, tm=INPROJ_TM, tiles_per_batch=tpb, row_offset=0)
    o_ctx, ab_ctx = _inproj(ctx.reshape(b * t_ctx, D), pre_g, mod, w_all, w_ab, tm=b * t_ctx,
                            tiles_per_batch=1, row_offset=b)

    wblk = jnp.stack([_block_diag(rg_wa[l]), _block_diag(rg_wi[l])], axis=1).astype(BF16)
    rg_args = (rg_conv_w[l], rg_conv_b[l], wblk, rg_ba[l], rg_bi[l], rg_lam[l])
    coef_c = _rg_coef(o_ctx.reshape(-1, t_ctx, D), *rg_args, b=b, tt=t_ctx, reset_first=True)
    coef_l = _rg_coef(o_lat.reshape(-1, s_lat, D), *rg_args, b=b, tt=256, reset_first=False)
    hf, hb = _rg_scan(coef_c, coef_l, t_ctx=t_ctx, t_lat=s_lat, tt=256)

    alog_l = jnp.zeros((1, LANES), F32).at[0, 0:2 * HEADS].set(dn_a_log[l].reshape(-1))
    dtb_l = jnp.zeros((1, LANES), F32).at[0, 0:2 * HEADS].set(dn_dt_bias[l].reshape(-1))
    of, ob = _delta(qkv_col, ab_col, o_ctx, ab_ctx, dn_conv_w[l], alog_l, dtb_l, b=b, s_lat=s_lat, t_ctx=t_ctx)
    of = _grid_transpose(of, b)
    ob = _grid_transpose(ob, b)

    rw = jnp.pad(router_w[l], ((0, 0), (0, LANES - N_EXPERTS)))
    rw_hi = rw.astype(BF16)
    rw = jnp.concatenate([rw_hi, (rw - rw_hi.astype(F32)).astype(BF16)], axis=1)
    rb = jnp.pad(router_b[l].reshape(1, -1), ((0, 0), (0, LANES - N_EXPERTS)))
    x1, hf2, topw, route, cnt = _post(
        x2d, o_lat, hf, hb, of, ob, mod,
        rg_w_o[l].astype(BF16), dn_w_o[l].astype(BF16), w_out[l].astype(BF16),
        jnp.tile(dn_norm_g[l], HEADS).reshape(1, D), mix_post_g[l].reshape(1, D), ffn_pre_g[l].reshape(1, D),
        rw, rb, tm=256, s_lat=s_lat)

    counts = cnt[0, 0:N_EXPERTS]
    padded = ((counts + MOE_TM - 1) // MOE_TM) * MOE_TM
    ends = jnp.cumsum(padded)
    starts = ends - padded
    pos = route[TOP_K:2 * TOP_K]
    for e in range(N_EXPERTS):
        pos = pos + jnp.where(route[0:TOP_K] == e, starts[e], 0)
    pos2d = pos.reshape(TOP_K, n_lat // MOE_TB, MOE_TB).transpose(1, 0, 2).reshape(n_lat // MOE_TB, TOP_K * MOE_TB)
    n_tiles = n_lat * TOP_K // MOE_TM + N_EXPERTS
    tile_start = jnp.arange(n_tiles, dtype=jnp.int32) * MOE_TM
    tile_expert = jnp.minimum(jnp.sum(tile_start[:, None] >= ends[None, :], axis=1), N_EXPERTS - 1).astype(jnp.int32)
    n_used = (ends[-1] // MOE_TM).astype(jnp.int32).reshape(1)

    xs = _dispatch(starts, ends, pos2d, hf2.reshape(n_lat, ROW_SUB, LANES), n_tiles * MOE_TM, tb=MOE_TB)
    xs = xs.reshape(n_tiles * MOE_TM * ROW_SUB, LANES)
    ys = _experts(tile_expert, n_used, xs, e_w1[l], e_b1[l].reshape(N_EXPERTS, 1, -1),
                  e_w2[l], e_b2[l].reshape(N_EXPERTS, 1, -1))
    out = _final(pos2d, ys, topw, x1, mod, ffn_post_g[l].reshape(1, D), tb=MOE_TB, s_lat=s_lat)
    return out.reshape(b, s_lat, D)
```
